```python
import math
import jax, jax.numpy as jnp
from jax import lax
import numpy as np

D_MODEL = 1024
BATCH = 8
SEQ = 2048
DEPTH = 2
DEC_BATCH = 128
DEC_SEQ = 4
PAST_LEN = 16384
PAGE_SIZE = 128

N_AB = (DEPTH + 1) // 2
N_C = DEPTH // 2
CHUNK = 64
EPS = 1e-6
D_FF = 2816

HGRN_HEADS = 4
HGRN_HEAD_DIM = 128
HGRN_WIDTH = HGRN_HEADS * HGRN_HEAD_DIM

SSM_HEADS = 8
SSM_HEAD_DIM = 64
SSM_INNER = SSM_HEADS * SSM_HEAD_DIM
SSM_GROUPS = 2
SSM_STATE = 128
CONV_W = 4
CONV_DIM = SSM_INNER + 2 * SSM_GROUPS * SSM_STATE

AB_PROJ = 4 * HGRN_WIDTH + SSM_INNER + CONV_DIM + SSM_HEADS
AB_SPLITS = (HGRN_WIDTH, 2 * HGRN_WIDTH, 3 * HGRN_WIDTH, 4 * HGRN_WIDTH,
             4 * HGRN_WIDTH + SSM_INNER, 4 * HGRN_WIDTH + SSM_INNER + CONV_DIM)
AB_WIDTH = HGRN_WIDTH + SSM_INNER

GLA_HEADS = 4
GLA_HEAD_K = 128
GLA_HEAD_V = 256
GLA_KEY = GLA_HEADS * GLA_HEAD_K
GLA_VAL = GLA_HEADS * GLA_HEAD_V
GK_RANK = 16
GK_NORMALIZER = 16.0
GLA_PROJ = 2 * GLA_KEY + 2 * GLA_VAL + GK_RANK
GLA_SPLITS = (GLA_KEY, 2 * GLA_KEY, 2 * GLA_KEY + GLA_VAL, 2 * GLA_KEY + 2 * GLA_VAL)

kernel_name = "hgrn2_mamba2_gla_macaron_step"


def rmsnorm(x, w):
    xf = x.astype(jnp.float32)
    y = xf * lax.rsqrt(jnp.mean(xf * xf, axis=-1, keepdims=True) + EPS)
    return (y * w.astype(jnp.float32)).astype(x.dtype)


def swiglu(x, w_in, w_out):
    gate, up = jnp.split(x @ w_in, 2, axis=-1)
    return (jax.nn.silu(gate) * up) @ w_out


def chunk_size(length):
    return math.gcd(length, CHUNK)


def chunked_gated_linear_attention(q, k, v, log_f, s0):
    B, L, H, K = q.shape
    V = v.shape[-1]
    C = chunk_size(L)
    n = L // C

    def blocks(t):
        return t.astype(jnp.float32).reshape(B, n, C, H, t.shape[-1])

    qc, kc, vc, gc = blocks(q), blocks(k), blocks(v), blocks(log_f)
    g = jnp.cumsum(gc, axis=2)
    g_last = g[:, :, -1:]
    q_dec = qc * jnp.exp(g)
    k_inv = kc * jnp.exp(-g)
    k_end = kc * jnp.exp(g_last - g)
    causal = jnp.tril(jnp.ones((C, C), bool))
    scores = jnp.where(causal, jnp.einsum('bnihk,bnjhk->bnhij', q_dec, k_inv), 0.0)
    o_intra = jnp.einsum('bnhij,bnjhv->bnihv', scores, vc)

    def step(s, inp):
        q_d, k_e, v_c, decay = inp
        o_inter = jnp.einsum('bihk,bhkv->bihv', q_d, s)
        s = decay[..., None] * s + jnp.einsum('bjhk,bjhv->bhkv', k_e, v_c)
        return s, o_inter

    xs = (jnp.moveaxis(q_dec, 1, 0), jnp.moveaxis(k_end, 1, 0), jnp.moveaxis(vc, 1, 0),
          jnp.moveaxis(jnp.exp(g_last[:, :, 0]), 1, 0))
    s_final, o_inter = lax.scan(step, s0.astype(jnp.float32), xs)
    o = o_intra + jnp.moveaxis(o_inter, 0, 1)
    return o.reshape(B, L, H, V), s_final


def chunked_ssd(x, dt, a, b, c, s0):
    B, L, H, P = x.shape
    G, N = b.shape[-2], b.shape[-1]
    R = H // G
    C = chunk_size(L)
    n = L // C
    f32 = jnp.float32
    xc = x.astype(f32).reshape(B, n, C, G, R, P)
    dtc = dt.astype(f32).reshape(B, n, C, G, R)
    bc = b.astype(f32).reshape(B, n, C, G, N)
    cc = c.astype(f32).reshape(B, n, C, G, N)
    cum = jnp.cumsum(dtc * a.astype(f32).reshape(G, R), axis=2)
    causal = jnp.tril(jnp.ones((C, C), bool))[:, :, None, None]
    seg = cum[:, :, :, None] - cum[:, :, None, :]
    decay = jnp.exp(jnp.where(causal, seg, -jnp.inf))
    xdt = xc * dtc[..., None]
    cb = jnp.einsum('bnigs,bnjgs->bnijg', cc, bc)
    y_intra = jnp.einsum('bnijgr,bnjgrp->bnigrp', cb[..., None] * decay, xdt)
    x_end = xdt * jnp.exp(cum[:, :, -1:] - cum)[..., None]
    chunk_decay = jnp.exp(cum[:, :, -1])

    def step(s, inp):
        c_i, cdec_i, b_i, xe_i, dec_i = inp
        y_inter = jnp.einsum('bigs,bigr,bgrps->bigrp', c_i, cdec_i, s)
        s = dec_i[..., None, None] * s + jnp.einsum('bjgrp,bjgs->bgrps', xe_i, b_i)
        return s, y_inter

    xs = (jnp.moveaxis(cc, 1, 0), jnp.moveaxis(jnp.exp(cum), 1, 0), jnp.moveaxis(bc, 1, 0),
          jnp.moveaxis(x_end, 1, 0), jnp.moveaxis(chunk_decay, 1, 0))
    s_final, y_inter = lax.scan(step, s0.astype(f32).reshape(B, G, R, P, N), xs)
    y = y_intra + jnp.moveaxis(y_inter, 0, 1)
    return y.reshape(B, L, H, P), s_final.reshape(B, H, P, N)


def hgrn2_mixer(q_raw, f_raw, i_raw, g_raw, lb, state, norm_w):
    B, L, _ = q_raw.shape

    def heads(t):
        return t.reshape(B, L, HGRN_HEADS, HGRN_HEAD_DIM)

    f = lb + (1.0 - lb) * jax.nn.sigmoid(f_raw.astype(jnp.float32))
    q = jax.nn.silu(q_raw)
    o, s_new = chunked_gated_linear_attention(heads(q), heads(1.0 - f), heads(i_raw), heads(jnp.log(f)), state)
    o = rmsnorm(o, norm_w).reshape(B, L, HGRN_WIDTH) * jax.nn.silu(g_raw.astype(jnp.float32))
    return o.astype(q_raw.dtype), s_new


def mamba2_mixer(z, xbc, dt_raw, conv_buf, ssm_state, conv_w, conv_b, dt_bias, a_log, d_skip, norm_w):
    B, L, _ = xbc.shape
    padded = jnp.concatenate([conv_buf.astype(xbc.dtype), xbc], axis=1)
    conv = conv_b + sum(padded[:, k:k + L] * conv_w[k] for k in range(CONV_W))
    new_buf = padded[:, -(CONV_W - 1):]
    xbc_act = jax.nn.silu(conv)
    xs, bs, cs = jnp.split(xbc_act, [SSM_INNER, SSM_INNER + SSM_GROUPS * SSM_STATE], axis=-1)
    dt = jax.nn.softplus(dt_raw.astype(jnp.float32) + dt_bias.astype(jnp.float32))
    a = -jnp.exp(a_log.astype(jnp.float32))
    xh = xs.reshape(B, L, SSM_HEADS, SSM_HEAD_DIM)
    y, s_new = chunked_ssd(xh, dt, a,
                           bs.reshape(B, L, SSM_GROUPS, SSM_STATE),
                           cs.reshape(B, L, SSM_GROUPS, SSM_STATE), ssm_state)
    y = y + d_skip.astype(jnp.float32)[:, None] * xh.astype(jnp.float32)
    y = y.reshape(B, L, SSM_INNER) * jax.nn.silu(z.astype(jnp.float32))
    y = rmsnorm(y.reshape(B, L, SSM_GROUPS, SSM_INNER // SSM_GROUPS),
                norm_w.reshape(SSM_GROUPS, SSM_INNER // SSM_GROUPS)).reshape(B, L, SSM_INNER)
    return y.astype(xbc.dtype), new_buf, s_new


def gla_mixer(proj, state, w_gk, b_gk, norm_w):
    B, L, _ = proj.shape
    q, k, v, g, gk_low = jnp.split(proj, GLA_SPLITS, axis=-1)
    log_f = jax.nn.log_sigmoid((gk_low @ w_gk + b_gk).astype(jnp.float32)) / GK_NORMALIZER
    hk = lambda t: t.reshape(B, L, GLA_HEADS, GLA_HEAD_K)
    o, s_new = chunked_gated_linear_attention(hk(q * GLA_HEAD_K ** -0.5), hk(k),
                                              v.reshape(B, L, GLA_HEADS, GLA_HEAD_V), hk(log_f), state)
    o = rmsnorm(o, norm_w).reshape(B, L, GLA_VAL) * jax.nn.silu(g.astype(jnp.float32))
    return o.astype(proj.dtype), s_new


def trunk(x, s_hgrn, s_ssm, s_conv, s_gla, p):
    dtype = x.dtype
    new_hgrn, new_ssm, new_conv, new_gla = [], [], [], []
    lb_all = jnp.cumsum(jax.nn.softmax(p["hgrn_lb_logits"].astype(jnp.float32), axis=0), axis=0)
    for layer in range(DEPTH):
        j = layer // 2
        x = x + 0.5 * swiglu(rmsnorm(x, p["norm_ffn1"][layer]), p["ffn1_w_in"][layer], p["ffn1_w_out"][layer])
        h = rmsnorm(x, p["norm_mix"][layer])
        if layer % 2 == 0:
            proj = h @ p["ab_w_in"][j]
            q_raw, f_raw, i_raw, g_raw, z, xbc, dt_raw = jnp.split(proj, AB_SPLITS, axis=-1)
            o_a, s_a = hgrn2_mixer(q_raw, f_raw, i_raw, g_raw, lb_all[j], s_hgrn[j], p["hgrn_norm"][j])
            o_b, buf_b, s_b = mamba2_mixer(z, xbc, dt_raw, s_conv[j], s_ssm[j], p["ssm_conv_w"][j],
                                           p["ssm_conv_b"][j], p["ssm_dt_bias"][j], p["ssm_a_log"][j],
                                           p["ssm_d"][j], p["ssm_norm"][j])
            x = x + jnp.concatenate([o_a, o_b], axis=-1) @ p["ab_w_out"][j]
            new_hgrn.append(s_a.astype(dtype))
            new_ssm.append(s_b.astype(dtype))
            new_conv.append(buf_b.astype(dtype))
        else:
            proj = h @ p["gla_w_in"][j]
            o_c, s_c = gla_mixer(proj, s_gla[j], p["gla_w_gk"][j], p["gla_b_gk"][j], p["gla_norm"][j])
            x = x + o_c @ p["gla_w_out"][j]
            new_gla.append(s_c.astype(dtype))
        x = x + 0.5 * swiglu(rmsnorm(x, p["norm_ffn2"][layer]), p["ffn2_w_in"][layer], p["ffn2_w_out"][layer])
    y = rmsnorm(x, p["norm_final"])
    return y, jnp.stack(new_hgrn), jnp.stack(new_ssm), jnp.stack(new_conv), jnp.stack(new_gla)


def setup_inputs(seed: int = 0) -> dict:
    key = jax.random.key(seed)
    keys = list(jax.random.split(key, 29))

    def normal(i, shape, scale):
        return jax.random.normal(keys[i], shape, jnp.float32) * scale

    def gain(i, shape):
        return 1.0 + normal(i, shape, 0.02)

    dt0 = jnp.exp(jax.random.uniform(keys[20], (N_AB, SSM_HEADS), jnp.float32,
                                     minval=math.log(1e-3), maxval=math.log(1e-1)))
    return {
        "x_prompt": normal(0, (BATCH, SEQ, D_MODEL), 1.0),
        "x_sample": normal(1, (DEC_BATCH, DEC_SEQ, D_MODEL), 1.0),
        "state_hgrn": normal(2, (N_AB, DEC_BATCH, HGRN_HEADS, HGRN_HEAD_DIM, HGRN_HEAD_DIM), 0.5),
        "state_ssm": normal(3, (N_AB, DEC_BATCH, SSM_HEADS, SSM_HEAD_DIM, SSM_STATE), 0.5),
        "state_conv": normal(4, (N_AB, DEC_BATCH, CONV_W - 1, CONV_DIM), 1.0),
        "state_gla": normal(5, (N_C, DEC_BATCH, GLA_HEADS, GLA_HEAD_K, GLA_HEAD_V), 1.0),
        "norm_ffn1": gain(6, (DEPTH, D_MODEL)),
        "norm_mix": gain(7, (DEPTH, D_MODEL)),
        "norm_ffn2": gain(8, (DEPTH, D_MODEL)),
        "norm_final": gain(9, (D_MODEL,)),
        "ffn1_w_in": normal(10, (DEPTH, D_MODEL, 2 * D_FF), D_MODEL ** -0.5),
        "ffn1_w_out": normal(11, (DEPTH, D_FF, D_MODEL), D_FF ** -0.5),
        "ffn2_w_in": normal(12, (DEPTH, D_MODEL, 2 * D_FF), D_MODEL ** -0.5),
        "ffn2_w_out": normal(13, (DEPTH, D_FF, D_MODEL), D_FF ** -0.5),
        "ab_w_in": normal(14, (N_AB, D_MODEL, AB_PROJ), D_MODEL ** -0.5),
        "ab_w_out": normal(15, (N_AB, AB_WIDTH, D_MODEL), AB_WIDTH ** -0.5),
        "hgrn_lb_logits": normal(16, (N_AB + 1, HGRN_WIDTH), 0.1),
        "hgrn_norm": gain(17, (N_AB, HGRN_HEAD_DIM)),
        "ssm_conv_w": normal(18, (N_AB, CONV_W, CONV_DIM), CONV_W ** -0.5),
        "ssm_conv_b": normal(19, (N_AB, CONV_DIM), 0.02),
        "ssm_dt_bias": dt0 + jnp.log(-jnp.expm1(-dt0)),
        "ssm_a_log": jnp.log(jax.random.uniform(keys[21], (N_AB, SSM_HEADS), jnp.float32, minval=1.0, maxval=16.0)),
        "ssm_d": gain(22, (N_AB, SSM_HEADS)),
        "ssm_norm": gain(23, (N_AB, SSM_INNER)),
        "gla_w_in": normal(24, (N_C, D_MODEL, GLA_PROJ), D_MODEL ** -0.5),
        "gla_w_gk": normal(25, (N_C, GK_RANK, GLA_KEY), GK_RANK ** -0.5),
        "gla_b_gk": normal(26, (N_C, GLA_KEY), 0.1),
        "gla_norm": gain(27, (N_C, GLA_HEAD_V)),
        "gla_w_out": normal(28, (N_C, GLA_VAL, D_MODEL), GLA_VAL ** -0.5),
    }


def reference(x_prompt, x_sample, state_hgrn, state_ssm, state_conv, state_gla,
              norm_ffn1, norm_mix, norm_ffn2, norm_final,
              ffn1_w_in, ffn1_w_out, ffn2_w_in, ffn2_w_out,
              ab_w_in, ab_w_out, hgrn_lb_logits, hgrn_norm,
              ssm_conv_w, ssm_conv_b, ssm_dt_bias, ssm_a_log, ssm_d, ssm_norm,
              gla_w_in, gla_w_gk, gla_b_gk, gla_norm, gla_w_out):
    p = dict(norm_ffn1=norm_ffn1, norm_mix=norm_mix, norm_ffn2=norm_ffn2, norm_final=norm_final,
             ffn1_w_in=ffn1_w_in, ffn1_w_out=ffn1_w_out, ffn2_w_in=ffn2_w_in, ffn2_w_out=ffn2_w_out,
             ab_w_in=ab_w_in, ab_w_out=ab_w_out, hgrn_lb_logits=hgrn_lb_logits, hgrn_norm=hgrn_norm,
             ssm_conv_w=ssm_conv_w, ssm_conv_b=ssm_conv_b, ssm_dt_bias=ssm_dt_bias, ssm_a_log=ssm_a_log,
             ssm_d=ssm_d, ssm_norm=ssm_norm, gla_w_in=gla_w_in, gla_w_gk=gla_w_gk, gla_b_gk=gla_b_gk,
             gla_norm=gla_norm, gla_w_out=gla_w_out)
    b = x_prompt.shape[0]
    dt = x_prompt.dtype
    zh = jnp.zeros((N_AB, b, HGRN_HEADS, HGRN_HEAD_DIM, HGRN_HEAD_DIM), dt)
    zs = jnp.zeros((N_AB, b, SSM_HEADS, SSM_HEAD_DIM, SSM_STATE), dt)
    zc = jnp.zeros((N_AB, b, CONV_W - 1, CONV_DIM), dt)
    zg = jnp.zeros((N_C, b, GLA_HEADS, GLA_HEAD_K, GLA_HEAD_V), dt)
    y_prompt, hgrn_p, ssm_p, conv_p, gla_p = trunk(x_prompt, zh, zs, zc, zg, p)
    y_sample, hgrn_s, ssm_s, conv_s, gla_s = trunk(x_sample, state_hgrn, state_ssm, state_conv, state_gla, p)
    return (y_prompt, y_sample, hgrn_p, hgrn_s, ssm_p, ssm_s, conv_p, conv_s, gla_p, gla_s)
```

```python
import functools
import math

import jax
import jax.numpy as jnp
from jax import lax
from jax.experimental import pallas as pl
from jax.experimental.pallas import tpu as pltpu

F32 = jnp.float32
BF16 = jnp.bfloat16

D_MODEL = 1024
D_FF = 2816
EPS = 1e-6
CHUNK = 64

HGRN_HEADS = 4
HGRN_HEAD_DIM = 128
HGRN_WIDTH = HGRN_HEADS * HGRN_HEAD_DIM

SSM_HEADS = 8
SSM_HEAD_DIM = 64
SSM_INNER = SSM_HEADS * SSM_HEAD_DIM
SSM_GROUPS = 2
SSM_STATE = 128
SSM_GROUP_WIDTH = SSM_INNER // SSM_GROUPS
HEADS_PER_GROUP = SSM_HEADS // SSM_GROUPS
CONV_W = 4
CONV_DIM = SSM_INNER + 2 * SSM_GROUPS * SSM_STATE
AB_PROJ = 4 * HGRN_WIDTH + SSM_INNER + CONV_DIM + SSM_HEADS
AB_WIDTH = HGRN_WIDTH + SSM_INNER

GLA_HEADS = 4
GLA_HEAD_K = 128
GLA_HEAD_V = 256
GLA_KEY = GLA_HEADS * GLA_HEAD_K
GLA_VAL = GLA_HEADS * GLA_HEAD_V
GK_RANK = 16
GK_NORMALIZER = 16.0
GLA_PROJ = 2 * GLA_KEY + 2 * GLA_VAL + GK_RANK

LANES = 128
SUBLANES = 8
VMEM_LIMIT_BYTES = 56 * 1024 * 1024

AB_PROJ_PAD = -(-AB_PROJ // LANES) * LANES
GLA_PROJ_PAD = -(-GLA_PROJ // LANES) * LANES

_Q0, _F0, _I0, _G0 = 0, HGRN_WIDTH, 2 * HGRN_WIDTH, 3 * HGRN_WIDTH
_Z0 = 4 * HGRN_WIDTH
_XBC0 = _Z0 + SSM_INNER
_DT0 = _XBC0 + CONV_DIM


def _rms(x, w):
    return x * lax.rsqrt(jnp.mean(x * x, axis=-1, keepdims=True) + EPS) * w


def _silu(x):
    return x * jax.nn.sigmoid(x)


def _softplus(x):
    return jnp.maximum(x, 0.0) + jnp.log1p(jnp.exp(-jnp.abs(x)))


def _dot(a, b):
    return jnp.dot(a, b, preferred_element_type=F32)


def _bdot(a, b, ca, cb):
    return lax.dot_general(a, b, (((ca,), (cb,)), ((0,), (0,))), preferred_element_type=F32)


def _split3(x):
    hi = x.astype(BF16)
    r1 = x - hi.astype(F32)
    mid = r1.astype(BF16)
    lo = (r1 - mid.astype(F32)).astype(BF16)
    return hi, mid, lo


def _exact_bdot_lhs01(m01, x, ca, cb):
    return sum(_bdot(m01, p, ca, cb) for p in _split3(x))


def _exact_bdot_rhs01(x, m01, ca, cb):
    return sum(_bdot(p, m01, ca, cb) for p in _split3(x))


def _causal(nb, c):
    r = lax.broadcasted_iota(jnp.int32, (nb, c, c), 1)
    col = lax.broadcasted_iota(jnp.int32, (nb, c, c), 2)
    return r >= col


def _chunk_cumsum(x, causal):
    return _exact_bdot_lhs01(causal.astype(BF16), x, 2, 1)


def _colsum_bcast(x, lanes=LANES):
    nb, c, _ = x.shape
    return _exact_bdot_rhs01(x, jnp.ones((nb, c, lanes), BF16), 1, 1)


FF_TILE = D_FF // 2


def _ffn_half(x, norm_w, w_in_ref, w_out_ref):
    hb = _rms(x, norm_w).astype(BF16)
    acc = None
    for j in range(D_FF // FF_TILE):
        gate = _dot(hb, w_in_ref[:, j * FF_TILE:(j + 1) * FF_TILE])
        up = _dot(hb, w_in_ref[:, D_FF + j * FF_TILE:D_FF + (j + 1) * FF_TILE])
        act = (_silu(gate) * up).astype(BF16)
        part = _dot(act, w_out_ref[j * FF_TILE:(j + 1) * FF_TILE, :])
        acc = part if acc is None else acc + part
    return 0.5 * acc


def _pre_kernel(x_ref, nf_ref, w_in_ref, w_out_ref, nm_ref, wp_ref, x1_ref, proj_ref):
    x = x_ref[...]
    x1 = x + _ffn_half(x, nf_ref[...], w_in_ref, w_out_ref)
    x1_ref[...] = x1
    hb = _rms(x1, nm_ref[...]).astype(BF16)
    proj_ref[...] = _dot(hb, wp_ref[...])


def _post_kernel(x_ref, o_ref, wo_ref, nf_ref, w_in_ref, w_out_ref, nfin_ref, y_ref, *, final_norm):
    x2 = x_ref[...] + _dot(o_ref[...].astype(BF16), wo_ref[...])
    y = x2 + _ffn_half(x2, nf_ref[...], w_in_ref, w_out_ref)
    if final_norm:
        y = _rms(y, nfin_ref[...])
    y_ref[...] = y


def _resident(shape):
    return pl.BlockSpec(shape, lambda *_: (0,) * len(shape), pipeline_mode=pl.Buffered(1))


def _row_tile(rows, want):
    t = min(rows, want)
    assert rows % t == 0
    return t


def _pre_call(x, nf, w_in, w_out, nm, wp, *, name):
    rows = x.shape[0]
    npad = wp.shape[1]
    tm = _row_tile(rows, 256)
    tok = lambda n: pl.BlockSpec((tm, n), lambda i: (i, 0))
    return pl.pallas_call(
        _pre_kernel,
        grid=(rows // tm,),
        in_specs=[tok(D_MODEL), _resident((1, D_MODEL)), _resident(w_in.shape), _resident(w_out.shape),
                  _resident((1, D_MODEL)), _resident(wp.shape)],
        out_specs=[tok(D_MODEL), tok(npad)],
        out_shape=[jax.ShapeDtypeStruct((rows, D_MODEL), F32), jax.ShapeDtypeStruct((rows, npad), F32)],
        compiler_params=pltpu.CompilerParams(dimension_semantics=("arbitrary",), vmem_limit_bytes=VMEM_LIMIT_BYTES),
        name=name,
    )(x, nf, w_in, w_out, nm, wp)


def _post_call(x, o, wo, nf, w_in, w_out, nfin, *, final_norm, name):
    rows = x.shape[0]
    tm = _row_tile(rows, 512)
    tok = lambda n: pl.BlockSpec((tm, n), lambda i: (i, 0))
    return pl.pallas_call(
        functools.partial(_post_kernel, final_norm=final_norm),
        grid=(rows // tm,),
        in_specs=[tok(D_MODEL), tok(o.shape[1]), _resident(wo.shape), _resident((1, D_MODEL)),
                  _resident(w_in.shape), _resident(w_out.shape), _resident((1, D_MODEL))],
        out_specs=tok(D_MODEL),
        out_shape=jax.ShapeDtypeStruct((rows, D_MODEL), F32),
        compiler_params=pltpu.CompilerParams(dimension_semantics=("arbitrary",), vmem_limit_bytes=VMEM_LIMIT_BYTES),
        name=name,
    )(x, o, wo, nf, w_in, w_out, nfin)


def _gla_heads(q, k, v, log_f, gate, norm_w, n_heads, dk, dv, causal, sequential, read_state, write_state,
               o_ref, o_col0):
    nb, c, _ = q.shape
    g = _chunk_cumsum(log_f, causal)
    g_last = g[:, c - 1:c, :]
    q_dec = (q * jnp.exp(g)).astype(BF16)
    k_inv = (k * jnp.exp(-g)).astype(BF16)
    k_end = (k * jnp.exp(g_last - g)).astype(BF16)
    vb = v.astype(BF16)
    for h in range(n_heads):
        ks = slice(h * dk, (h + 1) * dk)
        vs = slice(h * dv, (h + 1) * dv)
        scores = jnp.where(causal, _bdot(q_dec[:, :, ks], k_inv[:, :, ks], 2, 2), 0.0).astype(BF16)
        o_intra = _bdot(scores, vb[:, :, vs], 2, 1)
        kv = _bdot(k_end[:, :, ks], vb[:, :, vs], 1, 1)
        decay = jnp.exp(_colsum_bcast(log_f[:, :, ks]))
        decay = jnp.concatenate([decay] * (dv // LANES), axis=-1)
        cols = slice(o_col0 + h * dv, o_col0 + (h + 1) * dv)
        if sequential:
            s = read_state(h)
            for b in range(nb):
                o = o_intra[b] + _dot(q_dec[b, :, ks], s.astype(BF16))
                s = decay[b] * s + kv[b]
                o_ref[b, :, cols] = (_rms(o, norm_w) * _silu(gate[b, :, vs])).astype(o_ref.dtype)
            write_state(h, s)
        else:
            s0 = read_state(h)
            o = o_intra + _bdot(q_dec[:, :, ks], s0.astype(BF16), 2, 1)
            write_state(h, decay * s0 + kv)
            o_ref[:, :, cols] = (_rms(o, norm_w) * _silu(gate[:, :, vs])).astype(o_ref.dtype)


def _expand_heads(x, expand):
    return _exact_bdot_rhs01(x, expand, 2, 1)


def _ssd_heads(xs, bs, cs, z, dt, a_row, d_row, norm_w, causal, sequential, read_state, write_state,
               o_ref, o_col0):
    nb, c, _ = xs.shape
    hrow = lax.broadcasted_iota(jnp.int32, (SSM_HEADS, SSM_INNER), 0)
    hcol = lax.broadcasted_iota(jnp.int32, (SSM_HEADS, SSM_INNER), 1) // SSM_HEAD_DIM
    expand2d = (hrow == hcol).astype(BF16)
    expand = jnp.broadcast_to(expand2d[None], (nb, SSM_HEADS, SSM_INNER))
    a_x = sum(_dot(p, expand2d) for p in _split3(a_row))
    d_x = sum(_dot(p, expand2d) for p in _split3(d_row))
    dta = dt * a_row
    cum = _chunk_cumsum(dta, causal)
    cum_t = _exact_bdot_rhs01(dta, _upper(nb, c), 1, 1)
    dt_x = _expand_heads(dt, expand)
    cum_x = _expand_heads(cum, expand)
    dta_x = dt_x * a_x
    xdt = xs * dt_x
    cum_last = cum_x[:, c - 1:c, :]
    x_end = (xdt * jnp.exp(cum_last - cum_x)).astype(BF16)
    chunk_dec = jnp.exp(cum_x)
    bsb = bs.astype(BF16)
    csb = cs.astype(BF16)
    lane_head = lax.broadcasted_iota(jnp.int32, (nb, c, SSM_GROUP_WIDTH), 2) // SSM_HEAD_DIM
    for g in range(SSM_GROUPS):
        gl = slice(g * SSM_STATE, (g + 1) * SSM_STATE)
        hl = slice(g * SSM_GROUP_WIDTH, (g + 1) * SSM_GROUP_WIDTH)
        cb = _bdot(csb[:, :, gl], bsb[:, :, gl], 2, 2)
        xdt_g = xdt[:, :, hl]
        y = None
        for r in range(HEADS_PER_GROUP):
            h = g * HEADS_PER_GROUP + r
            col = jnp.broadcast_to(cum[:, :, h:h + 1], (nb, c, c))
            row = cum_t[:, h:h + 1, :]
            dec = jnp.where(causal, jnp.exp(col - row), 0.0)
            lmat = (cb * dec).astype(BF16)
            xm = jnp.where(lane_head == r, xdt_g, 0.0).astype(BF16)
            part = _bdot(lmat, xm, 2, 1)
            y = part if y is None else y + part
        kv = _bdot(x_end[:, :, hl], bsb[:, :, gl], 1, 1)
        decay = jnp.exp(_colsum_bcast(dta_x[:, :, hl], SSM_STATE))
        cols = slice(o_col0 + g * SSM_GROUP_WIDTH, o_col0 + (g + 1) * SSM_GROUP_WIDTH)

        def finish(y_intra, y_inter, b):
            y_all = y_intra + y_inter * chunk_dec[b, :, hl] + d_x[:, hl] * xs[b, :, hl]
            y_all = _rms(y_all * _silu(z[b, :, hl]), norm_w[:, hl])
            o_ref[b, :, cols] = y_all.astype(o_ref.dtype)

        if sequential:
            s = read_state(g)
            for b in range(nb):
                y_inter = lax.dot_general(csb[b, :, gl], s.astype(BF16), (((1,), (1,)), ((), ())),
                                          preferred_element_type=F32)
                s = decay[b] * s + kv[b]
                finish(y[b], y_inter, b)
            write_state(g, s)
        else:
            s0 = read_state(g)
            y_inter = _bdot(csb[:, :, gl], s0.astype(BF16), 2, 2)
            write_state(g, decay * s0 + kv)
            finish(y, y_inter, slice(None))


def _upper(nb, c):
    r = lax.broadcasted_iota(jnp.int32, (nb, c, c), 1)
    col = lax.broadcasted_iota(jnp.int32, (nb, c, c), 2)
    return (r <= col).astype(BF16)


def _lower_bound(lb_logits):
    m = jnp.max(lb_logits, axis=0, keepdims=True)
    e = jnp.exp(lb_logits - m)
    return e[0:1, :] / jnp.sum(e, axis=0, keepdims=True)


def _ab_math(proj_ref, conv, lb_ref, hn_ref, dtb_ref, alog_ref, d_ref, sn_ref, sequential,
             read_h, write_h, read_s, write_s, o_ref):
    nb, c, _ = o_ref.shape
    causal = _causal(nb, c)
    lb = _lower_bound(lb_ref[...])
    f = lb + (1.0 - lb) * jax.nn.sigmoid(proj_ref[:, :, _F0:_F0 + HGRN_WIDTH])
    _gla_heads(_silu(proj_ref[:, :, _Q0:_Q0 + HGRN_WIDTH]), 1.0 - f, proj_ref[:, :, _I0:_I0 + HGRN_WIDTH],
               jnp.log(f), proj_ref[:, :, _G0:_G0 + HGRN_WIDTH], hn_ref[...], HGRN_HEADS, HGRN_HEAD_DIM,
               HGRN_HEAD_DIM, causal, sequential, read_h, write_h, o_ref, 0)
    act = _silu(conv)
    dt = _softplus(proj_ref[:, :, _DT0:_DT0 + SSM_HEADS] + dtb_ref[...])
    _ssd_heads(act[:, :, :SSM_INNER], act[:, :, SSM_INNER:SSM_INNER + SSM_GROUPS * SSM_STATE],
               act[:, :, SSM_INNER + SSM_GROUPS * SSM_STATE:], proj_ref[:, :, _Z0:_Z0 + SSM_INNER], dt,
               -jnp.exp(alog_ref[...]), d_ref[...], sn_ref[...], causal, sequential, read_s, write_s,
               o_ref, HGRN_WIDTH)


def _ab_seq_kernel(proj_ref, lb_ref, hn_ref, cw_ref, cbias_ref, dtb_ref, alog_ref, d_ref, sn_ref,
                   o_ref, sh_out, ss_out, sc_out, sh, ss, xpad):
    t = pl.program_id(1)
    nb, c, _ = o_ref.shape
    rows = nb * c

    @pl.when(t == 0)
    def _():
        sh[...] = jnp.zeros_like(sh)
        ss[...] = jnp.zeros_like(ss)
        xpad[0:SUBLANES, :] = jnp.zeros((SUBLANES, CONV_DIM), F32)

    xpad[SUBLANES:SUBLANES + rows, :] = proj_ref[:, :, _XBC0:_XBC0 + CONV_DIM].reshape(rows, CONV_DIM)
    conv = cbias_ref[...]
    for k in range(CONV_W):
        off = SUBLANES - (CONV_W - 1) + k
        conv = conv + xpad[off:off + rows, :] * cw_ref[k:k + 1, :]
    xpad[0:SUBLANES, :] = xpad[rows:rows + SUBLANES, :]
    conv = conv.reshape(nb, c, CONV_DIM)

    def read_h(h):
        return sh[h]

    def write_h(h, s):
        sh[h] = s

    def read_s(g):
        return ss[g]

    def write_s(g, s):
        ss[g] = s

    _ab_math(proj_ref, conv, lb_ref, hn_ref, dtb_ref, alog_ref, d_ref, sn_ref, True,
             read_h, write_h, read_s, write_s, o_ref)

    @pl.when(t == pl.num_programs(1) - 1)
    def _():
        sh_out[0] = sh[...]
        ss_out[0] = ss[...]
        sc_out[0] = xpad[SUBLANES - (CONV_W - 1):SUBLANES, :]


def _ab_par_kernel(proj_ref, sh_in, ss_in, sc_in, lb_ref, hn_ref, cw_ref, cbias_ref, dtb_ref, alog_ref,
                   d_ref, sn_ref, o_ref, sh_out, ss_out, sc_out, xpad):
    nb, c, _ = o_ref.shape
    nbuf = CONV_W - 1
    xpad[:, 0:nbuf, :] = sc_in[...]
    xpad[:, nbuf:nbuf + c, :] = proj_ref[:, :, _XBC0:_XBC0 + CONV_DIM]
    conv = cbias_ref[...]
    for k in range(CONV_W):
        conv = conv + xpad[:, k:k + c, :] * cw_ref[k:k + 1, :]
    sc_out[...] = xpad[:, c:c + nbuf, :]

    def read_h(h):
        return sh_in[:, h]

    def write_h(h, s):
        sh_out[:, h] = s

    def read_s(g):
        return ss_in[:, g]

    def write_s(g, s):
        ss_out[:, g] = s

    _ab_math(proj_ref, conv, lb_ref, hn_ref, dtb_ref, alog_ref, d_ref, sn_ref, False,
             read_h, write_h, read_s, write_s, o_ref)


def _gla_math(proj_ref, wgk_ref, bgk_ref, gn_ref, sequential, read_g, write_g, o_ref):
    nb, c, _ = o_ref.shape
    causal = _causal(nb, c)
    q = proj_ref[:, :, 0:GLA_KEY] * (GLA_HEAD_K ** -0.5)
    k = proj_ref[:, :, GLA_KEY:2 * GLA_KEY]
    v = proj_ref[:, :, 2 * GLA_KEY:2 * GLA_KEY + GLA_VAL]
    gate = proj_ref[:, :, 2 * GLA_KEY + GLA_VAL:2 * GLA_KEY + 2 * GLA_VAL]
    gk_low = proj_ref[:, :, 2 * GLA_KEY + 2 * GLA_VAL:2 * GLA_KEY + 2 * GLA_VAL + GK_RANK].astype(BF16)
    if c % SUBLANES == 0:
        gk = _dot(gk_low.reshape(nb * c, GK_RANK), wgk_ref[...]).reshape(nb, c, GLA_KEY)
    else:
        gk = _bdot(gk_low, jnp.broadcast_to(wgk_ref[...][None], (nb, GK_RANK, GLA_KEY)), 2, 1)
    gk = gk + bgk_ref[...]
    log_f = -_softplus(-gk) / GK_NORMALIZER
    _gla_heads(q, k, v, log_f, gate, gn_ref[...], GLA_HEADS, GLA_HEAD_K, GLA_HEAD_V, causal, sequential,
               read_g, write_g, o_ref, 0)


def _gla_seq_kernel(proj_ref, wgk_ref, bgk_ref, gn_ref, o_ref, sg_out, sg):
    t = pl.program_id(1)

    @pl.when(t == 0)
    def _():
        sg[...] = jnp.zeros_like(sg)

    def read_g(h):
        return sg[h]

    def write_g(h, s):
        sg[h] = s

    _gla_math(proj_ref, wgk_ref, bgk_ref, gn_ref, True, read_g, write_g, o_ref)

    @pl.when(t == pl.num_programs(1) - 1)
    def _():
        sg_out[0] = sg[...]


def _gla_par_kernel(proj_ref, sg_in, wgk_ref, bgk_ref, gn_ref, o_ref, sg_out):
    def read_g(h):
        return sg_in[:, h]

    def write_g(h, s):
        sg_out[:, h] = s

    _gla_math(proj_ref, wgk_ref, bgk_ref, gn_ref, False, read_g, write_g, o_ref)


SEQ_TILE_CHUNKS = 4
PAR_TILE_SEQS = 8


def _full(shape):
    return pl.BlockSpec(shape, lambda *_: (0,) * len(shape))


def _mixer_o_dtype(c):
    return BF16 if c % (2 * SUBLANES) == 0 else F32


def _ab_seq_call(proj, batch, length, params, *, name):
    c = math.gcd(length, CHUNK)
    nb = SEQ_TILE_CHUNKS
    tiles = length // (c * nb)
    assert tiles * c * nb == length
    width = HEADS_PER_GROUP * SSM_HEAD_DIM
    blk = lambda n: pl.BlockSpec((nb, c, n), lambda b, t: (b * tiles + t, 0, 0))
    out_shapes = [
        jax.ShapeDtypeStruct((batch * length // c, c, AB_WIDTH), _mixer_o_dtype(c)),
        jax.ShapeDtypeStruct((batch, HGRN_HEADS, HGRN_HEAD_DIM, HGRN_HEAD_DIM), F32),
        jax.ShapeDtypeStruct((batch, SSM_GROUPS, width, SSM_STATE), F32),
        jax.ShapeDtypeStruct((batch, CONV_W - 1, CONV_DIM), F32),
    ]
    out_specs = [
        blk(AB_WIDTH),
        pl.BlockSpec((1, HGRN_HEADS, HGRN_HEAD_DIM, HGRN_HEAD_DIM), lambda b, t: (b, 0, 0, 0)),
        pl.BlockSpec((1, SSM_GROUPS, width, SSM_STATE), lambda b, t: (b, 0, 0, 0)),
        pl.BlockSpec((1, CONV_W - 1, CONV_DIM), lambda b, t: (b, 0, 0)),
    ]
    return pl.pallas_call(
        _ab_seq_kernel,
        grid=(batch, tiles),
        in_specs=[blk(AB_PROJ_PAD)] + [_full(p.shape) for p in params],
        out_specs=out_specs,
        out_shape=out_shapes,
        scratch_shapes=[
            pltpu.VMEM((HGRN_HEADS, HGRN_HEAD_DIM, HGRN_HEAD_DIM), F32),
            pltpu.VMEM((SSM_GROUPS, width, SSM_STATE), F32),
            pltpu.VMEM((nb * c + SUBLANES, CONV_DIM), F32),
        ],
        compiler_params=pltpu.CompilerParams(dimension_semantics=("arbitrary", "arbitrary"),
                                             vmem_limit_bytes=VMEM_LIMIT_BYTES),
        name=name,
    )(proj, *params)


def _ab_par_call(proj, s_hgrn, s_ssm, s_conv, params, *, name):
    batch, c, _ = proj.shape
    nb = _row_tile(batch, PAR_TILE_SEQS)
    width = HEADS_PER_GROUP * SSM_HEAD_DIM
    s_ssm = s_ssm.reshape(batch, SSM_GROUPS, width, SSM_STATE)
    blk3 = lambda a, n: pl.BlockSpec((nb, a, n), lambda b: (b, 0, 0))
    blk4 = lambda a, r, n: pl.BlockSpec((nb, a, r, n), lambda b: (b, 0, 0, 0))
    state_specs = [blk4(HGRN_HEADS, HGRN_HEAD_DIM, HGRN_HEAD_DIM), blk4(SSM_GROUPS, width, SSM_STATE),
                   blk3(CONV_W - 1, CONV_DIM)]
    out_shapes = [
        jax.ShapeDtypeStruct((batch, c, AB_WIDTH), _mixer_o_dtype(c)),
        jax.ShapeDtypeStruct(s_hgrn.shape, F32),
        jax.ShapeDtypeStruct(s_ssm.shape, F32),
        jax.ShapeDtypeStruct(s_conv.shape, F32),
    ]
    return pl.pallas_call(
        _ab_par_kernel,
        grid=(batch // nb,),
        in_specs=[blk3(c, AB_PROJ_PAD)] + state_specs + [_full(p.shape) for p in params],
        out_specs=[blk3(c, AB_WIDTH)] + state_specs,
        out_shape=out_shapes,
        scratch_shapes=[pltpu.VMEM((nb, c + CONV_W - 1, CONV_DIM), F32)],
        compiler_params=pltpu.CompilerParams(dimension_semantics=("arbitrary",),
                                             vmem_limit_bytes=VMEM_LIMIT_BYTES),
        name=name,
    )(proj, s_hgrn, s_ssm, s_conv, *params)


def _gla_seq_call(proj, batch, length, params, *, name):
    c = math.gcd(length, CHUNK)
    nb = SEQ_TILE_CHUNKS
    tiles = length // (c * nb)
    assert tiles * c * nb == length
    blk = lambda n: pl.BlockSpec((nb, c, n), lambda b, t: (b * tiles + t, 0, 0))
    return pl.pallas_call(
        _gla_seq_kernel,
        grid=(batch, tiles),
        in_specs=[blk(GLA_PROJ_PAD)] + [_full(p.shape) for p in params],
        out_specs=[blk(GLA_VAL),
                   pl.BlockSpec((1, GLA_HEADS, GLA_HEAD_K, GLA_HEAD_V), lambda b, t: (b, 0, 0, 0))],
        out_shape=[jax.ShapeDtypeStruct((batch * length // c, c, GLA_VAL), _mixer_o_dtype(c)),
                   jax.ShapeDtypeStruct((batch, GLA_HEADS, GLA_HEAD_K, GLA_HEAD_V), F32)],
        scratch_shapes=[pltpu.VMEM((GLA_HEADS, GLA_HEAD_K, GLA_HEAD_V), F32)],
        compiler_params=pltpu.CompilerParams(dimension_semantics=("arbitrary", "arbitrary"),
                                             vmem_limit_bytes=VMEM_LIMIT_BYTES),
        name=name,
    )(proj, *params)


def _gla_par_call(proj, s_gla, params, *, name):
    batch, c, _ = proj.shape
    nb = _row_tile(batch, PAR_TILE_SEQS)
    blk3 = lambda a, n: pl.BlockSpec((nb, a, n), lambda b: (b, 0, 0))
    sspec = pl.BlockSpec((nb, GLA_HEADS, GLA_HEAD_K, GLA_HEAD_V), lambda b: (b, 0, 0, 0))
    return pl.pallas_call(
        _gla_par_kernel,
        grid=(batch // nb,),
        in_specs=[blk3(c, GLA_PROJ_PAD), sspec] + [_full(p.shape) for p in params],
        out_specs=[blk3(c, GLA_VAL), sspec],
        out_shape=[jax.ShapeDtypeStruct((batch, c, GLA_VAL), _mixer_o_dtype(c)),
                   jax.ShapeDtypeStruct(s_gla.shape, F32)],
        compiler_params=pltpu.CompilerParams(dimension_semantics=("arbitrary",),
                                             vmem_limit_bytes=VMEM_LIMIT_BYTES),
        name=name,
    )(proj, s_gla, *params)


def _pad_cols(w, n):
    return jnp.pad(w, ((0, 0), (0, n - w.shape[1])))


def _trunk(x, states, p, tag):
    batch, length, _ = x.shape
    rows = batch * length
    c = math.gcd(length, CHUNK)
    row = lambda v: v.reshape(1, -1)
    xt = x.reshape(rows, D_MODEL)

    x1, proj = _pre_call(xt, row(p["norm_ffn1"][0]), p["ffn1_w_in"][0], p["ffn1_w_out"][0],
                         row(p["norm_mix"][0]), p["ab_w_in"], name=f"pre0_{tag}")
    ab_params = (p["hgrn_lb_logits"], row(p["hgrn_norm"][0]), p["ssm_conv_w"][0], row(p["ssm_conv_b"][0]),
                 row(p["ssm_dt_bias"][0]), row(p["ssm_a_log"][0]), row(p["ssm_d"][0]), row(p["ssm_norm"][0]))
    if states is None:
        o, hgrn, ssm, conv = _ab_seq_call(proj.reshape(rows // c, c, AB_PROJ_PAD), batch, length, ab_params,
                                          name=f"mix0_{tag}")
    else:
        o, hgrn, ssm, conv = _ab_par_call(proj.reshape(batch, c, AB_PROJ_PAD), states[0][0], states[1][0],
                                          states[2][0], ab_params, name=f"mix0_{tag}")
    ssm = ssm.reshape(batch, SSM_HEADS, SSM_HEAD_DIM, SSM_STATE)
    x3 = _post_call(x1, o.reshape(rows, AB_WIDTH), p["ab_w_out"], row(p["norm_ffn2"][0]), p["ffn2_w_in"][0],
                    p["ffn2_w_out"][0], row(p["norm_final"]), final_norm=False, name=f"post0_{tag}")

    x4, proj = _pre_call(x3, row(p["norm_ffn1"][1]), p["ffn1_w_in"][1], p["ffn1_w_out"][1],
                         row(p["norm_mix"][1]), p["gla_w_in"], name=f"pre1_{tag}")
    gla_params = (p["gla_w_gk"], row(p["gla_b_gk"][0]), row(p["gla_norm"][0]))
    if states is None:
        o, gla = _gla_seq_call(proj.reshape(rows // c, c, GLA_PROJ_PAD), batch, length, gla_params,
                               name=f"mix1_{tag}")
    else:
        o, gla = _gla_par_call(proj.reshape(batch, c, GLA_PROJ_PAD), states[3][0], gla_params,
                               name=f"mix1_{tag}")
    y = _post_call(x4, o.reshape(rows, GLA_VAL), p["gla_w_out"], row(p["norm_ffn2"][1]), p["ffn2_w_in"][1],
                   p["ffn2_w_out"][1], row(p["norm_final"]), final_norm=True, name=f"post1_{tag}")
    return (y.reshape(batch, length, D_MODEL), hgrn[None], ssm[None], conv[None], gla[None])


def kernel(x_prompt, x_sample, state_hgrn, state_ssm, state_conv, state_gla, norm_ffn1, norm_mix, norm_ffn2, norm_final, ffn1_w_in, ffn1_w_out, ffn2_w_in, ffn2_w_out, ab_w_in, ab_w_out, hgrn_lb_logits, hgrn_norm, ssm_conv_w, ssm_conv_b, ssm_dt_bias, ssm_a_log, ssm_d, ssm_norm, gla_w_in, gla_w_gk, gla_b_gk, gla_norm, gla_w_out):
    assert ab_w_in.shape[0] == 1 and gla_w_in.shape[0] == 1, "one HGRN2/SSD layer and one GLA layer"
    p = dict(
        norm_ffn1=norm_ffn1, norm_mix=norm_mix, norm_ffn2=norm_ffn2, norm_final=norm_final,
        ffn1_w_in=ffn1_w_in.astype(BF16), ffn1_w_out=ffn1_w_out.astype(BF16),
        ffn2_w_in=ffn2_w_in.astype(BF16), ffn2_w_out=ffn2_w_out.astype(BF16),
        ab_w_in=_pad_cols(ab_w_in[0], AB_PROJ_PAD).astype(BF16), ab_w_out=ab_w_out[0].astype(BF16),
        hgrn_lb_logits=hgrn_lb_logits, hgrn_norm=hgrn_norm,
        ssm_conv_w=ssm_conv_w, ssm_conv_b=ssm_conv_b, ssm_dt_bias=ssm_dt_bias, ssm_a_log=ssm_a_log,
        ssm_d=ssm_d, ssm_norm=ssm_norm,
        gla_w_in=_pad_cols(gla_w_in[0], GLA_PROJ_PAD).astype(BF16), gla_w_gk=gla_w_gk[0].astype(BF16),
        gla_b_gk=gla_b_gk, gla_norm=gla_norm, gla_w_out=gla_w_out[0].astype(BF16),
    )
    y_p, hgrn_p, ssm_p, conv_p, gla_p = _trunk(x_prompt, None, p, "prompt")
    y_s, hgrn_s, ssm_s, conv_s, gla_s = _trunk(x_sample, (state_hgrn, state_ssm, state_conv, state_gla), p,
                                               "sample")
    return (y_p, y_s, hgrn_p, hgrn_s, ssm_p, ssm_s, conv_p, conv_s, gla_p, gla_s)
```

```python
import functools
import math

import jax
import jax.numpy as jnp
from jax import lax
from jax.experimental import pallas as pl
from jax.experimental.pallas import tpu as pltpu

F32 = jnp.float32
BF16 = jnp.bfloat16

D_MODEL = 1024
D_FF = 2816
EPS = 1e-6
CHUNK = 64

HGRN_HEADS = 4
HGRN_HEAD_DIM = 128
HGRN_WIDTH = HGRN_HEADS * HGRN_HEAD_DIM

SSM_HEADS = 8
SSM_HEAD_DIM = 64
SSM_INNER = SSM_HEADS * SSM_HEAD_DIM
SSM_GROUPS = 2
SSM_STATE = 128
SSM_GROUP_WIDTH = SSM_INNER // SSM_GROUPS
HEADS_PER_GROUP = SSM_HEADS // SSM_GROUPS
CONV_W = 4
CONV_DIM = SSM_INNER + 2 * SSM_GROUPS * SSM_STATE
AB_PROJ = 4 * HGRN_WIDTH + SSM_INNER + CONV_DIM + SSM_HEADS
AB_WIDTH = HGRN_WIDTH + SSM_INNER

GLA_HEADS = 4
GLA_HEAD_K = 128
GLA_HEAD_V = 256
GLA_KEY = GLA_HEADS * GLA_HEAD_K
GLA_VAL = GLA_HEADS * GLA_HEAD_V
GK_RANK = 16
GK_NORMALIZER = 16.0
GLA_PROJ = 2 * GLA_KEY + 2 * GLA_VAL + GK_RANK

LANES = 128
SUBLANES = 8
VMEM_LIMIT_BYTES = 56 * 1024 * 1024

AB_PROJ_PAD = -(-AB_PROJ // LANES) * LANES
GLA_PROJ_PAD = -(-GLA_PROJ // LANES) * LANES

_Q0, _F0, _I0, _G0 = 0, HGRN_WIDTH, 2 * HGRN_WIDTH, 3 * HGRN_WIDTH
_Z0 = 4 * HGRN_WIDTH
_XBC0 = _Z0 + SSM_INNER
_DT0 = _XBC0 + CONV_DIM


def _rms(x, w):
    return x * lax.rsqrt(jnp.mean(x * x, axis=-1, keepdims=True) + EPS) * w


def _silu(x):
    return x * jax.nn.sigmoid(x)


def _softplus(x):
    return jnp.maximum(x, 0.0) + jnp.log1p(jnp.exp(-jnp.abs(x)))


def _dot(a, b):
    return jnp.dot(a, b, preferred_element_type=F32)


def _bdot(a, b, ca, cb):
    return lax.dot_general(a, b, (((ca,), (cb,)), ((0,), (0,))), preferred_element_type=F32)


def _split3(x):
    hi = x.astype(BF16)
    r1 = x - hi.astype(F32)
    mid = r1.astype(BF16)
    lo = (r1 - mid.astype(F32)).astype(BF16)
    return hi, mid, lo


def _exact_bdot_lhs01(m01, x, ca, cb):
    return sum(_bdot(m01, p, ca, cb) for p in _split3(x))


def _exact_bdot_rhs01(x, m01, ca, cb):
    return sum(_bdot(p, m01, ca, cb) for p in _split3(x))


def _causal(nb, c):
    r = lax.broadcasted_iota(jnp.int32, (nb, c, c), 1)
    col = lax.broadcasted_iota(jnp.int32, (nb, c, c), 2)
    return r >= col


def _chunk_cumsum(x, causal):
    return _exact_bdot_lhs01(causal.astype(BF16), x, 2, 1)


def _colsum_bcast(x, lanes=LANES):
    nb, c, _ = x.shape
    return _exact_bdot_rhs01(x, jnp.ones((nb, c, lanes), BF16), 1, 1)


FF_TILE = D_FF // 2


def _ffn_half(x, norm_w, w_in_ref, w_out_ref):
    hb = _rms(x, norm_w).astype(BF16)
    acc = None
    for j in range(D_FF // FF_TILE):
        gate = _dot(hb, w_in_ref[:, j * FF_TILE:(j + 1) * FF_TILE])
        up = _dot(hb, w_in_ref[:, D_FF + j * FF_TILE:D_FF + (j + 1) * FF_TILE])
        act = (_silu(gate) * up).astype(BF16)
        part = _dot(act, w_out_ref[j * FF_TILE:(j + 1) * FF_TILE, :])
        acc = part if acc is None else acc + part
    return 0.5 * acc


def _pick_group(first_steps, a, b):
    return jnp.where(pl.program_id(0) < first_steps, a, b)


def _pre_kernel(*refs, first_steps):
    if first_steps is None:
        x_ref, nf_ref, w_in_ref, w_out_ref, nm_ref, wp_ref, wt_ref, x1_ref, proj_ref = refs
        x = x_ref[...]
    else:
        xa_ref, xb_ref, nf_ref, w_in_ref, w_out_ref, nm_ref, wp_ref, wt_ref, x1_ref, proj_ref = refs
        x = _pick_group(first_steps, xa_ref[...], xb_ref[...])
    x1 = x + _ffn_half(x, nf_ref[...], w_in_ref, w_out_ref)
    x1_ref[...] = x1
    hb = _rms(x1, nm_ref[...]).astype(BF16)
    main = wp_ref.shape[1]
    proj_ref[:, :main] = _dot(hb, wp_ref[...])
    proj_ref[:, main:] = _dot(hb, wt_ref[...])


def _post_kernel(x_ref, oa_ref, ob_ref, wo_ref, nf_ref, w_in_ref, w_out_ref, nfin_ref, *y_refs, first_steps):
    o = _pick_group(first_steps, oa_ref[...].astype(BF16), ob_ref[...].astype(BF16))
    x2 = x_ref[...] + _dot(o, wo_ref[...])
    y = x2 + _ffn_half(x2, nf_ref[...], w_in_ref, w_out_ref)
    if len(y_refs) == 1:
        y_refs[0][...] = y
    else:
        y = _rms(y, nfin_ref[...])
        ya_ref, yb_ref = y_refs

        @pl.when(pl.program_id(0) < first_steps)
        def _():
            ya_ref[...] = y

        @pl.when(pl.program_id(0) >= first_steps)
        def _():
            yb_ref[...] = y


def _resident(shape, layer=None):
    if layer is None:
        return pl.BlockSpec(shape, lambda *_: (0,) * len(shape), pipeline_mode=pl.Buffered(1))
    return pl.BlockSpec((None,) + tuple(shape[1:]), lambda *_: (layer,) + (0,) * (len(shape) - 1),
                        pipeline_mode=pl.Buffered(1))


def _row_tile(rows, want):
    t = min(rows, want)
    assert rows % t == 0
    return t


PRE_ROWS = 256
POST_ROWS = 512


def _group_specs(tm, first_steps, width):
    first = pl.BlockSpec((tm, width), lambda i: (jnp.minimum(i, first_steps - 1), 0))
    second = pl.BlockSpec((tm, width), lambda i: (jnp.maximum(i - first_steps, 0), 0))
    return first, second


def _pre_call(xa, xb, layer, nf, w_in, w_out, nm, wp, wt, *, name):
    ra = xa.shape[0]
    rb = 0 if xb is None else xb.shape[0]
    rows = ra + rb
    tm = _row_tile(rb if rb else ra, PRE_ROWS)
    assert ra % tm == 0
    npad = wp.shape[1] + wt.shape[1]
    tok = lambda n: pl.BlockSpec((tm, n), lambda i: (i, 0))
    if xb is None:
        first_steps, x_specs, xs = None, [tok(D_MODEL)], (xa,)
    else:
        first_steps = ra // tm
        x_specs, xs = list(_group_specs(tm, first_steps, D_MODEL)), (xa, xb)
    return pl.pallas_call(
        functools.partial(_pre_kernel, first_steps=first_steps),
        grid=(rows // tm,),
        in_specs=x_specs + [_resident(nf.shape, layer), _resident(w_in.shape, layer),
                            _resident(w_out.shape, layer), _resident(nm.shape, layer), _resident(wp.shape),
                            _resident(wt.shape)],
        out_specs=[tok(D_MODEL), tok(npad)],
        out_shape=[jax.ShapeDtypeStruct((rows, D_MODEL), F32), jax.ShapeDtypeStruct((rows, npad), F32)],
        compiler_params=pltpu.CompilerParams(dimension_semantics=("arbitrary",), vmem_limit_bytes=VMEM_LIMIT_BYTES),
        name=name,
    )(*xs, nf, w_in, w_out, nm, wp, wt)


def _post_call(x, oa, ob, layer, wo, nf, w_in, w_out, nfin, *, split_output, name):
    rows = x.shape[0]
    ra, rb = oa.shape[0], ob.shape[0]
    assert ra + rb == rows
    tm = _row_tile(rb, POST_ROWS)
    assert ra % tm == 0
    first_steps = ra // tm
    tok = lambda n: pl.BlockSpec((tm, n), lambda i: (i, 0))
    spec_a, spec_b = _group_specs(tm, first_steps, oa.shape[1])
    if split_output:
        out_specs = list(_group_specs(tm, first_steps, D_MODEL))
        out_shape = [jax.ShapeDtypeStruct((ra, D_MODEL), F32), jax.ShapeDtypeStruct((rb, D_MODEL), F32)]
    else:
        out_specs = [tok(D_MODEL)]
        out_shape = [jax.ShapeDtypeStruct((rows, D_MODEL), F32)]
    return pl.pallas_call(
        functools.partial(_post_kernel, first_steps=first_steps),
        grid=(rows // tm,),
        in_specs=[tok(D_MODEL), spec_a, spec_b, _resident(wo.shape), _resident(nf.shape, layer),
                  _resident(w_in.shape, layer), _resident(w_out.shape, layer), _resident(nfin.shape)],
        out_specs=out_specs,
        out_shape=out_shape,
        compiler_params=pltpu.CompilerParams(dimension_semantics=("arbitrary",), vmem_limit_bytes=VMEM_LIMIT_BYTES),
        name=name,
    )(x, oa, ob, wo, nf, w_in, w_out, nfin)


def _gla_heads(q, k, v, log_f, gate, norm_w, n_heads, dk, dv, causal, sequential, read_state, write_state,
               o_ref, o_col0):
    nb, c, _ = q.shape
    g = _chunk_cumsum(log_f, causal)
    g_last = g[:, c - 1:c, :]
    q_dec = (q * jnp.exp(g)).astype(BF16)
    k_inv = (k * jnp.exp(-g)).astype(BF16)
    k_end = (k * jnp.exp(g_last - g)).astype(BF16)
    vb = v.astype(BF16)
    for h in range(n_heads):
        ks = slice(h * dk, (h + 1) * dk)
        vs = slice(h * dv, (h + 1) * dv)
        scores = jnp.where(causal, _bdot(q_dec[:, :, ks], k_inv[:, :, ks], 2, 2), 0.0).astype(BF16)
        o_intra = _bdot(scores, vb[:, :, vs], 2, 1)
        kv = _bdot(k_end[:, :, ks], vb[:, :, vs], 1, 1)
        decay = jnp.exp(_colsum_bcast(log_f[:, :, ks]))
        decay = jnp.concatenate([decay] * (dv // LANES), axis=-1)
        cols = slice(o_col0 + h * dv, o_col0 + (h + 1) * dv)
        if sequential:
            s = read_state(h)
            for b in range(nb):
                o = o_intra[b] + _dot(q_dec[b, :, ks], s.astype(BF16))
                s = decay[b] * s + kv[b]
                o_ref[b, :, cols] = (_rms(o, norm_w) * _silu(gate[b, :, vs])).astype(o_ref.dtype)
            write_state(h, s)
        else:
            s0 = read_state(h)
            o = o_intra + _bdot(q_dec[:, :, ks], s0.astype(BF16), 2, 1)
            write_state(h, decay * s0 + kv)
            o_ref[:, :, cols] = (_rms(o, norm_w) * _silu(gate[:, :, vs])).astype(o_ref.dtype)


def _expand_heads(x, expand):
    return _exact_bdot_rhs01(x, expand, 2, 1)


def _ssd_heads(xs, bs, cs, z, dt, a_row, d_row, norm_w, causal, sequential, read_state, write_state,
               o_ref, o_col0):
    nb, c, _ = xs.shape
    hrow = lax.broadcasted_iota(jnp.int32, (SSM_HEADS, SSM_INNER), 0)
    hcol = lax.broadcasted_iota(jnp.int32, (SSM_HEADS, SSM_INNER), 1) // SSM_HEAD_DIM
    expand2d = (hrow == hcol).astype(BF16)
    expand = jnp.broadcast_to(expand2d[None], (nb, SSM_HEADS, SSM_INNER))
    a_x = sum(_dot(p, expand2d) for p in _split3(a_row))
    d_x = sum(_dot(p, expand2d) for p in _split3(d_row))
    dta = dt * a_row
    cum = _chunk_cumsum(dta, causal)
    cum_t = _exact_bdot_rhs01(dta, _upper(nb, c), 1, 1)
    dt_x = _expand_heads(dt, expand)
    cum_x = _expand_heads(cum, expand)
    dta_x = dt_x * a_x
    xdt = xs * dt_x
    cum_last = cum_x[:, c - 1:c, :]
    x_end = (xdt * jnp.exp(cum_last - cum_x)).astype(BF16)
    chunk_dec = jnp.exp(cum_x)
    bsb = bs.astype(BF16)
    csb = cs.astype(BF16)
    lane_head = lax.broadcasted_iota(jnp.int32, (nb, c, SSM_GROUP_WIDTH), 2) // SSM_HEAD_DIM
    for g in range(SSM_GROUPS):
        gl = slice(g * SSM_STATE, (g + 1) * SSM_STATE)
        hl = slice(g * SSM_GROUP_WIDTH, (g + 1) * SSM_GROUP_WIDTH)
        cb = _bdot(csb[:, :, gl], bsb[:, :, gl], 2, 2)
        xdt_g = xdt[:, :, hl]
        y = None
        for r in range(HEADS_PER_GROUP):
            h = g * HEADS_PER_GROUP + r
            col = jnp.broadcast_to(cum[:, :, h:h + 1], (nb, c, c))
            row = cum_t[:, h:h + 1, :]
            dec = jnp.where(causal, jnp.exp(col - row), 0.0)
            lmat = (cb * dec).astype(BF16)
            xm = jnp.where(lane_head == r, xdt_g, 0.0).astype(BF16)
            part = _bdot(lmat, xm, 2, 1)
            y = part if y is None else y + part
        kv = _bdot(x_end[:, :, hl], bsb[:, :, gl], 1, 1)
        decay = jnp.exp(_colsum_bcast(dta_x[:, :, hl], SSM_STATE))
        cols = slice(o_col0 + g * SSM_GROUP_WIDTH, o_col0 + (g + 1) * SSM_GROUP_WIDTH)

        def finish(y_intra, y_inter, b):
            y_all = y_intra + y_inter * chunk_dec[b, :, hl] + d_x[:, hl] * xs[b, :, hl]
            y_all = _rms(y_all * _silu(z[b, :, hl]), norm_w[:, hl])
            o_ref[b, :, cols] = y_all.astype(o_ref.dtype)

        if sequential:
            s = read_state(g)
            for b in range(nb):
                y_inter = lax.dot_general(csb[b, :, gl], s.astype(BF16), (((1,), (1,)), ((), ())),
                                          preferred_element_type=F32)
                s = decay[b] * s + kv[b]
                finish(y[b], y_inter, b)
            write_state(g, s)
        else:
            s0 = read_state(g)
            y_inter = _bdot(csb[:, :, gl], s0.astype(BF16), 2, 2)
            write_state(g, decay * s0 + kv)
            finish(y, y_inter, slice(None))


def _upper(nb, c):
    r = lax.broadcasted_iota(jnp.int32, (nb, c, c), 1)
    col = lax.broadcasted_iota(jnp.int32, (nb, c, c), 2)
    return (r <= col).astype(BF16)


def _lower_bound(lb_logits):
    m = jnp.max(lb_logits, axis=0, keepdims=True)
    e = jnp.exp(lb_logits - m)
    return e[0:1, :] / jnp.sum(e, axis=0, keepdims=True)


def _ab_math(proj_ref, conv, lb_ref, hn_ref, dtb_ref, alog_ref, d_ref, sn_ref, sequential,
             read_h, write_h, read_s, write_s, o_ref):
    nb, c, _ = o_ref.shape
    causal = _causal(nb, c)
    lb = _lower_bound(lb_ref[...])
    f = lb + (1.0 - lb) * jax.nn.sigmoid(proj_ref[:, :, _F0:_F0 + HGRN_WIDTH])
    _gla_heads(_silu(proj_ref[:, :, _Q0:_Q0 + HGRN_WIDTH]), 1.0 - f, proj_ref[:, :, _I0:_I0 + HGRN_WIDTH],
               jnp.log(f), proj_ref[:, :, _G0:_G0 + HGRN_WIDTH], hn_ref[...], HGRN_HEADS, HGRN_HEAD_DIM,
               HGRN_HEAD_DIM, causal, sequential, read_h, write_h, o_ref, 0)
    act = _silu(conv)
    dt = _softplus(proj_ref[:, :, _DT0:_DT0 + SSM_HEADS] + dtb_ref[...])
    _ssd_heads(act[:, :, :SSM_INNER], act[:, :, SSM_INNER:SSM_INNER + SSM_GROUPS * SSM_STATE],
               act[:, :, SSM_INNER + SSM_GROUPS * SSM_STATE:], proj_ref[:, :, _Z0:_Z0 + SSM_INNER], dt,
               -jnp.exp(alog_ref[...]), d_ref[...], sn_ref[...], causal, sequential, read_s, write_s,
               o_ref, HGRN_WIDTH)


def _ab_seq_kernel(proj_ref, lb_ref, hn_ref, cw_ref, cbias_ref, dtb_ref, alog_ref, d_ref, sn_ref,
                   o_ref, sh_out, ss_out, sc_out, sh, ss, xpad):
    t = pl.program_id(1)
    nb, c, _ = o_ref.shape
    rows = nb * c

    @pl.when(t == 0)
    def _():
        sh[...] = jnp.zeros_like(sh)
        ss[...] = jnp.zeros_like(ss)
        xpad[0:SUBLANES, :] = jnp.zeros((SUBLANES, CONV_DIM), F32)

    xpad[SUBLANES:SUBLANES + rows, :] = proj_ref[:, :, _XBC0:_XBC0 + CONV_DIM].reshape(rows, CONV_DIM)
    conv = cbias_ref[...]
    for k in range(CONV_W):
        off = SUBLANES - (CONV_W - 1) + k
        conv = conv + xpad[off:off + rows, :] * cw_ref[k:k + 1, :]
    xpad[0:SUBLANES, :] = xpad[rows:rows + SUBLANES, :]
    conv = conv.reshape(nb, c, CONV_DIM)

    def read_h(h):
        return sh[h]

    def write_h(h, s):
        sh[h] = s

    def read_s(g):
        return ss[g]

    def write_s(g, s):
        ss[g] = s

    _ab_math(proj_ref, conv, lb_ref, hn_ref, dtb_ref, alog_ref, d_ref, sn_ref, True,
             read_h, write_h, read_s, write_s, o_ref)

    @pl.when(t == pl.num_programs(1) - 1)
    def _():
        sh_out[0] = sh[...]
        ss_out[0] = ss[...]
        sc_out[0] = xpad[SUBLANES - (CONV_W - 1):SUBLANES, :]


def _ab_par_kernel(proj_ref, sh_in, ss_in, sc_in, lb_ref, hn_ref, cw_ref, cbias_ref, dtb_ref, alog_ref,
                   d_ref, sn_ref, o_ref, sh_out, ss_out, sc_out, xpad):
    nb, c, _ = o_ref.shape
    nbuf = CONV_W - 1
    xpad[:, 0:nbuf, :] = sc_in[...]
    xpad[:, nbuf:nbuf + c, :] = proj_ref[:, :, _XBC0:_XBC0 + CONV_DIM]
    conv = cbias_ref[...]
    for k in range(CONV_W):
        conv = conv + xpad[:, k:k + c, :] * cw_ref[k:k + 1, :]
    sc_out[...] = xpad[:, c:c + nbuf, :]

    def read_h(h):
        return sh_in[:, h]

    def write_h(h, s):
        sh_out[:, h] = s

    def read_s(g):
        return ss_in[:, g]

    def write_s(g, s):
        ss_out[:, g] = s

    _ab_math(proj_ref, conv, lb_ref, hn_ref, dtb_ref, alog_ref, d_ref, sn_ref, False,
             read_h, write_h, read_s, write_s, o_ref)


def _gla_math(proj_ref, wgk_ref, bgk_ref, gn_ref, sequential, read_g, write_g, o_ref):
    nb, c, _ = o_ref.shape
    causal = _causal(nb, c)
    q = proj_ref[:, :, 0:GLA_KEY] * (GLA_HEAD_K ** -0.5)
    k = proj_ref[:, :, GLA_KEY:2 * GLA_KEY]
    v = proj_ref[:, :, 2 * GLA_KEY:2 * GLA_KEY + GLA_VAL]
    gate = proj_ref[:, :, 2 * GLA_KEY + GLA_VAL:2 * GLA_KEY + 2 * GLA_VAL]
    gk_low = proj_ref[:, :, 2 * GLA_KEY + 2 * GLA_VAL:2 * GLA_KEY + 2 * GLA_VAL + GK_RANK].astype(BF16)
    if c % SUBLANES == 0:
        gk = _dot(gk_low.reshape(nb * c, GK_RANK), wgk_ref[...]).reshape(nb, c, GLA_KEY)
    else:
        gk = _bdot(gk_low, jnp.broadcast_to(wgk_ref[...][None], (nb, GK_RANK, GLA_KEY)), 2, 1)
    gk = gk + bgk_ref[...]
    log_f = -_softplus(-gk) / GK_NORMALIZER
    _gla_heads(q, k, v, log_f, gate, gn_ref[...], GLA_HEADS, GLA_HEAD_K, GLA_HEAD_V, causal, sequential,
               read_g, write_g, o_ref, 0)


def _gla_seq_kernel(proj_ref, wgk_ref, bgk_ref, gn_ref, o_ref, sg_out, sg):
    t = pl.program_id(1)

    @pl.when(t == 0)
    def _():
        sg[...] = jnp.zeros_like(sg)

    def read_g(h):
        return sg[h]

    def write_g(h, s):
        sg[h] = s

    _gla_math(proj_ref, wgk_ref, bgk_ref, gn_ref, True, read_g, write_g, o_ref)

    @pl.when(t == pl.num_programs(1) - 1)
    def _():
        sg_out[0] = sg[...]


def _gla_par_kernel(proj_ref, sg_in, wgk_ref, bgk_ref, gn_ref, o_ref, sg_out):
    def read_g(h):
        return sg_in[:, h]

    def write_g(h, s):
        sg_out[:, h] = s

    _gla_math(proj_ref, wgk_ref, bgk_ref, gn_ref, False, read_g, write_g, o_ref)


SEQ_TILE_CHUNKS = 4
PAR_TILE_SEQS = 8


def _full(shape):
    return pl.BlockSpec(shape, lambda *_: (0,) * len(shape))


def _mixer_o_dtype(c):
    return BF16 if c % (2 * SUBLANES) == 0 else F32


def _ab_seq_call(proj, batch, length, params, *, name):
    c = math.gcd(length, CHUNK)
    nb = SEQ_TILE_CHUNKS
    tiles = length // (c * nb)
    assert tiles * c * nb == length
    width = HEADS_PER_GROUP * SSM_HEAD_DIM
    blk = lambda n: pl.BlockSpec((nb, c, n), lambda b, t: (b * tiles + t, 0, 0))
    out_shapes = [
        jax.ShapeDtypeStruct((batch * length // c, c, AB_WIDTH), _mixer_o_dtype(c)),
        jax.ShapeDtypeStruct((batch, HGRN_HEADS, HGRN_HEAD_DIM, HGRN_HEAD_DIM), F32),
        jax.ShapeDtypeStruct((batch, SSM_GROUPS, width, SSM_STATE), F32),
        jax.ShapeDtypeStruct((batch, CONV_W - 1, CONV_DIM), F32),
    ]
    out_specs = [
        blk(AB_WIDTH),
        pl.BlockSpec((1, HGRN_HEADS, HGRN_HEAD_DIM, HGRN_HEAD_DIM), lambda b, t: (b, 0, 0, 0)),
        pl.BlockSpec((1, SSM_GROUPS, width, SSM_STATE), lambda b, t: (b, 0, 0, 0)),
        pl.BlockSpec((1, CONV_W - 1, CONV_DIM), lambda b, t: (b, 0, 0)),
    ]
    return pl.pallas_call(
        _ab_seq_kernel,
        grid=(batch, tiles),
        in_specs=[blk(AB_PROJ_PAD)] + [_full(p.shape) for p in params],
        out_specs=out_specs,
        out_shape=out_shapes,
        scratch_shapes=[
            pltpu.VMEM((HGRN_HEADS, HGRN_HEAD_DIM, HGRN_HEAD_DIM), F32),
            pltpu.VMEM((SSM_GROUPS, width, SSM_STATE), F32),
            pltpu.VMEM((nb * c + SUBLANES, CONV_DIM), F32),
        ],
        compiler_params=pltpu.CompilerParams(dimension_semantics=("arbitrary", "arbitrary"),
                                             vmem_limit_bytes=VMEM_LIMIT_BYTES),
        name=name,
    )(proj, *params)


def _ab_par_call(proj, s_hgrn, s_ssm, s_conv, params, *, name):
    batch, c, _ = proj.shape
    nb = _row_tile(batch, PAR_TILE_SEQS)
    width = HEADS_PER_GROUP * SSM_HEAD_DIM
    s_ssm = s_ssm.reshape(batch, SSM_GROUPS, width, SSM_STATE)
    blk3 = lambda a, n: pl.BlockSpec((nb, a, n), lambda b: (b, 0, 0))
    blk4 = lambda a, r, n: pl.BlockSpec((nb, a, r, n), lambda b: (b, 0, 0, 0))
    state_specs = [blk4(HGRN_HEADS, HGRN_HEAD_DIM, HGRN_HEAD_DIM), blk4(SSM_GROUPS, width, SSM_STATE),
                   blk3(CONV_W - 1, CONV_DIM)]
    out_shapes = [
        jax.ShapeDtypeStruct((batch, c, AB_WIDTH), _mixer_o_dtype(c)),
        jax.ShapeDtypeStruct(s_hgrn.shape, F32),
        jax.ShapeDtypeStruct(s_ssm.shape, F32),
        jax.ShapeDtypeStruct(s_conv.shape, F32),
    ]
    return pl.pallas_call(
        _ab_par_kernel,
        grid=(batch // nb,),
        in_specs=[blk3(c, AB_PROJ_PAD)] + state_specs + [_full(p.shape) for p in params],
        out_specs=[blk3(c, AB_WIDTH)] + state_specs,
        out_shape=out_shapes,
        scratch_shapes=[pltpu.VMEM((nb, c + CONV_W - 1, CONV_DIM), F32)],
        compiler_params=pltpu.CompilerParams(dimension_semantics=("arbitrary",),
                                             vmem_limit_bytes=VMEM_LIMIT_BYTES),
        name=name,
    )(proj, s_hgrn, s_ssm, s_conv, *params)


def _gla_seq_call(proj, batch, length, params, *, name):
    c = math.gcd(length, CHUNK)
    nb = SEQ_TILE_CHUNKS
    tiles = length // (c * nb)
    assert tiles * c * nb == length
    blk = lambda n: pl.BlockSpec((nb, c, n), lambda b, t: (b * tiles + t, 0, 0))
    return pl.pallas_call(
        _gla_seq_kernel,
        grid=(batch, tiles),
        in_specs=[blk(GLA_PROJ_PAD)] + [_full(p.shape) for p in params],
        out_specs=[blk(GLA_VAL),
                   pl.BlockSpec((1, GLA_HEADS, GLA_HEAD_K, GLA_HEAD_V), lambda b, t: (b, 0, 0, 0))],
        out_shape=[jax.ShapeDtypeStruct((batch * length // c, c, GLA_VAL), _mixer_o_dtype(c)),
                   jax.ShapeDtypeStruct((batch, GLA_HEADS, GLA_HEAD_K, GLA_HEAD_V), F32)],
        scratch_shapes=[pltpu.VMEM((GLA_HEADS, GLA_HEAD_K, GLA_HEAD_V), F32)],
        compiler_params=pltpu.CompilerParams(dimension_semantics=("arbitrary", "arbitrary"),
                                             vmem_limit_bytes=VMEM_LIMIT_BYTES),
        name=name,
    )(proj, *params)


def _gla_par_call(proj, s_gla, params, *, name):
    batch, c, _ = proj.shape
    nb = _row_tile(batch, PAR_TILE_SEQS)
    blk3 = lambda a, n: pl.BlockSpec((nb, a, n), lambda b: (b, 0, 0))
    sspec = pl.BlockSpec((nb, GLA_HEADS, GLA_HEAD_K, GLA_HEAD_V), lambda b: (b, 0, 0, 0))
    return pl.pallas_call(
        _gla_par_kernel,
        grid=(batch // nb,),
        in_specs=[blk3(c, GLA_PROJ_PAD), sspec] + [_full(p.shape) for p in params],
        out_specs=[blk3(c, GLA_VAL), sspec],
        out_shape=[jax.ShapeDtypeStruct((batch, c, GLA_VAL), _mixer_o_dtype(c)),
                   jax.ShapeDtypeStruct(s_gla.shape, F32)],
        compiler_params=pltpu.CompilerParams(dimension_semantics=("arbitrary",),
                                             vmem_limit_bytes=VMEM_LIMIT_BYTES),
        name=name,
    )(proj, s_gla, *params)


def _split_cols(w):
    main = w.shape[1] // LANES * LANES
    tail = jnp.pad(w[:, main:].astype(BF16), ((0, 0), (0, LANES - (w.shape[1] - main))))
    return w[:, :main].astype(BF16), tail


def kernel(x_prompt, x_sample, state_hgrn, state_ssm, state_conv, state_gla, norm_ffn1, norm_mix, norm_ffn2, norm_final, ffn1_w_in, ffn1_w_out, ffn2_w_in, ffn2_w_out, ab_w_in, ab_w_out, hgrn_lb_logits, hgrn_norm, ssm_conv_w, ssm_conv_b, ssm_dt_bias, ssm_a_log, ssm_d, ssm_norm, gla_w_in, gla_w_gk, gla_b_gk, gla_norm, gla_w_out):
    assert ab_w_in.shape[0] == 1 and gla_w_in.shape[0] == 1, "one HGRN2/SSD layer and one GLA layer"
    bp, lp, _ = x_prompt.shape
    bs, ls, _ = x_sample.shape
    rp, rs = bp * lp, bs * ls
    cp, cs = math.gcd(lp, CHUNK), math.gcd(ls, CHUNK)
    assert rs % cp == 0
    row = lambda v: v.reshape(1, -1)
    stack_row = lambda v: v.reshape(v.shape[0], 1, v.shape[1])
    nf1, nm, nf2 = stack_row(norm_ffn1), stack_row(norm_mix), stack_row(norm_ffn2)
    w1_in, w1_out = ffn1_w_in.astype(BF16), ffn1_w_out.astype(BF16)
    w2_in, w2_out = ffn2_w_in.astype(BF16), ffn2_w_out.astype(BF16)
    xp = x_prompt.reshape(rp, D_MODEL)
    xs = x_sample.reshape(rs, D_MODEL)

    wp, wt = _split_cols(ab_w_in[0])
    x1, proj = _pre_call(xp, xs, 0, nf1, w1_in, w1_out, nm, wp, wt, name="pre0")
    ab_params = (hgrn_lb_logits, row(hgrn_norm[0]), ssm_conv_w[0], row(ssm_conv_b[0]), row(ssm_dt_bias[0]),
                 row(ssm_a_log[0]), row(ssm_d[0]), row(ssm_norm[0]))
    o_p, hgrn_p, ssm_p, conv_p = _ab_seq_call(proj.reshape((rp + rs) // cp, cp, AB_PROJ_PAD), bp, lp, ab_params,
                                              name="mix0_prompt")
    o_s, hgrn_s, ssm_s, conv_s = _ab_par_call(proj[rp:].reshape(bs, cs, AB_PROJ_PAD), state_hgrn[0], state_ssm[0],
                                              state_conv[0], ab_params, name="mix0_sample")
    (x3,) = _post_call(x1, o_p.reshape(rp, AB_WIDTH), o_s.reshape(rs, AB_WIDTH), 0, ab_w_out[0].astype(BF16), nf2,
                       w2_in, w2_out, row(norm_final), split_output=False, name="post0")

    wp, wt = _split_cols(gla_w_in[0])
    x4, proj = _pre_call(x3, None, 1, nf1, w1_in, w1_out, nm, wp, wt, name="pre1")
    gla_params = (gla_w_gk[0].astype(BF16), row(gla_b_gk[0]), row(gla_norm[0]))
    o_p, gla_p = _gla_seq_call(proj.reshape((rp + rs) // cp, cp, GLA_PROJ_PAD), bp, lp, gla_params, name="mix1_prompt")
    o_s, gla_s = _gla_par_call(proj[rp:].reshape(bs, cs, GLA_PROJ_PAD), state_gla[0], gla_params, name="mix1_sample")
    y_p, y_s = _post_call(x4, o_p.reshape(rp, GLA_VAL), o_s.reshape(rs, GLA_VAL), 1, gla_w_out[0].astype(BF16), nf2,
                          w2_in, w2_out, row(norm_final), split_output=True, name="post1")

    ssm_shape = (1, -1, SSM_HEADS, SSM_HEAD_DIM, SSM_STATE)
    return (y_p.reshape(bp, lp, D_MODEL), y_s.reshape(bs, ls, D_MODEL), hgrn_p[None], hgrn_s[None],
            ssm_p.reshape(ssm_shape), ssm_s.reshape(ssm_shape), conv_p[None], conv_s[None], gla_p[None], gla_s[None])
```

```python
import functools
import math

import jax
import jax.numpy as jnp
from jax import lax
from jax.experimental import pallas as pl
from jax.experimental.pallas import tpu as pltpu

F32 = jnp.float32
BF16 = jnp.bfloat16

D_MODEL = 1024
D_FF = 2816
EPS = 1e-6
CHUNK = 64

HGRN_HEADS = 4
HGRN_HEAD_DIM = 128
HGRN_WIDTH = HGRN_HEADS * HGRN_HEAD_DIM

SSM_HEADS = 8
SSM_HEAD_DIM = 64
SSM_INNER = SSM_HEADS * SSM_HEAD_DIM
SSM_GROUPS = 2
SSM_STATE = 128
SSM_GROUP_WIDTH = SSM_INNER // SSM_GROUPS
HEADS_PER_GROUP = SSM_HEADS // SSM_GROUPS
CONV_W = 4
CONV_DIM = SSM_INNER + 2 * SSM_GROUPS * SSM_STATE
AB_PROJ = 4 * HGRN_WIDTH + SSM_INNER + CONV_DIM + SSM_HEADS
AB_WIDTH = HGRN_WIDTH + SSM_INNER

GLA_HEADS = 4
GLA_HEAD_K = 128
GLA_HEAD_V = 256
GLA_KEY = GLA_HEADS * GLA_HEAD_K
GLA_VAL = GLA_HEADS * GLA_HEAD_V
GK_RANK = 16
GK_NORMALIZER = 16.0
GLA_PROJ = 2 * GLA_KEY + 2 * GLA_VAL + GK_RANK

LANES = 128
SUBLANES = 8
VMEM_LIMIT_BYTES = 56 * 1024 * 1024

AB_PROJ_PAD = -(-AB_PROJ // LANES) * LANES
GLA_PROJ_PAD = -(-GLA_PROJ // LANES) * LANES

_Q0, _F0, _I0, _G0 = 0, HGRN_WIDTH, 2 * HGRN_WIDTH, 3 * HGRN_WIDTH
_Z0 = 4 * HGRN_WIDTH
_XBC0 = _Z0 + SSM_INNER
_DT0 = _XBC0 + CONV_DIM


def _rms(x, w):
    return x * lax.rsqrt(jnp.mean(x * x, axis=-1, keepdims=True) + EPS) * w


def _silu(x):
    return x * jax.nn.sigmoid(x)


def _softplus(x):
    return jnp.maximum(x, 0.0) + jnp.log1p(jnp.exp(-jnp.abs(x)))


def _dot(a, b):
    return jnp.dot(a, b, preferred_element_type=F32)


def _bdot(a, b, ca, cb):
    return lax.dot_general(a, b, (((ca,), (cb,)), ((0,), (0,))), preferred_element_type=F32)


def _split3(x):
    hi = x.astype(BF16)
    r1 = x - hi.astype(F32)
    mid = r1.astype(BF16)
    lo = (r1 - mid.astype(F32)).astype(BF16)
    return hi, mid, lo


def _exact_bdot_lhs01(m01, x, ca, cb):
    return sum(_bdot(m01, p, ca, cb) for p in _split3(x))


def _exact_bdot_rhs01(x, m01, ca, cb):
    return sum(_bdot(p, m01, ca, cb) for p in _split3(x))


def _causal(nb, c):
    r = lax.broadcasted_iota(jnp.int32, (nb, c, c), 1)
    col = lax.broadcasted_iota(jnp.int32, (nb, c, c), 2)
    return r >= col


def _chunk_cumsum(x, causal):
    return _exact_bdot_lhs01(causal.astype(BF16), x, 2, 1)


def _colsum_bcast(x, lanes=LANES):
    nb, c, _ = x.shape
    return _exact_bdot_rhs01(x, jnp.ones((nb, c, lanes), BF16), 1, 1)


MXU_DIM = 256
FF_TILES = ((0, 6 * MXU_DIM), (6 * MXU_DIM, D_FF))
assert all(lo % MXU_DIM == 0 and hi % MXU_DIM == 0 for lo, hi in FF_TILES)


def _ffn_half(x, norm_w, w_in_ref, w_out_ref):
    hb = _rms(x, norm_w).astype(BF16)
    acc = None
    for lo, hi in FF_TILES:
        gate = _dot(hb, w_in_ref[:, lo:hi])
        up = _dot(hb, w_in_ref[:, D_FF + lo:D_FF + hi])
        act = (_silu(gate) * up).astype(BF16)
        part = _dot(act, w_out_ref[lo:hi, :])
        acc = part if acc is None else acc + part
    return 0.5 * acc


def _pick_group(first_steps, a, b):
    return jnp.where(pl.program_id(0) < first_steps, a, b)


def _pre_kernel(*refs, first_steps):
    if first_steps is None:
        x_ref, nf_ref, w_in_ref, w_out_ref, nm_ref, wp_ref, wt_ref, x1_ref, proj_ref = refs
        x = x_ref[...]
    else:
        xa_ref, xb_ref, nf_ref, w_in_ref, w_out_ref, nm_ref, wp_ref, wt_ref, x1_ref, proj_ref = refs
        x = _pick_group(first_steps, xa_ref[...], xb_ref[...])
    x1 = x + _ffn_half(x, nf_ref[...], w_in_ref, w_out_ref)
    x1_ref[...] = x1
    hb = _rms(x1, nm_ref[...]).astype(BF16)
    main = wp_ref.shape[1]
    proj_ref[:, :main] = _dot(hb, wp_ref[...])
    proj_ref[:, main:] = _dot(hb, wt_ref[...])


def _post_kernel(x_ref, oa_ref, ob_ref, wo_ref, nf_ref, w_in_ref, w_out_ref, nfin_ref, *y_refs, first_steps):
    o = _pick_group(first_steps, oa_ref[...].astype(BF16), ob_ref[...].astype(BF16))
    x2 = x_ref[...] + _dot(o, wo_ref[...])
    y = x2 + _ffn_half(x2, nf_ref[...], w_in_ref, w_out_ref)
    if len(y_refs) == 1:
        y_refs[0][...] = y
    else:
        y = _rms(y, nfin_ref[...])
        ya_ref, yb_ref = y_refs

        @pl.when(pl.program_id(0) < first_steps)
        def _():
            ya_ref[...] = y

        @pl.when(pl.program_id(0) >= first_steps)
        def _():
            yb_ref[...] = y


def _resident(shape, layer=None):
    if layer is None:
        return pl.BlockSpec(shape, lambda *_: (0,) * len(shape), pipeline_mode=pl.Buffered(1))
    return pl.BlockSpec((None,) + tuple(shape[1:]), lambda *_: (layer,) + (0,) * (len(shape) - 1),
                        pipeline_mode=pl.Buffered(1))


def _row_tile(rows, want):
    t = min(rows, want)
    assert rows % t == 0
    return t


PRE_ROWS = 256
POST_ROWS = 512


def _group_specs(tm, first_steps, width):
    first = pl.BlockSpec((tm, width), lambda i: (jnp.minimum(i, first_steps - 1), 0))
    second = pl.BlockSpec((tm, width), lambda i: (jnp.maximum(i - first_steps, 0), 0))
    return first, second


def _pre_call(xa, xb, layer, nf, w_in, w_out, nm, wp, wt, *, name):
    ra = xa.shape[0]
    rb = 0 if xb is None else xb.shape[0]
    rows = ra + rb
    tm = _row_tile(rb if rb else ra, PRE_ROWS)
    assert ra % tm == 0
    npad = wp.shape[1] + wt.shape[1]
    tok = lambda n: pl.BlockSpec((tm, n), lambda i: (i, 0))
    if xb is None:
        first_steps, x_specs, xs = None, [tok(D_MODEL)], (xa,)
    else:
        first_steps = ra // tm
        x_specs, xs = list(_group_specs(tm, first_steps, D_MODEL)), (xa, xb)
    return pl.pallas_call(
        functools.partial(_pre_kernel, first_steps=first_steps),
        grid=(rows // tm,),
        in_specs=x_specs + [_resident(nf.shape, layer), _resident(w_in.shape, layer),
                            _resident(w_out.shape, layer), _resident(nm.shape, layer), _resident(wp.shape),
                            _resident(wt.shape)],
        out_specs=[tok(D_MODEL), tok(npad)],
        out_shape=[jax.ShapeDtypeStruct((rows, D_MODEL), F32), jax.ShapeDtypeStruct((rows, npad), F32)],
        compiler_params=pltpu.CompilerParams(dimension_semantics=("arbitrary",), vmem_limit_bytes=VMEM_LIMIT_BYTES),
        name=name,
    )(*xs, nf, w_in, w_out, nm, wp, wt)


def _post_call(x, oa, ob, layer, wo, nf, w_in, w_out, nfin, *, split_output, name):
    rows = x.shape[0]
    ra, rb = oa.shape[0], ob.shape[0]
    assert ra + rb == rows
    tm = _row_tile(rb, POST_ROWS)
    assert ra % tm == 0
    first_steps = ra // tm
    tok = lambda n: pl.BlockSpec((tm, n), lambda i: (i, 0))
    spec_a, spec_b = _group_specs(tm, first_steps, oa.shape[1])
    if split_output:
        out_specs = list(_group_specs(tm, first_steps, D_MODEL))
        out_shape = [jax.ShapeDtypeStruct((ra, D_MODEL), F32), jax.ShapeDtypeStruct((rb, D_MODEL), F32)]
    else:
        out_specs = [tok(D_MODEL)]
        out_shape = [jax.ShapeDtypeStruct((rows, D_MODEL), F32)]
    return pl.pallas_call(
        functools.partial(_post_kernel, first_steps=first_steps),
        grid=(rows // tm,),
        in_specs=[tok(D_MODEL), spec_a, spec_b, _resident(wo.shape), _resident(nf.shape, layer),
                  _resident(w_in.shape, layer), _resident(w_out.shape, layer), _resident(nfin.shape)],
        out_specs=out_specs,
        out_shape=out_shape,
        compiler_params=pltpu.CompilerParams(dimension_semantics=("arbitrary",), vmem_limit_bytes=VMEM_LIMIT_BYTES),
        name=name,
    )(x, oa, ob, wo, nf, w_in, w_out, nfin)


def _gla_heads(q, k, v, log_f, gate, norm_w, n_heads, dk, dv, causal, sequential, read_state, write_state,
               o_ref, o_col0):
    nb, c, _ = q.shape
    g = _chunk_cumsum(log_f, causal)
    g_last = g[:, c - 1:c, :]
    q_dec = (q * jnp.exp(g)).astype(BF16)
    k_inv = (k * jnp.exp(-g)).astype(BF16)
    k_end = (k * jnp.exp(g_last - g)).astype(BF16)
    vb = v.astype(BF16)
    heads = range(n_heads)
    ks = [slice(h * dk, (h + 1) * dk) for h in heads]
    vs = [slice(h * dv, (h + 1) * dv) for h in heads]
    scores = [_bdot(q_dec[:, :, ks[h]], k_inv[:, :, ks[h]], 2, 2) for h in heads]
    scores = [jnp.where(causal, sc, 0.0).astype(BF16) for sc in scores]
    o_intra = [_bdot(scores[h], vb[:, :, vs[h]], 2, 1) for h in heads]

    def emit(h, b, o):
        cols = slice(o_col0 + h * dv, o_col0 + (h + 1) * dv)
        o_ref[b, :, cols] = (_rms(o, norm_w) * _silu(gate[b, :, vs[h]])).astype(o_ref.dtype)

    if sequential:
        kv_t = [_bdot(vb[:, :, vs[h]], k_end[:, :, ks[h]], 1, 1) for h in heads]
        decay = jnp.exp(g_last)
        for b in range(nb):
            for h in heads:
                s_t = read_state(h)
                o_inter = lax.dot_general(q_dec[b, :, ks[h]], s_t.astype(BF16), (((1,), (1,)), ((), ())),
                                          preferred_element_type=F32)
                write_state(h, decay[b, :, ks[h]] * s_t + kv_t[h][b])
                emit(h, b, o_intra[h][b] + o_inter)
    else:
        for h in heads:
            kv = _bdot(k_end[:, :, ks[h]], vb[:, :, vs[h]], 1, 1)
            decay = jnp.exp(_colsum_bcast(log_f[:, :, ks[h]]))
            decay = jnp.concatenate([decay] * (dv // LANES), axis=-1)
            s0 = read_state(h)
            o_inter = _bdot(q_dec[:, :, ks[h]], s0.astype(BF16), 2, 1)
            write_state(h, decay * s0 + kv)
            emit(h, slice(None), o_intra[h] + o_inter)


def _expand_heads(x, expand):
    return _exact_bdot_rhs01(x, expand, 2, 1)


def _ssd_heads(xs, bs, cs, z, dt, a_row, d_row, norm_w, causal, sequential, read_state, write_state,
               o_ref, o_col0):
    nb, c, _ = xs.shape
    hrow = lax.broadcasted_iota(jnp.int32, (SSM_HEADS, SSM_INNER), 0)
    hcol = lax.broadcasted_iota(jnp.int32, (SSM_HEADS, SSM_INNER), 1) // SSM_HEAD_DIM
    expand2d = (hrow == hcol).astype(BF16)
    expand = jnp.broadcast_to(expand2d[None], (nb, SSM_HEADS, SSM_INNER))
    d_x = sum(_dot(p, expand2d) for p in _split3(d_row))
    dta = dt * a_row
    cum = _chunk_cumsum(dta, causal)
    cum_t = _exact_bdot_rhs01(dta, _upper(nb, c), 1, 1)
    dt_x = _expand_heads(dt, expand)
    cum_x = _expand_heads(cum, expand)
    xdt = xs * dt_x
    cum_last = cum_x[:, c - 1:c, :]
    x_end = (xdt * jnp.exp(cum_last - cum_x)).astype(BF16)
    chunk_dec = jnp.exp(cum_x)
    bsb = bs.astype(BF16)
    csb = cs.astype(BF16)
    lane_head = lax.broadcasted_iota(jnp.int32, (nb, c, SSM_GROUP_WIDTH), 2) // SSM_HEAD_DIM
    groups = range(SSM_GROUPS)
    gl = [slice(g * SSM_STATE, (g + 1) * SSM_STATE) for g in groups]
    hl = [slice(g * SSM_GROUP_WIDTH, (g + 1) * SSM_GROUP_WIDTH) for g in groups]
    cb = [_bdot(csb[:, :, gl[g]], bsb[:, :, gl[g]], 2, 2) for g in groups]
    y_intra = []
    for g in groups:
        xdt_g = xdt[:, :, hl[g]]
        y = None
        for r in range(HEADS_PER_GROUP):
            h = g * HEADS_PER_GROUP + r
            col = jnp.broadcast_to(cum[:, :, h:h + 1], (nb, c, c))
            row = cum_t[:, h:h + 1, :]
            dec = jnp.where(causal, jnp.exp(col - row), 0.0)
            lmat = (cb[g] * dec).astype(BF16)
            xm = jnp.where(lane_head == r, xdt_g, 0.0).astype(BF16)
            part = _bdot(lmat, xm, 2, 1)
            y = part if y is None else y + part
        y_intra.append(y)

    def finish(g, y_in, y_inter, b):
        y_all = y_in + y_inter * chunk_dec[b, :, hl[g]] + d_x[:, hl[g]] * xs[b, :, hl[g]]
        y_all = _rms(y_all * _silu(z[b, :, hl[g]]), norm_w[:, hl[g]])
        cols = slice(o_col0 + g * SSM_GROUP_WIDTH, o_col0 + (g + 1) * SSM_GROUP_WIDTH)
        o_ref[b, :, cols] = y_all.astype(o_ref.dtype)

    if sequential:
        kv_t = [_bdot(bsb[:, :, gl[g]], x_end[:, :, hl[g]], 1, 1) for g in groups]
        for b in range(nb):
            for g in groups:
                s_t = read_state(g)
                y_inter = _dot(csb[b, :, gl[g]], s_t.astype(BF16))
                write_state(g, chunk_dec[b, c - 1:c, hl[g]] * s_t + kv_t[g][b])
                finish(g, y_intra[g][b], y_inter, b)
    else:
        a_x = sum(_dot(p, expand2d) for p in _split3(a_row))
        dta_x = dt_x * a_x
        for g in groups:
            kv = _bdot(x_end[:, :, hl[g]], bsb[:, :, gl[g]], 1, 1)
            decay = jnp.exp(_colsum_bcast(dta_x[:, :, hl[g]], SSM_STATE))
            s0 = read_state(g)
            y_inter = _bdot(csb[:, :, gl[g]], s0.astype(BF16), 2, 2)
            write_state(g, decay * s0 + kv)
            finish(g, y_intra[g], y_inter, slice(None))


def _upper(nb, c):
    r = lax.broadcasted_iota(jnp.int32, (nb, c, c), 1)
    col = lax.broadcasted_iota(jnp.int32, (nb, c, c), 2)
    return (r <= col).astype(BF16)


def _lower_bound(lb_logits):
    m = jnp.max(lb_logits, axis=0, keepdims=True)
    e = jnp.exp(lb_logits - m)
    return e[0:1, :] / jnp.sum(e, axis=0, keepdims=True)


def _ab_math(proj_ref, conv, lb_ref, hn_ref, dtb_ref, alog_ref, d_ref, sn_ref, sequential,
             read_h, write_h, read_s, write_s, o_ref):
    nb, c, _ = o_ref.shape
    causal = _causal(nb, c)
    lb = _lower_bound(lb_ref[...])
    f = lb + (1.0 - lb) * jax.nn.sigmoid(proj_ref[:, :, _F0:_F0 + HGRN_WIDTH])
    _gla_heads(_silu(proj_ref[:, :, _Q0:_Q0 + HGRN_WIDTH]), 1.0 - f, proj_ref[:, :, _I0:_I0 + HGRN_WIDTH],
               jnp.log(f), proj_ref[:, :, _G0:_G0 + HGRN_WIDTH], hn_ref[...], HGRN_HEADS, HGRN_HEAD_DIM,
               HGRN_HEAD_DIM, causal, sequential, read_h, write_h, o_ref, 0)
    act = _silu(conv)
    dt = _softplus(proj_ref[:, :, _DT0:_DT0 + SSM_HEADS] + dtb_ref[...])
    _ssd_heads(act[:, :, :SSM_INNER], act[:, :, SSM_INNER:SSM_INNER + SSM_GROUPS * SSM_STATE],
               act[:, :, SSM_INNER + SSM_GROUPS * SSM_STATE:], proj_ref[:, :, _Z0:_Z0 + SSM_INNER], dt,
               -jnp.exp(alog_ref[...]), d_ref[...], sn_ref[...], causal, sequential, read_s, write_s,
               o_ref, HGRN_WIDTH)


def _ab_seq_kernel(proj_ref, lb_ref, hn_ref, cw_ref, cbias_ref, dtb_ref, alog_ref, d_ref, sn_ref,
                   o_ref, sh_out, ss_out, sc_out, sh, ss, xpad):
    t = pl.program_id(1)
    nb, c, _ = o_ref.shape
    rows = nb * c

    @pl.when(t == 0)
    def _():
        sh[...] = jnp.zeros_like(sh)
        ss[...] = jnp.zeros_like(ss)
        xpad[0:SUBLANES, :] = jnp.zeros((SUBLANES, CONV_DIM), F32)

    xpad[SUBLANES:SUBLANES + rows, :] = proj_ref[:, :, _XBC0:_XBC0 + CONV_DIM].reshape(rows, CONV_DIM)
    padded = xpad[...]
    conv = cbias_ref[...] + padded[SUBLANES:] * cw_ref[CONV_W - 1:CONV_W, :]
    for d in range(1, CONV_W):
        conv = conv + pltpu.roll(padded, d, 0)[SUBLANES:] * cw_ref[CONV_W - 1 - d:CONV_W - d, :]
    xpad[0:SUBLANES, :] = padded[rows:rows + SUBLANES]
    conv = conv.reshape(nb, c, CONV_DIM)

    def read_h(h):
        return sh[h]

    def write_h(h, s):
        sh[h] = s

    def read_s(g):
        return ss[g]

    def write_s(g, s):
        ss[g] = s

    _ab_math(proj_ref, conv, lb_ref, hn_ref, dtb_ref, alog_ref, d_ref, sn_ref, True,
             read_h, write_h, read_s, write_s, o_ref)

    @pl.when(t == pl.num_programs(1) - 1)
    def _():
        for h in range(HGRN_HEADS):
            sh_out[0, h] = sh[h].T
        for g in range(SSM_GROUPS):
            ss_out[0, g] = ss[g].T
        sc_out[0] = xpad[SUBLANES - (CONV_W - 1):SUBLANES, :]


def _ab_par_kernel(proj_ref, sh_in, ss_in, sc_in, lb_ref, hn_ref, cw_ref, cbias_ref, dtb_ref, alog_ref,
                   d_ref, sn_ref, o_ref, sh_out, ss_out, sc_out, xpad):
    nb, c, _ = o_ref.shape
    nbuf = CONV_W - 1
    xpad[:, 0:nbuf, :] = sc_in[...]
    xpad[:, nbuf:nbuf + c, :] = proj_ref[:, :, _XBC0:_XBC0 + CONV_DIM]
    conv = cbias_ref[...]
    for k in range(CONV_W):
        conv = conv + xpad[:, k:k + c, :] * cw_ref[k:k + 1, :]
    sc_out[...] = xpad[:, c:c + nbuf, :]

    def read_h(h):
        return sh_in[:, h]

    def write_h(h, s):
        sh_out[:, h] = s

    def read_s(g):
        return ss_in[:, g]

    def write_s(g, s):
        ss_out[:, g] = s

    _ab_math(proj_ref, conv, lb_ref, hn_ref, dtb_ref, alog_ref, d_ref, sn_ref, False,
             read_h, write_h, read_s, write_s, o_ref)


def _gla_math(proj_ref, wgk_ref, bgk_ref, gn_ref, sequential, read_g, write_g, o_ref):
    nb, c, _ = o_ref.shape
    causal = _causal(nb, c)
    q = proj_ref[:, :, 0:GLA_KEY] * (GLA_HEAD_K ** -0.5)
    k = proj_ref[:, :, GLA_KEY:2 * GLA_KEY]
    v = proj_ref[:, :, 2 * GLA_KEY:2 * GLA_KEY + GLA_VAL]
    gate = proj_ref[:, :, 2 * GLA_KEY + GLA_VAL:2 * GLA_KEY + 2 * GLA_VAL]
    gk_low = proj_ref[:, :, 2 * GLA_KEY + 2 * GLA_VAL:2 * GLA_KEY + 2 * GLA_VAL + GK_RANK].astype(BF16)
    if c % SUBLANES == 0:
        gk = _dot(gk_low.reshape(nb * c, GK_RANK), wgk_ref[...]).reshape(nb, c, GLA_KEY)
    else:
        gk = _bdot(gk_low, jnp.broadcast_to(wgk_ref[...][None], (nb, GK_RANK, GLA_KEY)), 2, 1)
    gk = gk + bgk_ref[...]
    log_f = -_softplus(-gk) / GK_NORMALIZER
    _gla_heads(q, k, v, log_f, gate, gn_ref[...], GLA_HEADS, GLA_HEAD_K, GLA_HEAD_V, causal, sequential,
               read_g, write_g, o_ref, 0)


def _gla_seq_kernel(proj_ref, wgk_ref, bgk_ref, gn_ref, o_ref, sg_out, sg):
    t = pl.program_id(1)

    @pl.when(t == 0)
    def _():
        sg[...] = jnp.zeros_like(sg)

    def read_g(h):
        return sg[h]

    def write_g(h, s):
        sg[h] = s

    _gla_math(proj_ref, wgk_ref, bgk_ref, gn_ref, True, read_g, write_g, o_ref)

    @pl.when(t == pl.num_programs(1) - 1)
    def _():
        for h in range(GLA_HEADS):
            sg_out[0, h] = sg[h].T


def _gla_par_kernel(proj_ref, sg_in, wgk_ref, bgk_ref, gn_ref, o_ref, sg_out):
    def read_g(h):
        return sg_in[:, h]

    def write_g(h, s):
        sg_out[:, h] = s

    _gla_math(proj_ref, wgk_ref, bgk_ref, gn_ref, False, read_g, write_g, o_ref)


SEQ_TILE_CHUNKS = 8
PAR_TILE_SEQS = 8


def _full(shape):
    return pl.BlockSpec(shape, lambda *_: (0,) * len(shape))


def _mixer_o_dtype(c):
    return BF16 if c % (2 * SUBLANES) == 0 else F32


def _ab_seq_call(proj, batch, length, params, *, name):
    c = math.gcd(length, CHUNK)
    nb = SEQ_TILE_CHUNKS
    tiles = length // (c * nb)
    assert tiles * c * nb == length
    width = HEADS_PER_GROUP * SSM_HEAD_DIM
    blk = lambda n: pl.BlockSpec((nb, c, n), lambda b, t: (b * tiles + t, 0, 0))
    out_shapes = [
        jax.ShapeDtypeStruct((batch * length // c, c, AB_WIDTH), _mixer_o_dtype(c)),
        jax.ShapeDtypeStruct((batch, HGRN_HEADS, HGRN_HEAD_DIM, HGRN_HEAD_DIM), F32),
        jax.ShapeDtypeStruct((batch, SSM_GROUPS, width, SSM_STATE), F32),
        jax.ShapeDtypeStruct((batch, CONV_W - 1, CONV_DIM), F32),
    ]
    out_specs = [
        blk(AB_WIDTH),
        pl.BlockSpec((1, HGRN_HEADS, HGRN_HEAD_DIM, HGRN_HEAD_DIM), lambda b, t: (b, 0, 0, 0)),
        pl.BlockSpec((1, SSM_GROUPS, width, SSM_STATE), lambda b, t: (b, 0, 0, 0)),
        pl.BlockSpec((1, CONV_W - 1, CONV_DIM), lambda b, t: (b, 0, 0)),
    ]
    return pl.pallas_call(
        _ab_seq_kernel,
        grid=(batch, tiles),
        in_specs=[blk(AB_PROJ_PAD)] + [_full(p.shape) for p in params],
        out_specs=out_specs,
        out_shape=out_shapes,
        scratch_shapes=[
            pltpu.VMEM((HGRN_HEADS, HGRN_HEAD_DIM, HGRN_HEAD_DIM), F32),
            pltpu.VMEM((SSM_GROUPS, SSM_STATE, width), F32),
            pltpu.VMEM((nb * c + SUBLANES, CONV_DIM), F32),
        ],
        compiler_params=pltpu.CompilerParams(dimension_semantics=("arbitrary", "arbitrary"),
                                             vmem_limit_bytes=VMEM_LIMIT_BYTES),
        name=name,
    )(proj, *params)


def _ab_par_call(proj, s_hgrn, s_ssm, s_conv, params, *, name):
    batch, c, _ = proj.shape
    nb = _row_tile(batch, PAR_TILE_SEQS)
    width = HEADS_PER_GROUP * SSM_HEAD_DIM
    s_ssm = s_ssm.reshape(batch, SSM_GROUPS, width, SSM_STATE)
    blk3 = lambda a, n: pl.BlockSpec((nb, a, n), lambda b: (b, 0, 0))
    blk4 = lambda a, r, n: pl.BlockSpec((nb, a, r, n), lambda b: (b, 0, 0, 0))
    state_specs = [blk4(HGRN_HEADS, HGRN_HEAD_DIM, HGRN_HEAD_DIM), blk4(SSM_GROUPS, width, SSM_STATE),
                   blk3(CONV_W - 1, CONV_DIM)]
    out_shapes = [
        jax.ShapeDtypeStruct((batch, c, AB_WIDTH), _mixer_o_dtype(c)),
        jax.ShapeDtypeStruct(s_hgrn.shape, F32),
        jax.ShapeDtypeStruct(s_ssm.shape, F32),
        jax.ShapeDtypeStruct(s_conv.shape, F32),
    ]
    return pl.pallas_call(
        _ab_par_kernel,
        grid=(batch // nb,),
        in_specs=[blk3(c, AB_PROJ_PAD)] + state_specs + [_full(p.shape) for p in params],
        out_specs=[blk3(c, AB_WIDTH)] + state_specs,
        out_shape=out_shapes,
        scratch_shapes=[pltpu.VMEM((nb, c + CONV_W - 1, CONV_DIM), F32)],
        compiler_params=pltpu.CompilerParams(dimension_semantics=("arbitrary",),
                                             vmem_limit_bytes=VMEM_LIMIT_BYTES),
        name=name,
    )(proj, s_hgrn, s_ssm, s_conv, *params)


def _gla_seq_call(proj, batch, length, params, *, name):
    c = math.gcd(length, CHUNK)
    nb = SEQ_TILE_CHUNKS
    tiles = length // (c * nb)
    assert tiles * c * nb == length
    blk = lambda n: pl.BlockSpec((nb, c, n), lambda b, t: (b * tiles + t, 0, 0))
    return pl.pallas_call(
        _gla_seq_kernel,
        grid=(batch, tiles),
        in_specs=[blk(GLA_PROJ_PAD)] + [_full(p.shape) for p in params],
        out_specs=[blk(GLA_VAL),
                   pl.BlockSpec((1, GLA_HEADS, GLA_HEAD_K, GLA_HEAD_V), lambda b, t: (b, 0, 0, 0))],
        out_shape=[jax.ShapeDtypeStruct((batch * length // c, c, GLA_VAL), _mixer_o_dtype(c)),
                   jax.ShapeDtypeStruct((batch, GLA_HEADS, GLA_HEAD_K, GLA_HEAD_V), F32)],
        scratch_shapes=[pltpu.VMEM((GLA_HEADS, GLA_HEAD_V, GLA_HEAD_K), F32)],
        compiler_params=pltpu.CompilerParams(dimension_semantics=("arbitrary", "arbitrary"),
                                             vmem_limit_bytes=VMEM_LIMIT_BYTES),
        name=name,
    )(proj, *params)


def _gla_par_call(proj, s_gla, params, *, name):
    batch, c, _ = proj.shape
    nb = _row_tile(batch, PAR_TILE_SEQS)
    blk3 = lambda a, n: pl.BlockSpec((nb, a, n), lambda b: (b, 0, 0))
    sspec = pl.BlockSpec((nb, GLA_HEADS, GLA_HEAD_K, GLA_HEAD_V), lambda b: (b, 0, 0, 0))
    return pl.pallas_call(
        _gla_par_kernel,
        grid=(batch // nb,),
        in_specs=[blk3(c, GLA_PROJ_PAD), sspec] + [_full(p.shape) for p in params],
        out_specs=[blk3(c, GLA_VAL), sspec],
        out_shape=[jax.ShapeDtypeStruct((batch, c, GLA_VAL), _mixer_o_dtype(c)),
                   jax.ShapeDtypeStruct(s_gla.shape, F32)],
        compiler_params=pltpu.CompilerParams(dimension_semantics=("arbitrary",),
                                             vmem_limit_bytes=VMEM_LIMIT_BYTES),
        name=name,
    )(proj, s_gla, *params)


def _split_cols(w):
    main = w.shape[1] // LANES * LANES
    tail = jnp.pad(w[:, main:].astype(BF16), ((0, 0), (0, LANES - (w.shape[1] - main))))
    return w[:, :main].astype(BF16), tail


def kernel(x_prompt, x_sample, state_hgrn, state_ssm, state_conv, state_gla, norm_ffn1, norm_mix, norm_ffn2, norm_final, ffn1_w_in, ffn1_w_out, ffn2_w_in, ffn2_w_out, ab_w_in, ab_w_out, hgrn_lb_logits, hgrn_norm, ssm_conv_w, ssm_conv_b, ssm_dt_bias, ssm_a_log, ssm_d, ssm_norm, gla_w_in, gla_w_gk, gla_b_gk, gla_norm, gla_w_out):
    assert ab_w_in.shape[0] == 1 and gla_w_in.shape[0] == 1, "one HGRN2/SSD layer and one GLA layer"
    bp, lp, _ = x_prompt.shape
    bs, ls, _ = x_sample.shape
    rp, rs = bp * lp, bs * ls
    cp, cs = math.gcd(lp, CHUNK), math.gcd(ls, CHUNK)
    assert rs % cp == 0
    row = lambda v: v.reshape(1, -1)
    stack_row = lambda v: v.reshape(v.shape[0], 1, v.shape[1])
    nf1, nm, nf2 = stack_row(norm_ffn1), stack_row(norm_mix), stack_row(norm_ffn2)
    w1_in, w1_out = ffn1_w_in.astype(BF16), ffn1_w_out.astype(BF16)
    w2_in, w2_out = ffn2_w_in.astype(BF16), ffn2_w_out.astype(BF16)
    xp = x_prompt.reshape(rp, D_MODEL)
    xs = x_sample.reshape(rs, D_MODEL)

    wp, wt = _split_cols(ab_w_in[0])
    x1, proj = _pre_call(xp, xs, 0, nf1, w1_in, w1_out, nm, wp, wt, name="pre0")
    ab_params = (hgrn_lb_logits, row(hgrn_norm[0]), ssm_conv_w[0], row(ssm_conv_b[0]), row(ssm_dt_bias[0]),
                 row(ssm_a_log[0]), row(ssm_d[0]), row(ssm_norm[0]))
    o_p, hgrn_p, ssm_p, conv_p = _ab_seq_call(proj.reshape((rp + rs) // cp, cp, AB_PROJ_PAD), bp, lp, ab_params,
                                              name="mix0_prompt")
    o_s, hgrn_s, ssm_s, conv_s = _ab_par_call(proj[rp:].reshape(bs, cs, AB_PROJ_PAD), state_hgrn[0], state_ssm[0],
                                              state_conv[0], ab_params, name="mix0_sample")
    (x3,) = _post_call(x1, o_p.reshape(rp, AB_WIDTH), o_s.reshape(rs, AB_WIDTH), 0, ab_w_out[0].astype(BF16), nf2,
                       w2_in, w2_out, row(norm_final), split_output=False, name="post0")

    wp, wt = _split_cols(gla_w_in[0])
    x4, proj = _pre_call(x3, None, 1, nf1, w1_in, w1_out, nm, wp, wt, name="pre1")
    gla_params = (gla_w_gk[0].astype(BF16), row(gla_b_gk[0]), row(gla_norm[0]))
    o_p, gla_p = _gla_seq_call(proj.reshape((rp + rs) // cp, cp, GLA_PROJ_PAD), bp, lp, gla_params, name="mix1_prompt")
    o_s, gla_s = _gla_par_call(proj[rp:].reshape(bs, cs, GLA_PROJ_PAD), state_gla[0], gla_params, name="mix1_sample")
    y_p, y_s = _post_call(x4, o_p.reshape(rp, GLA_VAL), o_s.reshape(rs, GLA_VAL), 1, gla_w_out[0].astype(BF16), nf2,
                          w2_in, w2_out, row(norm_final), split_output=True, name="post1")

    ssm_shape = (1, -1, SSM_HEADS, SSM_HEAD_DIM, SSM_STATE)
    return (y_p.reshape(bp, lp, D_MODEL), y_s.reshape(bs, ls, D_MODEL), hgrn_p[None], hgrn_s[None],
            ssm_p.reshape(ssm_shape), ssm_s.reshape(ssm_shape), conv_p[None], conv_s[None], gla_p[None], gla_s[None])
```

```python
import functools
import math

import jax
import jax.numpy as jnp
from jax import lax
from jax.experimental import pallas as pl
from jax.experimental.pallas import tpu as pltpu

F32 = jnp.float32
BF16 = jnp.bfloat16

D_MODEL = 1024
D_FF = 2816
EPS = 1e-6
CHUNK = 64

HGRN_HEADS = 4
HGRN_HEAD_DIM = 128
HGRN_WIDTH = HGRN_HEADS * HGRN_HEAD_DIM

SSM_HEADS = 8
SSM_HEAD_DIM = 64
SSM_INNER = SSM_HEADS * SSM_HEAD_DIM
SSM_GROUPS = 2
SSM_STATE = 128
SSM_GROUP_WIDTH = SSM_INNER // SSM_GROUPS
HEADS_PER_GROUP = SSM_HEADS // SSM_GROUPS
CONV_W = 4
CONV_DIM = SSM_INNER + 2 * SSM_GROUPS * SSM_STATE
AB_PROJ = 4 * HGRN_WIDTH + SSM_INNER + CONV_DIM + SSM_HEADS
AB_WIDTH = HGRN_WIDTH + SSM_INNER

GLA_HEADS = 4
GLA_HEAD_K = 128
GLA_HEAD_V = 256
GLA_KEY = GLA_HEADS * GLA_HEAD_K
GLA_VAL = GLA_HEADS * GLA_HEAD_V
GK_RANK = 16
GK_NORMALIZER = 16.0
GLA_PROJ = 2 * GLA_KEY + 2 * GLA_VAL + GK_RANK

LANES = 128
SUBLANES = 8
VMEM_LIMIT_BYTES = 56 * 1024 * 1024

AB_PROJ_PAD = -(-AB_PROJ // LANES) * LANES
GLA_PROJ_PAD = -(-GLA_PROJ // LANES) * LANES

_Q0, _F0, _I0, _G0 = 0, HGRN_WIDTH, 2 * HGRN_WIDTH, 3 * HGRN_WIDTH
_Z0 = 4 * HGRN_WIDTH
_XBC0 = _Z0 + SSM_INNER
_DT0 = _XBC0 + CONV_DIM


def _rms(x, w):
    return x * lax.rsqrt(jnp.mean(x * x, axis=-1, keepdims=True) + EPS) * w


def _silu(x):
    return x * jax.nn.sigmoid(x)


def _softplus(x):
    return jnp.maximum(x, 0.0) + jnp.log1p(jnp.exp(-jnp.abs(x)))


def _dot(a, b):
    return jnp.dot(a, b, preferred_element_type=F32)


def _bdot(a, b, ca, cb):
    return lax.dot_general(a, b, (((ca,), (cb,)), ((0,), (0,))), preferred_element_type=F32)


def _split3(x):
    hi = x.astype(BF16)
    r1 = x - hi.astype(F32)
    mid = r1.astype(BF16)
    lo = (r1 - mid.astype(F32)).astype(BF16)
    return hi, mid, lo


def _exact_bdot_lhs01(m01, x, ca, cb):
    return sum(_bdot(m01, p, ca, cb) for p in _split3(x))


def _exact_bdot_rhs01(x, m01, ca, cb):
    return sum(_bdot(p, m01, ca, cb) for p in _split3(x))


def _causal(nb, c):
    r = lax.broadcasted_iota(jnp.int32, (nb, c, c), 1)
    col = lax.broadcasted_iota(jnp.int32, (nb, c, c), 2)
    return r >= col


BF16_ROWS = 2 * SUBLANES


def _triangle3(nb, c, wide_axis):
    shape = (1, c, 3 * c) if wide_axis == 2 else (1, 3 * c, c)
    wide = lax.broadcasted_iota(jnp.int32, shape, wide_axis)
    narrow = lax.broadcasted_iota(jnp.int32, shape, 3 - wide_axis)
    hit = None
    for k in range(3):
        wk = wide - k * c
        term = (wk >= 0) & (wk < c) & (wk <= narrow)
        hit = term if hit is None else hit | term
    return jnp.broadcast_to(hit.astype(BF16), (nb,) + shape[1:])


def _chunk_cumsum(x, causal):
    nb, c, _ = x.shape
    if c % BF16_ROWS:
        return _exact_bdot_lhs01(causal.astype(BF16), x, 2, 1)
    return _bdot(_triangle3(nb, c, 2), jnp.concatenate(_split3(x), axis=1), 2, 1)


def _chunk_cumsum_t(x):
    nb, c, _ = x.shape
    if c % BF16_ROWS:
        return _exact_bdot_rhs01(x, _upper(nb, c), 1, 1)
    return _bdot(jnp.concatenate(_split3(x), axis=1), _triangle3(nb, c, 1), 1, 1)


def _col_bcast(row, lanes=LANES):
    nb, _, k = row.shape
    hi, mid, lo = (p.astype(F32) for p in _split3(row))
    r = lax.broadcasted_iota(jnp.int32, (1, BF16_ROWS, k), 1)
    stacked = jnp.where(r == 0, hi, jnp.where(r == 1, mid, jnp.where(r == 2, lo, 0.0))).astype(BF16)
    return _bdot(stacked, jnp.ones((nb, BF16_ROWS, lanes), BF16), 1, 1)


MXU_DIM = 256
FF_TILES = ((0, 6 * MXU_DIM), (6 * MXU_DIM, D_FF))
assert all(lo % MXU_DIM == 0 and hi % MXU_DIM == 0 for lo, hi in FF_TILES)


def _ffn_half(x, norm_w, w_in_ref, w_out_ref):
    hb = _rms(x, norm_w).astype(BF16)
    acc = None
    for lo, hi in FF_TILES:
        gate = _dot(hb, w_in_ref[:, lo:hi])
        up = _dot(hb, w_in_ref[:, D_FF + lo:D_FF + hi])
        act = (_silu(gate) * up).astype(BF16)
        part = _dot(act, w_out_ref[lo:hi, :])
        acc = part if acc is None else acc + part
    return 0.5 * acc


def _pick_group(first_steps, a, b):
    return jnp.where(pl.program_id(0) < first_steps, a, b)


def _cast_rows(src_ref, *dst_refs):
    if len(dst_refs) == 1:
        dst_refs[0][...] = src_ref[...].astype(BF16)
        return
    main_ref, tail_ref = dst_refs
    main = main_ref.shape[-1]
    rest = src_ref.shape[-1] - main
    main_ref[...] = src_ref[:, :main].astype(BF16)
    tail_ref[...] = jnp.zeros(tail_ref.shape, BF16)
    tail_ref[:, :rest] = src_ref[:, main:].astype(BF16)


def _pre_kernel(*refs, first_steps, cast_arity):
    n_x = 1 if first_steps is None else 2
    x_refs, refs = refs[:n_x], refs[n_x:]
    nf_ref, w_in_ref, w_out_ref, nm_ref, wp_ref, wt_ref = refs[:6]
    cast_src = refs[6:6 + len(cast_arity)]
    x1_ref, proj_ref = refs[6 + len(cast_arity):8 + len(cast_arity)]
    cast_dst = list(refs[8 + len(cast_arity):])
    if first_steps is None:
        x = x_refs[0][...]
    else:
        x = _pick_group(first_steps, x_refs[0][...], x_refs[1][...])
    x1 = x + _ffn_half(x, nf_ref[...], w_in_ref, w_out_ref)
    x1_ref[...] = x1
    hb = _rms(x1, nm_ref[...]).astype(BF16)
    main = wp_ref.shape[1]
    proj_ref[:, :main] = _dot(hb, wp_ref[...])
    proj_ref[:, main:] = _dot(hb, wt_ref[...])
    for src_ref, arity in zip(cast_src, cast_arity):
        _cast_rows(src_ref, *cast_dst[:arity])
        cast_dst = cast_dst[arity:]


def _post_kernel(x_ref, oa_ref, ob_ref, wo_ref, nf_ref, w_in_ref, w_out_ref, nfin_ref, *y_refs, first_steps):
    o = _pick_group(first_steps, oa_ref[...].astype(BF16), ob_ref[...].astype(BF16))
    x2 = x_ref[...] + _dot(o, wo_ref[...])
    y = x2 + _ffn_half(x2, nf_ref[...], w_in_ref, w_out_ref)
    if len(y_refs) == 1:
        y_refs[0][...] = y
    else:
        y = _rms(y, nfin_ref[...])
        ya_ref, yb_ref = y_refs

        @pl.when(pl.program_id(0) < first_steps)
        def _():
            ya_ref[...] = y

        @pl.when(pl.program_id(0) >= first_steps)
        def _():
            yb_ref[...] = y


def _resident(shape, layer=None):
    if layer is None:
        return pl.BlockSpec(shape, lambda *_: (0,) * len(shape), pipeline_mode=pl.Buffered(1))
    return pl.BlockSpec((None,) + tuple(shape[1:]), lambda *_: (layer,) + (0,) * (len(shape) - 1),
                        pipeline_mode=pl.Buffered(1))


def _row_tile(rows, want):
    t = min(rows, want)
    assert rows % t == 0
    return t


PRE_ROWS = 256
POST_ROWS = 512


def _group_specs(tm, first_steps, width):
    first = pl.BlockSpec((tm, width), lambda i: (jnp.minimum(i, first_steps - 1), 0))
    second = pl.BlockSpec((tm, width), lambda i: (jnp.maximum(i - first_steps, 0), 0))
    return first, second


def _weight_spec(w, layer):
    return _resident(w.shape, layer if w.ndim == 3 else None)


class _Cast:
    def __init__(self, src, rows_per_step, layer=None, split=False):
        self.src, self.rps, self.layer, self.split = src, rows_per_step, layer, split
        rows = src.shape[-2]
        assert rows % rows_per_step == 0 and rows_per_step % (2 * SUBLANES) == 0
        self.steps = rows // rows_per_step
        assert not split or layer is not None or src.ndim == 2

    def _index(self, lead):
        last = self.steps - 1
        return lambda i: lead + (jnp.minimum(i, last), 0)

    def in_spec(self):
        cols = self.src.shape[-1]
        if self.src.ndim == 2:
            return pl.BlockSpec((self.rps, cols), self._index(()))
        if self.layer is None:
            return pl.BlockSpec((self.src.shape[0], self.rps, cols), self._index((0,)))
        return pl.BlockSpec((None, self.rps, cols), self._index((self.layer,)))

    def outs(self):
        rows, cols = self.src.shape[-2:]
        if self.src.ndim == 3 and self.layer is None:
            n = self.src.shape[0]
            return [(jax.ShapeDtypeStruct((n, rows, cols), BF16), pl.BlockSpec((n, self.rps, cols), self._index((0,))))]
        widths = [cols // LANES * LANES, LANES] if self.split else [cols]
        return [(jax.ShapeDtypeStruct((rows, w), BF16), pl.BlockSpec((self.rps, w), self._index(()))) for w in widths]


def _pre_call(xa, xb, layer, nf, w_in, w_out, nm, wp, wt, casts=(), *, name):
    ra = xa.shape[0]
    rb = 0 if xb is None else xb.shape[0]
    rows = ra + rb
    tm = _row_tile(rb if rb else ra, PRE_ROWS)
    assert ra % tm == 0
    assert all(cast.steps <= rows // tm for cast in casts)
    npad = wp.shape[1] + wt.shape[1]
    tok = lambda n: pl.BlockSpec((tm, n), lambda i: (i, 0))
    if xb is None:
        first_steps, x_specs, xs = None, [tok(D_MODEL)], (xa,)
    else:
        first_steps = ra // tm
        x_specs, xs = list(_group_specs(tm, first_steps, D_MODEL)), (xa, xb)
    cast_outs = [cast.outs() for cast in casts]
    flat_outs = [o for outs in cast_outs for o in outs]
    return pl.pallas_call(
        functools.partial(_pre_kernel, first_steps=first_steps, cast_arity=tuple(len(o) for o in cast_outs)),
        grid=(rows // tm,),
        in_specs=x_specs + [_resident(nf.shape, layer), _weight_spec(w_in, layer), _weight_spec(w_out, layer),
                            _resident(nm.shape, layer), _resident(wp.shape), _resident(wt.shape)]
                         + [cast.in_spec() for cast in casts],
        out_specs=[tok(D_MODEL), tok(npad)] + [spec for _, spec in flat_outs],
        out_shape=[jax.ShapeDtypeStruct((rows, D_MODEL), F32), jax.ShapeDtypeStruct((rows, npad), F32)]
                  + [shape for shape, _ in flat_outs],
        compiler_params=pltpu.CompilerParams(dimension_semantics=("arbitrary",), vmem_limit_bytes=VMEM_LIMIT_BYTES),
        name=name,
    )(*xs, nf, w_in, w_out, nm, wp, wt, *[cast.src for cast in casts])


def _post_call(x, oa, ob, layer, wo, nf, w_in, w_out, nfin, *, split_output, name):
    rows = x.shape[0]
    ra, rb = oa.shape[0], ob.shape[0]
    assert ra + rb == rows
    tm = _row_tile(rb, POST_ROWS)
    assert ra % tm == 0
    first_steps = ra // tm
    tok = lambda n: pl.BlockSpec((tm, n), lambda i: (i, 0))
    spec_a, spec_b = _group_specs(tm, first_steps, oa.shape[1])
    if split_output:
        out_specs = list(_group_specs(tm, first_steps, D_MODEL))
        out_shape = [jax.ShapeDtypeStruct((ra, D_MODEL), F32), jax.ShapeDtypeStruct((rb, D_MODEL), F32)]
    else:
        out_specs = [tok(D_MODEL)]
        out_shape = [jax.ShapeDtypeStruct((rows, D_MODEL), F32)]
    return pl.pallas_call(
        functools.partial(_post_kernel, first_steps=first_steps),
        grid=(rows // tm,),
        in_specs=[tok(D_MODEL), spec_a, spec_b, _resident(wo.shape), _resident(nf.shape, layer),
                  _resident(w_in.shape, layer), _resident(w_out.shape, layer), _resident(nfin.shape)],
        out_specs=out_specs,
        out_shape=out_shape,
        compiler_params=pltpu.CompilerParams(dimension_semantics=("arbitrary",), vmem_limit_bytes=VMEM_LIMIT_BYTES),
        name=name,
    )(x, oa, ob, wo, nf, w_in, w_out, nfin)


def _gla_heads(q, k, v, log_f, gate, norm_w, n_heads, dk, dv, causal, sequential, read_state, write_state,
               o_ref, o_col0):
    nb, c, _ = q.shape
    g = _chunk_cumsum(log_f, causal)
    g_last = g[:, c - 1:c, :]
    q_dec = (q * jnp.exp(g)).astype(BF16)
    k_inv = (k * jnp.exp(-g)).astype(BF16)
    k_end = (k * jnp.exp(g_last - g)).astype(BF16)
    vb = v.astype(BF16)
    heads = range(n_heads)
    ks = [slice(h * dk, (h + 1) * dk) for h in heads]
    vs = [slice(h * dv, (h + 1) * dv) for h in heads]
    scores = [_bdot(q_dec[:, :, ks[h]], k_inv[:, :, ks[h]], 2, 2) for h in heads]
    scores = [jnp.where(causal, sc, 0.0).astype(BF16) for sc in scores]
    o_intra = [_bdot(scores[h], vb[:, :, vs[h]], 2, 1) for h in heads]

    def emit(h, b, o):
        cols = slice(o_col0 + h * dv, o_col0 + (h + 1) * dv)
        o_ref[b, :, cols] = (_rms(o, norm_w) * _silu(gate[b, :, vs[h]])).astype(o_ref.dtype)

    if sequential:
        kv_t = [_bdot(vb[:, :, vs[h]], k_end[:, :, ks[h]], 1, 1) for h in heads]
        decay = jnp.exp(g_last)
        for b in range(nb):
            for h in heads:
                s_t = read_state(h)
                o_inter = lax.dot_general(q_dec[b, :, ks[h]], s_t.astype(BF16), (((1,), (1,)), ((), ())),
                                          preferred_element_type=F32)
                write_state(h, decay[b, :, ks[h]] * s_t + kv_t[h][b])
                emit(h, b, o_intra[h][b] + o_inter)
    else:
        for h in heads:
            kv = _bdot(k_end[:, :, ks[h]], vb[:, :, vs[h]], 1, 1)
            decay = jnp.exp(_col_bcast(g_last[:, :, ks[h]]))
            decay = jnp.concatenate([decay] * (dv // LANES), axis=-1)
            s0 = read_state(h)
            o_inter = _bdot(q_dec[:, :, ks[h]], s0.astype(BF16), 2, 1)
            write_state(h, decay * s0 + kv)
            emit(h, slice(None), o_intra[h] + o_inter)


def _expand_heads(x, expand):
    return _exact_bdot_rhs01(x, expand, 2, 1)


def _ssd_heads(xs, bs, cs, z, dt, a_row, d_row, norm_w, causal, sequential, read_state, write_state,
               o_ref, o_col0):
    nb, c, _ = xs.shape
    hrow = lax.broadcasted_iota(jnp.int32, (SSM_HEADS, SSM_INNER), 0)
    hcol = lax.broadcasted_iota(jnp.int32, (SSM_HEADS, SSM_INNER), 1) // SSM_HEAD_DIM
    expand2d = (hrow == hcol).astype(BF16)
    expand = jnp.broadcast_to(expand2d[None], (nb, SSM_HEADS, SSM_INNER))
    d_x = sum(_dot(p, expand2d) for p in _split3(d_row))
    dta = dt * a_row
    cum = _chunk_cumsum(dta, causal)
    cum_t = _chunk_cumsum_t(dta)
    dt_x = _expand_heads(dt, expand)
    cum_x = _expand_heads(cum, expand)
    xdt = xs * dt_x
    cum_last = cum_x[:, c - 1:c, :]
    x_end = (xdt * jnp.exp(cum_last - cum_x)).astype(BF16)
    chunk_dec = jnp.exp(cum_x)
    bsb = bs.astype(BF16)
    csb = cs.astype(BF16)
    lane_head = lax.broadcasted_iota(jnp.int32, (nb, c, SSM_GROUP_WIDTH), 2) // SSM_HEAD_DIM
    groups = range(SSM_GROUPS)
    gl = [slice(g * SSM_STATE, (g + 1) * SSM_STATE) for g in groups]
    hl = [slice(g * SSM_GROUP_WIDTH, (g + 1) * SSM_GROUP_WIDTH) for g in groups]
    cb = [_bdot(csb[:, :, gl[g]], bsb[:, :, gl[g]], 2, 2) for g in groups]
    y_intra = []
    for g in groups:
        xdt_g = xdt[:, :, hl[g]]
        y = None
        for r in range(HEADS_PER_GROUP):
            h = g * HEADS_PER_GROUP + r
            col = jnp.broadcast_to(cum[:, :, h:h + 1], (nb, c, c))
            row = cum_t[:, h:h + 1, :]
            dec = jnp.where(causal, jnp.exp(col - row), 0.0)
            lmat = (cb[g] * dec).astype(BF16)
            xm = jnp.where(lane_head == r, xdt_g, 0.0).astype(BF16)
            part = _bdot(lmat, xm, 2, 1)
            y = part if y is None else y + part
        y_intra.append(y)

    def finish(g, y_in, y_inter, b):
        y_all = y_in + y_inter * chunk_dec[b, :, hl[g]] + d_x[:, hl[g]] * xs[b, :, hl[g]]
        y_all = _rms(y_all * _silu(z[b, :, hl[g]]), norm_w[:, hl[g]])
        cols = slice(o_col0 + g * SSM_GROUP_WIDTH, o_col0 + (g + 1) * SSM_GROUP_WIDTH)
        o_ref[b, :, cols] = y_all.astype(o_ref.dtype)

    if sequential:
        kv_t = [_bdot(bsb[:, :, gl[g]], x_end[:, :, hl[g]], 1, 1) for g in groups]
        for b in range(nb):
            for g in groups:
                s_t = read_state(g)
                y_inter = _dot(csb[b, :, gl[g]], s_t.astype(BF16))
                write_state(g, chunk_dec[b, c - 1:c, hl[g]] * s_t + kv_t[g][b])
                finish(g, y_intra[g][b], y_inter, b)
    else:
        for g in groups:
            kv = _bdot(x_end[:, :, hl[g]], bsb[:, :, gl[g]], 1, 1)
            decay = jnp.exp(_col_bcast(cum_last[:, :, hl[g]], SSM_STATE))
            s0 = read_state(g)
            y_inter = _bdot(csb[:, :, gl[g]], s0.astype(BF16), 2, 2)
            write_state(g, decay * s0 + kv)
            finish(g, y_intra[g], y_inter, slice(None))


def _upper(nb, c):
    r = lax.broadcasted_iota(jnp.int32, (nb, c, c), 1)
    col = lax.broadcasted_iota(jnp.int32, (nb, c, c), 2)
    return (r <= col).astype(BF16)


def _lower_bound(lb_logits):
    m = jnp.max(lb_logits, axis=0, keepdims=True)
    e = jnp.exp(lb_logits - m)
    return e[0:1, :] / jnp.sum(e, axis=0, keepdims=True)


def _ab_math(proj_ref, conv, lb_ref, hn_ref, dtb_ref, alog_ref, d_ref, sn_ref, sequential,
             read_h, write_h, read_s, write_s, o_ref):
    nb, c, _ = o_ref.shape
    causal = _causal(nb, c)
    lb = _lower_bound(lb_ref[...])
    f = lb + (1.0 - lb) * jax.nn.sigmoid(proj_ref[:, :, _F0:_F0 + HGRN_WIDTH])
    _gla_heads(_silu(proj_ref[:, :, _Q0:_Q0 + HGRN_WIDTH]), 1.0 - f, proj_ref[:, :, _I0:_I0 + HGRN_WIDTH],
               jnp.log(f), proj_ref[:, :, _G0:_G0 + HGRN_WIDTH], hn_ref[...], HGRN_HEADS, HGRN_HEAD_DIM,
               HGRN_HEAD_DIM, causal, sequential, read_h, write_h, o_ref, 0)
    act = _silu(conv)
    dt = _softplus(proj_ref[:, :, _DT0:_DT0 + SSM_HEADS] + dtb_ref[...])
    _ssd_heads(act[:, :, :SSM_INNER], act[:, :, SSM_INNER:SSM_INNER + SSM_GROUPS * SSM_STATE],
               act[:, :, SSM_INNER + SSM_GROUPS * SSM_STATE:], proj_ref[:, :, _Z0:_Z0 + SSM_INNER], dt,
               -jnp.exp(alog_ref[...]), d_ref[...], sn_ref[...], causal, sequential, read_s, write_s,
               o_ref, HGRN_WIDTH)


def _ab_seq_kernel(proj_ref, lb_ref, hn_ref, cw_ref, cbias_ref, dtb_ref, alog_ref, d_ref, sn_ref,
                   o_ref, sh_out, ss_out, sc_out, sh, ss, xpad):
    t = pl.program_id(1)
    nb, c, _ = o_ref.shape
    rows = nb * c

    @pl.when(t == 0)
    def _():
        sh[...] = jnp.zeros_like(sh)
        ss[...] = jnp.zeros_like(ss)
        xpad[0:SUBLANES, :] = jnp.zeros((SUBLANES, CONV_DIM), F32)

    xpad[SUBLANES:SUBLANES + rows, :] = proj_ref[:, :, _XBC0:_XBC0 + CONV_DIM].reshape(rows, CONV_DIM)
    padded = xpad[...]
    conv = cbias_ref[...] + padded[SUBLANES:] * cw_ref[CONV_W - 1:CONV_W, :]
    for d in range(1, CONV_W):
        conv = conv + pltpu.roll(padded, d, 0)[SUBLANES:] * cw_ref[CONV_W - 1 - d:CONV_W - d, :]
    xpad[0:SUBLANES, :] = padded[rows:rows + SUBLANES]
    conv = conv.reshape(nb, c, CONV_DIM)

    def read_h(h):
        return sh[h]

    def write_h(h, s):
        sh[h] = s

    def read_s(g):
        return ss[g]

    def write_s(g, s):
        ss[g] = s

    _ab_math(proj_ref, conv, lb_ref, hn_ref, dtb_ref, alog_ref, d_ref, sn_ref, True,
             read_h, write_h, read_s, write_s, o_ref)

    @pl.when(t == pl.num_programs(1) - 1)
    def _():
        for h in range(HGRN_HEADS):
            sh_out[0, h] = sh[h].T
        for g in range(SSM_GROUPS):
            ss_out[0, g] = ss[g].T
        sc_out[0] = xpad[SUBLANES - (CONV_W - 1):SUBLANES, :]


def _ab_par_kernel(proj_ref, sh_in, ss_in, sc_in, lb_ref, hn_ref, cw_ref, cbias_ref, dtb_ref, alog_ref,
                   d_ref, sn_ref, o_ref, sh_out, ss_out, sc_out, xpad):
    nb, c, _ = o_ref.shape
    nbuf = CONV_W - 1
    xpad[:, 0:nbuf, :] = sc_in[...]
    xpad[:, nbuf:nbuf + c, :] = proj_ref[:, :, _XBC0:_XBC0 + CONV_DIM]
    conv = cbias_ref[...]
    for k in range(CONV_W):
        conv = conv + xpad[:, k:k + c, :] * cw_ref[k:k + 1, :]
    sc_out[...] = xpad[:, c:c + nbuf, :]

    def read_h(h):
        return sh_in[:, h]

    def write_h(h, s):
        sh_out[:, h] = s

    def read_s(g):
        return ss_in[:, g]

    def write_s(g, s):
        ss_out[:, g] = s

    _ab_math(proj_ref, conv, lb_ref, hn_ref, dtb_ref, alog_ref, d_ref, sn_ref, False,
             read_h, write_h, read_s, write_s, o_ref)


def _gla_math(proj_ref, wgk_ref, bgk_ref, gn_ref, sequential, read_g, write_g, o_ref):
    nb, c, _ = o_ref.shape
    causal = _causal(nb, c)
    q = proj_ref[:, :, 0:GLA_KEY] * (GLA_HEAD_K ** -0.5)
    k = proj_ref[:, :, GLA_KEY:2 * GLA_KEY]
    v = proj_ref[:, :, 2 * GLA_KEY:2 * GLA_KEY + GLA_VAL]
    gate = proj_ref[:, :, 2 * GLA_KEY + GLA_VAL:2 * GLA_KEY + 2 * GLA_VAL]
    gk_low = proj_ref[:, :, 2 * GLA_KEY + 2 * GLA_VAL:2 * GLA_KEY + 2 * GLA_VAL + GK_RANK].astype(BF16)
    if c % SUBLANES == 0:
        gk = _dot(gk_low.reshape(nb * c, GK_RANK), wgk_ref[...]).reshape(nb, c, GLA_KEY)
    else:
        gk = _bdot(gk_low, jnp.broadcast_to(wgk_ref[...][None], (nb, GK_RANK, GLA_KEY)), 2, 1)
    gk = gk + bgk_ref[...]
    log_f = -_softplus(-gk) / GK_NORMALIZER
    _gla_heads(q, k, v, log_f, gate, gn_ref[...], GLA_HEADS, GLA_HEAD_K, GLA_HEAD_V, causal, sequential,
               read_g, write_g, o_ref, 0)


def _gla_seq_kernel(proj_ref, wgk_ref, bgk_ref, gn_ref, o_ref, sg_out, sg):
    t = pl.program_id(1)

    @pl.when(t == 0)
    def _():
        sg[...] = jnp.zeros_like(sg)

    def read_g(h):
        return sg[h]

    def write_g(h, s):
        sg[h] = s

    _gla_math(proj_ref, wgk_ref, bgk_ref, gn_ref, True, read_g, write_g, o_ref)

    @pl.when(t == pl.num_programs(1) - 1)
    def _():
        for h in range(GLA_HEADS):
            sg_out[0, h] = sg[h].T


def _gla_par_kernel(proj_ref, sg_in, wgk_ref, bgk_ref, gn_ref, o_ref, sg_out):
    def read_g(h):
        return sg_in[:, h]

    def write_g(h, s):
        sg_out[:, h] = s

    _gla_math(proj_ref, wgk_ref, bgk_ref, gn_ref, False, read_g, write_g, o_ref)


SEQ_TILE_CHUNKS = 8
PAR_TILE_SEQS = 8


def _full(shape):
    return pl.BlockSpec(shape, lambda *_: (0,) * len(shape))


def _mixer_o_dtype(c):
    return BF16 if c % (2 * SUBLANES) == 0 else F32


def _ab_seq_call(proj, batch, length, params, *, name):
    c = math.gcd(length, CHUNK)
    nb = SEQ_TILE_CHUNKS
    tiles = length // (c * nb)
    assert tiles * c * nb == length
    width = HEADS_PER_GROUP * SSM_HEAD_DIM
    blk = lambda n: pl.BlockSpec((nb, c, n), lambda b, t: (b * tiles + t, 0, 0))
    out_shapes = [
        jax.ShapeDtypeStruct((batch * length // c, c, AB_WIDTH), _mixer_o_dtype(c)),
        jax.ShapeDtypeStruct((batch, HGRN_HEADS, HGRN_HEAD_DIM, HGRN_HEAD_DIM), F32),
        jax.ShapeDtypeStruct((batch, SSM_GROUPS, width, SSM_STATE), F32),
        jax.ShapeDtypeStruct((batch, CONV_W - 1, CONV_DIM), F32),
    ]
    out_specs = [
        blk(AB_WIDTH),
        pl.BlockSpec((1, HGRN_HEADS, HGRN_HEAD_DIM, HGRN_HEAD_DIM), lambda b, t: (b, 0, 0, 0)),
        pl.BlockSpec((1, SSM_GROUPS, width, SSM_STATE), lambda b, t: (b, 0, 0, 0)),
        pl.BlockSpec((1, CONV_W - 1, CONV_DIM), lambda b, t: (b, 0, 0)),
    ]
    return pl.pallas_call(
        _ab_seq_kernel,
        grid=(batch, tiles),
        in_specs=[blk(AB_PROJ_PAD)] + [_full(p.shape) for p in params],
        out_specs=out_specs,
        out_shape=out_shapes,
        scratch_shapes=[
            pltpu.VMEM((HGRN_HEADS, HGRN_HEAD_DIM, HGRN_HEAD_DIM), F32),
            pltpu.VMEM((SSM_GROUPS, SSM_STATE, width), F32),
            pltpu.VMEM((nb * c + SUBLANES, CONV_DIM), F32),
        ],
        compiler_params=pltpu.CompilerParams(dimension_semantics=("arbitrary", "arbitrary"),
                                             vmem_limit_bytes=VMEM_LIMIT_BYTES),
        name=name,
    )(proj, *params)


def _ab_par_call(proj, s_hgrn, s_ssm, s_conv, params, *, name):
    batch, c, _ = proj.shape
    nb = _row_tile(batch, PAR_TILE_SEQS)
    width = HEADS_PER_GROUP * SSM_HEAD_DIM
    s_ssm = s_ssm.reshape(batch, SSM_GROUPS, width, SSM_STATE)
    blk3 = lambda a, n: pl.BlockSpec((nb, a, n), lambda b: (b, 0, 0))
    blk4 = lambda a, r, n: pl.BlockSpec((nb, a, r, n), lambda b: (b, 0, 0, 0))
    state_specs = [blk4(HGRN_HEADS, HGRN_HEAD_DIM, HGRN_HEAD_DIM), blk4(SSM_GROUPS, width, SSM_STATE),
                   blk3(CONV_W - 1, CONV_DIM)]
    out_shapes = [
        jax.ShapeDtypeStruct((batch, c, AB_WIDTH), _mixer_o_dtype(c)),
        jax.ShapeDtypeStruct(s_hgrn.shape, F32),
        jax.ShapeDtypeStruct(s_ssm.shape, F32),
        jax.ShapeDtypeStruct(s_conv.shape, F32),
    ]
    return pl.pallas_call(
        _ab_par_kernel,
        grid=(batch // nb,),
        in_specs=[blk3(c, AB_PROJ_PAD)] + state_specs + [_full(p.shape) for p in params],
        out_specs=[blk3(c, AB_WIDTH)] + state_specs,
        out_shape=out_shapes,
        scratch_shapes=[pltpu.VMEM((nb, c + CONV_W - 1, CONV_DIM), F32)],
        compiler_params=pltpu.CompilerParams(dimension_semantics=("arbitrary",),
                                             vmem_limit_bytes=VMEM_LIMIT_BYTES),
        name=name,
    )(proj, s_hgrn, s_ssm, s_conv, *params)


def _gla_seq_call(proj, batch, length, params, *, name):
    c = math.gcd(length, CHUNK)
    nb = SEQ_TILE_CHUNKS
    tiles = length // (c * nb)
    assert tiles * c * nb == length
    blk = lambda n: pl.BlockSpec((nb, c, n), lambda b, t: (b * tiles + t, 0, 0))
    return pl.pallas_call(
        _gla_seq_kernel,
        grid=(batch, tiles),
        in_specs=[blk(GLA_PROJ_PAD)] + [_full(p.shape) for p in params],
        out_specs=[blk(GLA_VAL),
                   pl.BlockSpec((1, GLA_HEADS, GLA_HEAD_K, GLA_HEAD_V), lambda b, t: (b, 0, 0, 0))],
        out_shape=[jax.ShapeDtypeStruct((batch * length // c, c, GLA_VAL), _mixer_o_dtype(c)),
                   jax.ShapeDtypeStruct((batch, GLA_HEADS, GLA_HEAD_K, GLA_HEAD_V), F32)],
        scratch_shapes=[pltpu.VMEM((GLA_HEADS, GLA_HEAD_V, GLA_HEAD_K), F32)],
        compiler_params=pltpu.CompilerParams(dimension_semantics=("arbitrary", "arbitrary"),
                                             vmem_limit_bytes=VMEM_LIMIT_BYTES),
        name=name,
    )(proj, *params)


def _gla_par_call(proj, s_gla, params, *, name):
    batch, c, _ = proj.shape
    nb = _row_tile(batch, PAR_TILE_SEQS)
    blk3 = lambda a, n: pl.BlockSpec((nb, a, n), lambda b: (b, 0, 0))
    sspec = pl.BlockSpec((nb, GLA_HEADS, GLA_HEAD_K, GLA_HEAD_V), lambda b: (b, 0, 0, 0))
    return pl.pallas_call(
        _gla_par_kernel,
        grid=(batch // nb,),
        in_specs=[blk3(c, GLA_PROJ_PAD), sspec] + [_full(p.shape) for p in params],
        out_specs=[blk3(c, GLA_VAL), sspec],
        out_shape=[jax.ShapeDtypeStruct((batch, c, GLA_VAL), _mixer_o_dtype(c)),
                   jax.ShapeDtypeStruct(s_gla.shape, F32)],
        compiler_params=pltpu.CompilerParams(dimension_semantics=("arbitrary",),
                                             vmem_limit_bytes=VMEM_LIMIT_BYTES),
        name=name,
    )(proj, s_gla, *params)


def _split_cols(w):
    main = w.shape[1] // LANES * LANES
    tail = jnp.pad(w[:, main:].astype(BF16), ((0, 0), (0, LANES - (w.shape[1] - main))))
    return w[:, :main].astype(BF16), tail


def kernel(x_prompt, x_sample, state_hgrn, state_ssm, state_conv, state_gla, norm_ffn1, norm_mix, norm_ffn2, norm_final, ffn1_w_in, ffn1_w_out, ffn2_w_in, ffn2_w_out, ab_w_in, ab_w_out, hgrn_lb_logits, hgrn_norm, ssm_conv_w, ssm_conv_b, ssm_dt_bias, ssm_a_log, ssm_d, ssm_norm, gla_w_in, gla_w_gk, gla_b_gk, gla_norm, gla_w_out):
    assert ab_w_in.shape[0] == 1 and gla_w_in.shape[0] == 1, "one HGRN2/SSD layer and one GLA layer"
    bp, lp, _ = x_prompt.shape
    bs, ls, _ = x_sample.shape
    rp, rs = bp * lp, bs * ls
    cp, cs = math.gcd(lp, CHUNK), math.gcd(ls, CHUNK)
    assert rs % cp == 0
    row = lambda v: v.reshape(1, -1)
    stack_row = lambda v: v.reshape(v.shape[0], 1, v.shape[1])
    nf1, nm, nf2 = stack_row(norm_ffn1), stack_row(norm_mix), stack_row(norm_ffn2)
    in_rows, out_rows = D_MODEL // 64, D_FF // 16
    xp = x_prompt.reshape(rp, D_MODEL)
    xs = x_sample.reshape(rs, D_MODEL)

    wp, wt = _split_cols(ab_w_in[0])
    casts = (_Cast(ffn2_w_in, in_rows), _Cast(ffn2_w_out, out_rows), _Cast(ffn1_w_in, in_rows, layer=1),
             _Cast(ffn1_w_out, out_rows, layer=1), _Cast(ab_w_out, in_rows, layer=0),
             _Cast(gla_w_in, in_rows, layer=0, split=True), _Cast(gla_w_out, in_rows, layer=0))
    (x1, proj, w2_in, w2_out, w1_in, w1_out, ab_wo, gla_wp, gla_wt, gla_wo) = _pre_call(
        xp, xs, 0, nf1, ffn1_w_in[0].astype(BF16), ffn1_w_out[0].astype(BF16), nm, wp, wt, casts, name="pre0")
    ab_params = (hgrn_lb_logits, row(hgrn_norm[0]), ssm_conv_w[0], row(ssm_conv_b[0]), row(ssm_dt_bias[0]),
                 row(ssm_a_log[0]), row(ssm_d[0]), row(ssm_norm[0]))
    o_p, hgrn_p, ssm_p, conv_p = _ab_seq_call(proj.reshape((rp + rs) // cp, cp, AB_PROJ_PAD), bp, lp, ab_params,
                                              name="mix0_prompt")
    o_s, hgrn_s, ssm_s, conv_s = _ab_par_call(proj[rp:].reshape(bs, cs, AB_PROJ_PAD), state_hgrn[0], state_ssm[0],
                                              state_conv[0], ab_params, name="mix0_sample")
    (x3,) = _post_call(x1, o_p.reshape(rp, AB_WIDTH), o_s.reshape(rs, AB_WIDTH), 0, ab_wo, nf2,
                       w2_in, w2_out, row(norm_final), split_output=False, name="post0")

    x4, proj = _pre_call(x3, None, 1, nf1, w1_in, w1_out, nm, gla_wp, gla_wt, name="pre1")
    gla_params = (gla_w_gk[0].astype(BF16), row(gla_b_gk[0]), row(gla_norm[0]))
    o_p, gla_p = _gla_seq_call(proj.reshape((rp + rs) // cp, cp, GLA_PROJ_PAD), bp, lp, gla_params, name="mix1_prompt")
    o_s, gla_s = _gla_par_call(proj[rp:].reshape(bs, cs, GLA_PROJ_PAD), state_gla[0], gla_params, name="mix1_sample")
    y_p, y_s = _post_call(x4, o_p.reshape(rp, GLA_VAL), o_s.reshape(rs, GLA_VAL), 1, gla_wo, nf2,
                          w2_in, w2_out, row(norm_final), split_output=True, name="post1")

    ssm_shape = (1, -1, SSM_HEADS, SSM_HEAD_DIM, SSM_STATE)
    return (y_p.reshape(bp, lp, D_MODEL), y_s.reshape(bs, ls, D_MODEL), hgrn_p[None], hgrn_s[None],
            ssm_p.reshape(ssm_shape), ssm_s.reshape(ssm_shape), conv_p[None], conv_s[None], gla_p[None], gla_s[None])
```

```python
import functools
import math

import jax
import jax.numpy as jnp
from jax import lax
from jax.experimental import pallas as pl
from jax.experimental.pallas import tpu as pltpu

F32 = jnp.float32
BF16 = jnp.bfloat16

D_MODEL = 1024
D_FF = 2816
EPS = 1e-6
CHUNK = 64

HGRN_HEADS = 4
HGRN_HEAD_DIM = 128
HGRN_WIDTH = HGRN_HEADS * HGRN_HEAD_DIM

SSM_HEADS = 8
SSM_HEAD_DIM = 64
SSM_INNER = SSM_HEADS * SSM_HEAD_DIM
SSM_GROUPS = 2
SSM_STATE = 128
SSM_GROUP_WIDTH = SSM_INNER // SSM_GROUPS
HEADS_PER_GROUP = SSM_HEADS // SSM_GROUPS
CONV_W = 4
CONV_DIM = SSM_INNER + 2 * SSM_GROUPS * SSM_STATE
AB_PROJ = 4 * HGRN_WIDTH + SSM_INNER + CONV_DIM + SSM_HEADS
AB_WIDTH = HGRN_WIDTH + SSM_INNER

GLA_HEADS = 4
GLA_HEAD_K = 128
GLA_HEAD_V = 256
GLA_KEY = GLA_HEADS * GLA_HEAD_K
GLA_VAL = GLA_HEADS * GLA_HEAD_V
GK_RANK = 16
GK_NORMALIZER = 16.0
GLA_PROJ = 2 * GLA_KEY + 2 * GLA_VAL + GK_RANK

LANES = 128
SUBLANES = 8
VMEM_LIMIT_BYTES = 56 * 1024 * 1024

AB_PROJ_PAD = -(-AB_PROJ // LANES) * LANES
GLA_PROJ_PAD = -(-GLA_PROJ // LANES) * LANES

_Q0, _F0, _I0, _G0 = 0, HGRN_WIDTH, 2 * HGRN_WIDTH, 3 * HGRN_WIDTH
_Z0 = 4 * HGRN_WIDTH
_XBC0 = _Z0 + SSM_INNER
_DT0 = _XBC0 + CONV_DIM


def _rms(x, w):
    return x * lax.rsqrt(jnp.mean(x * x, axis=-1, keepdims=True) + EPS) * w


def _silu(x):
    return x * jax.nn.sigmoid(x)


def _softplus(x):
    return jnp.maximum(x, 0.0) + jnp.log1p(jnp.exp(-jnp.abs(x)))


def _dot(a, b):
    return jnp.dot(a, b, preferred_element_type=F32)


def _bdot(a, b, ca, cb):
    return lax.dot_general(a, b, (((ca,), (cb,)), ((0,), (0,))), preferred_element_type=F32)


def _split3(x):
    hi = x.astype(BF16)
    r1 = x - hi.astype(F32)
    mid = r1.astype(BF16)
    lo = (r1 - mid.astype(F32)).astype(BF16)
    return hi, mid, lo


def _exact_bdot_lhs01(m01, x, ca, cb):
    return sum(_bdot(m01, p, ca, cb) for p in _split3(x))


def _exact_bdot_rhs01(x, m01, ca, cb):
    return sum(_bdot(p, m01, ca, cb) for p in _split3(x))


def _causal(nb, c):
    r = lax.broadcasted_iota(jnp.int32, (nb, c, c), 1)
    col = lax.broadcasted_iota(jnp.int32, (nb, c, c), 2)
    return r >= col


BF16_ROWS = 2 * SUBLANES


def _triangle3(nb, c, wide_axis):
    shape = (1, c, 3 * c) if wide_axis == 2 else (1, 3 * c, c)
    wide = lax.broadcasted_iota(jnp.int32, shape, wide_axis)
    narrow = lax.broadcasted_iota(jnp.int32, shape, 3 - wide_axis)
    hit = None
    for k in range(3):
        wk = wide - k * c
        term = (wk >= 0) & (wk < c) & (wk <= narrow)
        hit = term if hit is None else hit | term
    return jnp.broadcast_to(hit.astype(BF16), (nb,) + shape[1:])


def _chunk_cumsum(x, causal):
    nb, c, _ = x.shape
    if c % BF16_ROWS:
        return _exact_bdot_lhs01(causal.astype(BF16), x, 2, 1)
    return _bdot(_triangle3(nb, c, 2), jnp.concatenate(_split3(x), axis=1), 2, 1)


def _chunk_cumsum_t(x):
    nb, c, _ = x.shape
    if c % BF16_ROWS:
        return _exact_bdot_rhs01(x, _upper(nb, c), 1, 1)
    return _bdot(jnp.concatenate(_split3(x), axis=1), _triangle3(nb, c, 1), 1, 1)


def _col_bcast(row, lanes=LANES):
    nb, _, k = row.shape
    hi, mid, lo = (p.astype(F32) for p in _split3(row))
    r = lax.broadcasted_iota(jnp.int32, (1, BF16_ROWS, k), 1)
    stacked = jnp.where(r == 0, hi, jnp.where(r == 1, mid, jnp.where(r == 2, lo, 0.0))).astype(BF16)
    return _bdot(stacked, jnp.ones((nb, BF16_ROWS, lanes), BF16), 1, 1)


MXU_DIM = 256
FF_TILES = ((0, 6 * MXU_DIM), (6 * MXU_DIM, D_FF))
assert all(lo % MXU_DIM == 0 and hi % MXU_DIM == 0 for lo, hi in FF_TILES)


def _ffn_half(x, norm_w, w_in_ref, w_out_ref):
    hb = _rms(x, norm_w).astype(BF16)
    acc = None
    for lo, hi in FF_TILES:
        gate = _dot(hb, w_in_ref[:, lo:hi])
        up = _dot(hb, w_in_ref[:, D_FF + lo:D_FF + hi])
        act = (_silu(gate) * up).astype(BF16)
        part = _dot(act, w_out_ref[lo:hi, :])
        acc = part if acc is None else acc + part
    return 0.5 * acc


def _pick_group(first_steps, a, b):
    return jnp.where(pl.program_id(0) < first_steps, a, b)


def _cast_rows(src_ref, *dst_refs):
    if len(dst_refs) == 1:
        dst_refs[0][...] = src_ref[...].astype(BF16)
        return
    main_ref, tail_ref = dst_refs
    main = main_ref.shape[-1]
    rest = src_ref.shape[-1] - main
    main_ref[...] = src_ref[:, :main].astype(BF16)
    tail_ref[...] = jnp.zeros(tail_ref.shape, BF16)
    tail_ref[:, :rest] = src_ref[:, main:].astype(BF16)


def _pre_kernel(*refs, first_steps, cast_arity):
    n_x = 1 if first_steps is None else 2
    x_refs, refs = refs[:n_x], refs[n_x:]
    nf_ref, w_in_ref, w_out_ref, nm_ref, wp_ref, wt_ref = refs[:6]
    cast_src = refs[6:6 + len(cast_arity)]
    x1_ref, proj_ref = refs[6 + len(cast_arity):8 + len(cast_arity)]
    cast_dst = list(refs[8 + len(cast_arity):])
    if first_steps is None:
        x = x_refs[0][...]
    else:
        x = _pick_group(first_steps, x_refs[0][...], x_refs[1][...])
    x1 = x + _ffn_half(x, nf_ref[...], w_in_ref, w_out_ref)
    x1_ref[...] = x1
    hb = _rms(x1, nm_ref[...]).astype(BF16)
    main = wp_ref.shape[1]
    proj_ref[:, :main] = _dot(hb, wp_ref[...])
    proj_ref[:, main:] = _dot(hb, wt_ref[...])
    for src_ref, arity in zip(cast_src, cast_arity):
        _cast_rows(src_ref, *cast_dst[:arity])
        cast_dst = cast_dst[arity:]


def _post_kernel(x_ref, oa_ref, ob_ref, wo_ref, nf_ref, w_in_ref, w_out_ref, nfin_ref, *y_refs, first_steps):
    o = _pick_group(first_steps, oa_ref[...].astype(BF16), ob_ref[...].astype(BF16))
    x2 = x_ref[...] + _dot(o, wo_ref[...])
    y = x2 + _ffn_half(x2, nf_ref[...], w_in_ref, w_out_ref)
    if len(y_refs) == 1:
        y_refs[0][...] = y
    else:
        y = _rms(y, nfin_ref[...])
        ya_ref, yb_ref = y_refs

        @pl.when(pl.program_id(0) < first_steps)
        def _():
            ya_ref[...] = y

        @pl.when(pl.program_id(0) >= first_steps)
        def _():
            yb_ref[...] = y


def _resident(shape, layer=None):
    if layer is None:
        return pl.BlockSpec(shape, lambda *_: (0,) * len(shape), pipeline_mode=pl.Buffered(1))
    return pl.BlockSpec((None,) + tuple(shape[1:]), lambda *_: (layer,) + (0,) * (len(shape) - 1),
                        pipeline_mode=pl.Buffered(1))


def _row_tile(rows, want):
    t = min(rows, want)
    assert rows % t == 0
    return t


PRE0_ROWS = 256
PRE1_ROWS = 512
POST_ROWS = 512


def _group_specs(tm, first_steps, width):
    first = pl.BlockSpec((tm, width), lambda i: (jnp.minimum(i, first_steps - 1), 0))
    second = pl.BlockSpec((tm, width), lambda i: (jnp.maximum(i - first_steps, 0), 0))
    return first, second


def _weight_spec(w, layer):
    return _resident(w.shape, layer if w.ndim == 3 else None)


class _Cast:
    def __init__(self, src, rows_per_step, layer=None, split=False):
        self.src, self.rps, self.layer, self.split = src, rows_per_step, layer, split
        rows = src.shape[-2]
        assert rows % rows_per_step == 0 and rows_per_step % (2 * SUBLANES) == 0
        self.steps = rows // rows_per_step
        assert not split or layer is not None or src.ndim == 2

    def _index(self, lead):
        last = self.steps - 1
        return lambda i: lead + (jnp.minimum(i, last), 0)

    def in_spec(self):
        cols = self.src.shape[-1]
        if self.src.ndim == 2:
            return pl.BlockSpec((self.rps, cols), self._index(()))
        if self.layer is None:
            return pl.BlockSpec((self.src.shape[0], self.rps, cols), self._index((0,)))
        return pl.BlockSpec((None, self.rps, cols), self._index((self.layer,)))

    def outs(self):
        rows, cols = self.src.shape[-2:]
        if self.src.ndim == 3 and self.layer is None:
            n = self.src.shape[0]
            return [(jax.ShapeDtypeStruct((n, rows, cols), BF16), pl.BlockSpec((n, self.rps, cols), self._index((0,))))]
        widths = [cols // LANES * LANES, LANES] if self.split else [cols]
        return [(jax.ShapeDtypeStruct((rows, w), BF16), pl.BlockSpec((self.rps, w), self._index(()))) for w in widths]


def _pre_call(xa, xb, layer, nf, w_in, w_out, nm, wp, wt, casts=(), *, row_tile, name):
    ra = xa.shape[0]
    rb = 0 if xb is None else xb.shape[0]
    rows = ra + rb
    tm = _row_tile(rb if rb else ra, row_tile)
    assert ra % tm == 0
    assert all(cast.steps <= rows // tm for cast in casts)
    npad = wp.shape[1] + wt.shape[1]
    tok = lambda n: pl.BlockSpec((tm, n), lambda i: (i, 0))
    if xb is None:
        first_steps, x_specs, xs = None, [tok(D_MODEL)], (xa,)
    else:
        first_steps = ra // tm
        x_specs, xs = list(_group_specs(tm, first_steps, D_MODEL)), (xa, xb)
    cast_outs = [cast.outs() for cast in casts]
    flat_outs = [o for outs in cast_outs for o in outs]
    return pl.pallas_call(
        functools.partial(_pre_kernel, first_steps=first_steps, cast_arity=tuple(len(o) for o in cast_outs)),
        grid=(rows // tm,),
        in_specs=x_specs + [_resident(nf.shape, layer), _weight_spec(w_in, layer), _weight_spec(w_out, layer),
                            _resident(nm.shape, layer), _resident(wp.shape), _resident(wt.shape)]
                         + [cast.in_spec() for cast in casts],
        out_specs=[tok(D_MODEL), tok(npad)] + [spec for _, spec in flat_outs],
        out_shape=[jax.ShapeDtypeStruct((rows, D_MODEL), F32), jax.ShapeDtypeStruct((rows, npad), F32)]
                  + [shape for shape, _ in flat_outs],
        compiler_params=pltpu.CompilerParams(dimension_semantics=("arbitrary",), vmem_limit_bytes=VMEM_LIMIT_BYTES),
        name=name,
    )(*xs, nf, w_in, w_out, nm, wp, wt, *[cast.src for cast in casts])


def _post_call(x, oa, ob, layer, wo, nf, w_in, w_out, nfin, *, split_output, name):
    rows = x.shape[0]
    ra, rb = oa.shape[0], ob.shape[0]
    assert ra + rb == rows
    tm = _row_tile(rb, POST_ROWS)
    assert ra % tm == 0
    first_steps = ra // tm
    tok = lambda n: pl.BlockSpec((tm, n), lambda i: (i, 0))
    spec_a, spec_b = _group_specs(tm, first_steps, oa.shape[1])
    if split_output:
        out_specs = list(_group_specs(tm, first_steps, D_MODEL))
        out_shape = [jax.ShapeDtypeStruct((ra, D_MODEL), F32), jax.ShapeDtypeStruct((rb, D_MODEL), F32)]
    else:
        out_specs = [tok(D_MODEL)]
        out_shape = [jax.ShapeDtypeStruct((rows, D_MODEL), F32)]
    return pl.pallas_call(
        functools.partial(_post_kernel, first_steps=first_steps),
        grid=(rows // tm,),
        in_specs=[tok(D_MODEL), spec_a, spec_b, _resident(wo.shape), _resident(nf.shape, layer),
                  _resident(w_in.shape, layer), _resident(w_out.shape, layer), _resident(nfin.shape)],
        out_specs=out_specs,
        out_shape=out_shape,
        compiler_params=pltpu.CompilerParams(dimension_semantics=("arbitrary",), vmem_limit_bytes=VMEM_LIMIT_BYTES),
        name=name,
    )(x, oa, ob, wo, nf, w_in, w_out, nfin)


def _gla_heads(q, k, v, log_f, gate, norm_w, n_heads, dk, dv, causal, sequential, read_state, write_state,
               o_ref, o_col0):
    nb, c, _ = q.shape
    g = _chunk_cumsum(log_f, causal)
    g_last = g[:, c - 1:c, :]
    q_dec = (q * jnp.exp(g)).astype(BF16)
    k_inv = (k * jnp.exp(-g)).astype(BF16)
    k_end = (k * jnp.exp(g_last - g)).astype(BF16)
    vb = v.astype(BF16)
    heads = range(n_heads)
    ks = [slice(h * dk, (h + 1) * dk) for h in heads]
    vs = [slice(h * dv, (h + 1) * dv) for h in heads]
    scores = [_bdot(q_dec[:, :, ks[h]], k_inv[:, :, ks[h]], 2, 2) for h in heads]
    scores = [jnp.where(causal, sc, 0.0).astype(BF16) for sc in scores]
    o_intra = [_bdot(scores[h], vb[:, :, vs[h]], 2, 1) for h in heads]

    def emit(h, b, o):
        cols = slice(o_col0 + h * dv, o_col0 + (h + 1) * dv)
        o_ref[b, :, cols] = (_rms(o, norm_w) * _silu(gate[b, :, vs[h]])).astype(o_ref.dtype)

    if sequential:
        kv_t = [_bdot(vb[:, :, vs[h]], k_end[:, :, ks[h]], 1, 1) for h in heads]
        decay = jnp.exp(g_last)
        for b in range(nb):
            for h in heads:
                s_t = read_state(h)
                o_inter = lax.dot_general(q_dec[b, :, ks[h]], s_t.astype(BF16), (((1,), (1,)), ((), ())),
                                          preferred_element_type=F32)
                write_state(h, decay[b, :, ks[h]] * s_t + kv_t[h][b])
                emit(h, b, o_intra[h][b] + o_inter)
    else:
        for h in heads:
            kv = _bdot(k_end[:, :, ks[h]], vb[:, :, vs[h]], 1, 1)
            decay = jnp.exp(_col_bcast(g_last[:, :, ks[h]]))
            decay = jnp.concatenate([decay] * (dv // LANES), axis=-1)
            s0 = read_state(h)
            o_inter = _bdot(q_dec[:, :, ks[h]], s0.astype(BF16), 2, 1)
            write_state(h, decay * s0 + kv)
            emit(h, slice(None), o_intra[h] + o_inter)


def _expand_heads(x, expand):
    return _exact_bdot_rhs01(x, expand, 2, 1)


def _ssd_heads(xs, bs, cs, z, dt, a_row, d_row, norm_w, causal, sequential, read_state, write_state,
               o_ref, o_col0):
    nb, c, _ = xs.shape
    hrow = lax.broadcasted_iota(jnp.int32, (SSM_HEADS, SSM_INNER), 0)
    hcol = lax.broadcasted_iota(jnp.int32, (SSM_HEADS, SSM_INNER), 1) // SSM_HEAD_DIM
    expand2d = (hrow == hcol).astype(BF16)
    expand = jnp.broadcast_to(expand2d[None], (nb, SSM_HEADS, SSM_INNER))
    d_x = sum(_dot(p, expand2d) for p in _split3(d_row))
    dta = dt * a_row
    cum = _chunk_cumsum(dta, causal)
    cum_t = _chunk_cumsum_t(dta)
    dt_x = _expand_heads(dt, expand)
    cum_x = _expand_heads(cum, expand)
    xdt = xs * dt_x
    cum_last = cum_x[:, c - 1:c, :]
    x_end = (xdt * jnp.exp(cum_last - cum_x)).astype(BF16)
    chunk_dec = jnp.exp(cum_x)
    bsb = bs.astype(BF16)
    csb = cs.astype(BF16)
    lane_head = lax.broadcasted_iota(jnp.int32, (nb, c, SSM_GROUP_WIDTH), 2) // SSM_HEAD_DIM
    groups = range(SSM_GROUPS)
    gl = [slice(g * SSM_STATE, (g + 1) * SSM_STATE) for g in groups]
    hl = [slice(g * SSM_GROUP_WIDTH, (g + 1) * SSM_GROUP_WIDTH) for g in groups]
    cb = [_bdot(csb[:, :, gl[g]], bsb[:, :, gl[g]], 2, 2) for g in groups]
    y_intra = []
    for g in groups:
        xdt_g = xdt[:, :, hl[g]]
        y = None
        for r in range(HEADS_PER_GROUP):
            h = g * HEADS_PER_GROUP + r
            col = jnp.broadcast_to(cum[:, :, h:h + 1], (nb, c, c))
            row = cum_t[:, h:h + 1, :]
            dec = jnp.where(causal, jnp.exp(col - row), 0.0)
            lmat = (cb[g] * dec).astype(BF16)
            xm = jnp.where(lane_head == r, xdt_g, 0.0).astype(BF16)
            part = _bdot(lmat, xm, 2, 1)
            y = part if y is None else y + part
        y_intra.append(y)

    def finish(g, y_in, y_inter, b):
        y_all = y_in + y_inter * chunk_dec[b, :, hl[g]] + d_x[:, hl[g]] * xs[b, :, hl[g]]
        y_all = _rms(y_all * _silu(z[b, :, hl[g]]), norm_w[:, hl[g]])
        cols = slice(o_col0 + g * SSM_GROUP_WIDTH, o_col0 + (g + 1) * SSM_GROUP_WIDTH)
        o_ref[b, :, cols] = y_all.astype(o_ref.dtype)

    if sequential:
        kv_t = [_bdot(bsb[:, :, gl[g]], x_end[:, :, hl[g]], 1, 1) for g in groups]
        for b in range(nb):
            for g in groups:
                s_t = read_state(g)
                y_inter = _dot(csb[b, :, gl[g]], s_t.astype(BF16))
                write_state(g, chunk_dec[b, c - 1:c, hl[g]] * s_t + kv_t[g][b])
                finish(g, y_intra[g][b], y_inter, b)
    else:
        for g in groups:
            kv = _bdot(x_end[:, :, hl[g]], bsb[:, :, gl[g]], 1, 1)
            decay = jnp.exp(_col_bcast(cum_last[:, :, hl[g]], SSM_STATE))
            s0 = read_state(g)
            y_inter = _bdot(csb[:, :, gl[g]], s0.astype(BF16), 2, 2)
            write_state(g, decay * s0 + kv)
            finish(g, y_intra[g], y_inter, slice(None))


def _upper(nb, c):
    r = lax.broadcasted_iota(jnp.int32, (nb, c, c), 1)
    col = lax.broadcasted_iota(jnp.int32, (nb, c, c), 2)
    return (r <= col).astype(BF16)


def _lower_bound(lb_logits):
    m = jnp.max(lb_logits, axis=0, keepdims=True)
    e = jnp.exp(lb_logits - m)
    return e[0:1, :] / jnp.sum(e, axis=0, keepdims=True)


def _ab_math(proj_ref, conv, lb_ref, hn_ref, dtb_ref, alog_ref, d_ref, sn_ref, sequential,
             read_h, write_h, read_s, write_s, o_ref):
    nb, c, _ = o_ref.shape
    causal = _causal(nb, c)
    lb = _lower_bound(lb_ref[...])
    f = lb + (1.0 - lb) * jax.nn.sigmoid(proj_ref[:, :, _F0:_F0 + HGRN_WIDTH])
    _gla_heads(_silu(proj_ref[:, :, _Q0:_Q0 + HGRN_WIDTH]), 1.0 - f, proj_ref[:, :, _I0:_I0 + HGRN_WIDTH],
               jnp.log(f), proj_ref[:, :, _G0:_G0 + HGRN_WIDTH], hn_ref[...], HGRN_HEADS, HGRN_HEAD_DIM,
               HGRN_HEAD_DIM, causal, sequential, read_h, write_h, o_ref, 0)
    act = _silu(conv)
    dt = _softplus(proj_ref[:, :, _DT0:_DT0 + SSM_HEADS] + dtb_ref[...])
    _ssd_heads(act[:, :, :SSM_INNER], act[:, :, SSM_INNER:SSM_INNER + SSM_GROUPS * SSM_STATE],
               act[:, :, SSM_INNER + SSM_GROUPS * SSM_STATE:], proj_ref[:, :, _Z0:_Z0 + SSM_INNER], dt,
               -jnp.exp(alog_ref[...]), d_ref[...], sn_ref[...], causal, sequential, read_s, write_s,
               o_ref, HGRN_WIDTH)


def _ab_seq_kernel(proj_ref, lb_ref, hn_ref, cw_ref, cbias_ref, dtb_ref, alog_ref, d_ref, sn_ref,
                   o_ref, sh_out, ss_out, sc_out, sh, ss, xpad):
    t = pl.program_id(1)
    nb, c, _ = o_ref.shape
    rows = nb * c

    @pl.when(t == 0)
    def _():
        sh[...] = jnp.zeros_like(sh)
        ss[...] = jnp.zeros_like(ss)
        xpad[0:SUBLANES, :] = jnp.zeros((SUBLANES, CONV_DIM), F32)

    xpad[SUBLANES:SUBLANES + rows, :] = proj_ref[:, :, _XBC0:_XBC0 + CONV_DIM].reshape(rows, CONV_DIM)
    padded = xpad[...]
    conv = cbias_ref[...] + padded[SUBLANES:] * cw_ref[CONV_W - 1:CONV_W, :]
    for d in range(1, CONV_W):
        conv = conv + pltpu.roll(padded, d, 0)[SUBLANES:] * cw_ref[CONV_W - 1 - d:CONV_W - d, :]
    xpad[0:SUBLANES, :] = padded[rows:rows + SUBLANES]
    conv = conv.reshape(nb, c, CONV_DIM)

    def read_h(h):
        return sh[h]

    def write_h(h, s):
        sh[h] = s

    def read_s(g):
        return ss[g]

    def write_s(g, s):
        ss[g] = s

    _ab_math(proj_ref, conv, lb_ref, hn_ref, dtb_ref, alog_ref, d_ref, sn_ref, True,
             read_h, write_h, read_s, write_s, o_ref)

    @pl.when(t == pl.num_programs(1) - 1)
    def _():
        for h in range(HGRN_HEADS):
            sh_out[0, h] = sh[h].T
        for g in range(SSM_GROUPS):
            ss_out[0, g] = ss[g].T
        sc_out[0] = xpad[SUBLANES - (CONV_W - 1):SUBLANES, :]


def _ab_par_kernel(proj_ref, sh_in, ss_in, sc_in, lb_ref, hn_ref, cw_ref, cbias_ref, dtb_ref, alog_ref,
                   d_ref, sn_ref, o_ref, sh_out, ss_out, sc_out, xpad, o3):
    nb, c, _ = o3.shape
    nbuf = CONV_W - 1
    proj = proj_ref[...].reshape(nb, c, proj_ref.shape[-1])
    xpad[:, 0:nbuf, :] = sc_in[...]
    xpad[:, nbuf:nbuf + c, :] = proj[:, :, _XBC0:_XBC0 + CONV_DIM]
    conv = cbias_ref[...]
    for k in range(CONV_W):
        conv = conv + xpad[:, k:k + c, :] * cw_ref[k:k + 1, :]
    sc_out[...] = xpad[:, c:c + nbuf, :]

    def read_h(h):
        return sh_in[:, h]

    def write_h(h, s):
        sh_out[:, h] = s

    def read_s(g):
        return ss_in[:, g]

    def write_s(g, s):
        ss_out[:, g] = s

    _ab_math(proj, conv, lb_ref, hn_ref, dtb_ref, alog_ref, d_ref, sn_ref, False,
             read_h, write_h, read_s, write_s, o3)
    o_ref[...] = o3[...].reshape(o_ref.shape)


def _gla_math(proj_ref, wgk_ref, bgk_ref, gn_ref, sequential, read_g, write_g, o_ref):
    nb, c, _ = o_ref.shape
    causal = _causal(nb, c)
    q = proj_ref[:, :, 0:GLA_KEY] * (GLA_HEAD_K ** -0.5)
    k = proj_ref[:, :, GLA_KEY:2 * GLA_KEY]
    v = proj_ref[:, :, 2 * GLA_KEY:2 * GLA_KEY + GLA_VAL]
    gate = proj_ref[:, :, 2 * GLA_KEY + GLA_VAL:2 * GLA_KEY + 2 * GLA_VAL]
    gk_low = proj_ref[:, :, 2 * GLA_KEY + 2 * GLA_VAL:2 * GLA_KEY + 2 * GLA_VAL + GK_RANK].astype(BF16)
    if c % SUBLANES == 0:
        gk = _dot(gk_low.reshape(nb * c, GK_RANK), wgk_ref[...]).reshape(nb, c, GLA_KEY)
    else:
        gk = _bdot(gk_low, jnp.broadcast_to(wgk_ref[...][None], (nb, GK_RANK, GLA_KEY)), 2, 1)
    gk = gk + bgk_ref[...]
    log_f = -_softplus(-gk) / GK_NORMALIZER
    _gla_heads(q, k, v, log_f, gate, gn_ref[...], GLA_HEADS, GLA_HEAD_K, GLA_HEAD_V, causal, sequential,
               read_g, write_g, o_ref, 0)


def _gla_seq_kernel(proj_ref, wgk_ref, bgk_ref, gn_ref, o_ref, sg_out, sg):
    t = pl.program_id(1)

    @pl.when(t == 0)
    def _():
        sg[...] = jnp.zeros_like(sg)

    def read_g(h):
        return sg[h]

    def write_g(h, s):
        sg[h] = s

    _gla_math(proj_ref, wgk_ref, bgk_ref, gn_ref, True, read_g, write_g, o_ref)

    @pl.when(t == pl.num_programs(1) - 1)
    def _():
        for h in range(GLA_HEADS):
            sg_out[0, h] = sg[h].T


def _gla_par_kernel(proj_ref, sg_in, wgk_ref, bgk_ref, gn_ref, o_ref, sg_out, o3):
    def read_g(h):
        return sg_in[:, h]

    def write_g(h, s):
        sg_out[:, h] = s

    nb, c, _ = o3.shape
    proj = proj_ref[...].reshape(nb, c, proj_ref.shape[-1])
    _gla_math(proj, wgk_ref, bgk_ref, gn_ref, False, read_g, write_g, o3)
    o_ref[...] = o3[...].reshape(o_ref.shape)


SEQ_TILE_CHUNKS = 8
GLA_SEQ_TILE_CHUNKS = 16
PAR_TILE_SEQS = 8


def _full(shape):
    return pl.BlockSpec(shape, lambda *_: (0,) * len(shape))


def _mixer_o_dtype(c):
    return BF16 if c % (2 * SUBLANES) == 0 else F32


def _ab_seq_call(proj, batch, length, params, *, name):
    c = math.gcd(length, CHUNK)
    nb = SEQ_TILE_CHUNKS
    tiles = length // (c * nb)
    assert tiles * c * nb == length
    width = HEADS_PER_GROUP * SSM_HEAD_DIM
    blk = lambda n: pl.BlockSpec((nb, c, n), lambda b, t: (b * tiles + t, 0, 0))
    out_shapes = [
        jax.ShapeDtypeStruct((batch * length // c, c, AB_WIDTH), _mixer_o_dtype(c)),
        jax.ShapeDtypeStruct((batch, HGRN_HEADS, HGRN_HEAD_DIM, HGRN_HEAD_DIM), F32),
        jax.ShapeDtypeStruct((batch, SSM_GROUPS, width, SSM_STATE), F32),
        jax.ShapeDtypeStruct((batch, CONV_W - 1, CONV_DIM), F32),
    ]
    out_specs = [
        blk(AB_WIDTH),
        pl.BlockSpec((1, HGRN_HEADS, HGRN_HEAD_DIM, HGRN_HEAD_DIM), lambda b, t: (b, 0, 0, 0)),
        pl.BlockSpec((1, SSM_GROUPS, width, SSM_STATE), lambda b, t: (b, 0, 0, 0)),
        pl.BlockSpec((1, CONV_W - 1, CONV_DIM), lambda b, t: (b, 0, 0)),
    ]
    return pl.pallas_call(
        _ab_seq_kernel,
        grid=(batch, tiles),
        in_specs=[blk(AB_PROJ_PAD)] + [_full(p.shape) for p in params],
        out_specs=out_specs,
        out_shape=out_shapes,
        scratch_shapes=[
            pltpu.VMEM((HGRN_HEADS, HGRN_HEAD_DIM, HGRN_HEAD_DIM), F32),
            pltpu.VMEM((SSM_GROUPS, SSM_STATE, width), F32),
            pltpu.VMEM((nb * c + SUBLANES, CONV_DIM), F32),
        ],
        compiler_params=pltpu.CompilerParams(dimension_semantics=("arbitrary", "arbitrary"),
                                             vmem_limit_bytes=VMEM_LIMIT_BYTES),
        name=name,
    )(proj, *params)


def _ab_par_call(proj, row0, c, s_hgrn, s_ssm, s_conv, params, *, name):
    batch = s_hgrn.shape[0]
    nb = _row_tile(batch, PAR_TILE_SEQS)
    rows = nb * c
    assert row0 % rows == 0
    width = HEADS_PER_GROUP * SSM_HEAD_DIM
    s_ssm = s_ssm.reshape(batch, SSM_GROUPS, width, SSM_STATE)
    blk3 = lambda a, n: pl.BlockSpec((nb, a, n), lambda b: (b, 0, 0))
    blk4 = lambda a, r, n: pl.BlockSpec((nb, a, r, n), lambda b: (b, 0, 0, 0))
    state_specs = [blk4(HGRN_HEADS, HGRN_HEAD_DIM, HGRN_HEAD_DIM), blk4(SSM_GROUPS, width, SSM_STATE),
                   blk3(CONV_W - 1, CONV_DIM)]
    out_shapes = [
        jax.ShapeDtypeStruct((batch * c, AB_WIDTH), F32),
        jax.ShapeDtypeStruct(s_hgrn.shape, F32),
        jax.ShapeDtypeStruct(s_ssm.shape, F32),
        jax.ShapeDtypeStruct(s_conv.shape, F32),
    ]
    return pl.pallas_call(
        _ab_par_kernel,
        grid=(batch // nb,),
        in_specs=[pl.BlockSpec((rows, proj.shape[1]), lambda b: (row0 // rows + b, 0))] + state_specs
                 + [_full(p.shape) for p in params],
        out_specs=[pl.BlockSpec((rows, AB_WIDTH), lambda b: (b, 0))] + state_specs,
        out_shape=out_shapes,
        scratch_shapes=[pltpu.VMEM((nb, c + CONV_W - 1, CONV_DIM), F32), pltpu.VMEM((nb, c, AB_WIDTH), F32)],
        compiler_params=pltpu.CompilerParams(dimension_semantics=("arbitrary",),
                                             vmem_limit_bytes=VMEM_LIMIT_BYTES),
        name=name,
    )(proj, s_hgrn, s_ssm, s_conv, *params)


def _gla_seq_call(proj, batch, length, params, *, name):
    c = math.gcd(length, CHUNK)
    nb = GLA_SEQ_TILE_CHUNKS
    tiles = length // (c * nb)
    assert tiles * c * nb == length
    blk = lambda n: pl.BlockSpec((nb, c, n), lambda b, t: (b * tiles + t, 0, 0))
    return pl.pallas_call(
        _gla_seq_kernel,
        grid=(batch, tiles),
        in_specs=[blk(GLA_PROJ_PAD)] + [_full(p.shape) for p in params],
        out_specs=[blk(GLA_VAL),
                   pl.BlockSpec((1, GLA_HEADS, GLA_HEAD_K, GLA_HEAD_V), lambda b, t: (b, 0, 0, 0))],
        out_shape=[jax.ShapeDtypeStruct((batch * length // c, c, GLA_VAL), _mixer_o_dtype(c)),
                   jax.ShapeDtypeStruct((batch, GLA_HEADS, GLA_HEAD_K, GLA_HEAD_V), F32)],
        scratch_shapes=[pltpu.VMEM((GLA_HEADS, GLA_HEAD_V, GLA_HEAD_K), F32)],
        compiler_params=pltpu.CompilerParams(dimension_semantics=("arbitrary", "arbitrary"),
                                             vmem_limit_bytes=VMEM_LIMIT_BYTES),
        name=name,
    )(proj, *params)


def _gla_par_call(proj, row0, c, s_gla, params, *, name):
    batch = s_gla.shape[0]
    nb = _row_tile(batch, PAR_TILE_SEQS)
    rows = nb * c
    assert row0 % rows == 0
    sspec = pl.BlockSpec((nb, GLA_HEADS, GLA_HEAD_K, GLA_HEAD_V), lambda b: (b, 0, 0, 0))
    return pl.pallas_call(
        _gla_par_kernel,
        grid=(batch // nb,),
        in_specs=[pl.BlockSpec((rows, proj.shape[1]), lambda b: (row0 // rows + b, 0)), sspec]
                 + [_full(p.shape) for p in params],
        out_specs=[pl.BlockSpec((rows, GLA_VAL), lambda b: (b, 0)), sspec],
        out_shape=[jax.ShapeDtypeStruct((batch * c, GLA_VAL), F32), jax.ShapeDtypeStruct(s_gla.shape, F32)],
        scratch_shapes=[pltpu.VMEM((nb, c, GLA_VAL), F32)],
        compiler_params=pltpu.CompilerParams(dimension_semantics=("arbitrary",),
                                             vmem_limit_bytes=VMEM_LIMIT_BYTES),
        name=name,
    )(proj, s_gla, *params)


def _split_cols(w):
    main = w.shape[1] // LANES * LANES
    tail = jnp.pad(w[:, main:].astype(BF16), ((0, 0), (0, LANES - (w.shape[1] - main))))
    return w[:, :main].astype(BF16), tail


def kernel(x_prompt, x_sample, state_hgrn, state_ssm, state_conv, state_gla, norm_ffn1, norm_mix, norm_ffn2, norm_final, ffn1_w_in, ffn1_w_out, ffn2_w_in, ffn2_w_out, ab_w_in, ab_w_out, hgrn_lb_logits, hgrn_norm, ssm_conv_w, ssm_conv_b, ssm_dt_bias, ssm_a_log, ssm_d, ssm_norm, gla_w_in, gla_w_gk, gla_b_gk, gla_norm, gla_w_out):
    assert ab_w_in.shape[0] == 1 and gla_w_in.shape[0] == 1, "one HGRN2/SSD layer and one GLA layer"
    bp, lp, _ = x_prompt.shape
    bs, ls, _ = x_sample.shape
    rp, rs = bp * lp, bs * ls
    cp, cs = math.gcd(lp, CHUNK), math.gcd(ls, CHUNK)
    assert rs % cp == 0
    row = lambda v: v.reshape(1, -1)
    stack_row = lambda v: v.reshape(v.shape[0], 1, v.shape[1])
    nf1, nm, nf2 = stack_row(norm_ffn1), stack_row(norm_mix), stack_row(norm_ffn2)
    in_rows, out_rows = D_MODEL // 64, D_FF // 16
    xp = x_prompt.reshape(rp, D_MODEL)
    xs = x_sample.reshape(rs, D_MODEL)

    wp, wt = _split_cols(ab_w_in[0])
    casts = (_Cast(ffn2_w_in, in_rows), _Cast(ffn2_w_out, out_rows), _Cast(ffn1_w_in, in_rows, layer=1),
             _Cast(ffn1_w_out, out_rows, layer=1), _Cast(ab_w_out, in_rows, layer=0),
             _Cast(gla_w_in, in_rows, layer=0, split=True), _Cast(gla_w_out, in_rows, layer=0))
    (x1, proj, w2_in, w2_out, w1_in, w1_out, ab_wo, gla_wp, gla_wt, gla_wo) = _pre_call(
        xp, xs, 0, nf1, ffn1_w_in[0].astype(BF16), ffn1_w_out[0].astype(BF16), nm, wp, wt, casts, row_tile=PRE0_ROWS,
        name="pre0")
    ab_params = (hgrn_lb_logits, row(hgrn_norm[0]), ssm_conv_w[0], row(ssm_conv_b[0]), row(ssm_dt_bias[0]),
                 row(ssm_a_log[0]), row(ssm_d[0]), row(ssm_norm[0]))
    o_p, hgrn_p, ssm_p, conv_p = _ab_seq_call(proj.reshape((rp + rs) // cp, cp, AB_PROJ_PAD), bp, lp, ab_params,
                                              name="mix0_prompt")
    o_s, hgrn_s, ssm_s, conv_s = _ab_par_call(proj, rp, cs, state_hgrn[0], state_ssm[0], state_conv[0], ab_params,
                                              name="mix0_sample")
    (x3,) = _post_call(x1, o_p.reshape(rp, AB_WIDTH), o_s, 0, ab_wo, nf2,
                       w2_in, w2_out, row(norm_final), split_output=False, name="post0")

    x4, proj = _pre_call(x3, None, 1, nf1, w1_in, w1_out, nm, gla_wp, gla_wt, row_tile=PRE1_ROWS, name="pre1")
    gla_params = (gla_w_gk[0].astype(BF16), row(gla_b_gk[0]), row(gla_norm[0]))
    o_p, gla_p = _gla_seq_call(proj.reshape((rp + rs) // cp, cp, GLA_PROJ_PAD), bp, lp, gla_params, name="mix1_prompt")
    o_s, gla_s = _gla_par_call(proj, rp, cs, state_gla[0], gla_params, name="mix1_sample")
    y_p, y_s = _post_call(x4, o_p.reshape(rp, GLA_VAL), o_s, 1, gla_wo, nf2,
                          w2_in, w2_out, row(norm_final), split_output=True, name="post1")

    ssm_shape = (1, -1, SSM_HEADS, SSM_HEAD_DIM, SSM_STATE)
    return (y_p.reshape(bp, lp, D_MODEL), y_s.reshape(bs, ls, D_MODEL), hgrn_p[None], hgrn_s[None],
            ssm_p.reshape(ssm_shape), ssm_s.reshape(ssm_shape), conv_p[None], conv_s[None], gla_p[None], gla_s[None])
```

```python
import functools
import math

import jax
import jax.numpy as jnp
from jax import lax
from jax.experimental import pallas as pl
from jax.experimental.pallas import tpu as pltpu

F32 = jnp.float32
BF16 = jnp.bfloat16

D_MODEL = 1024
D_FF = 2816
EPS = 1e-6
CHUNK = 64

HGRN_HEADS = 4
HGRN_HEAD_DIM = 128
HGRN_WIDTH = HGRN_HEADS * HGRN_HEAD_DIM

SSM_HEADS = 8
SSM_HEAD_DIM = 64
SSM_INNER = SSM_HEADS * SSM_HEAD_DIM
SSM_GROUPS = 2
SSM_STATE = 128
SSM_GROUP_WIDTH = SSM_INNER // SSM_GROUPS
HEADS_PER_GROUP = SSM_HEADS // SSM_GROUPS
CONV_W = 4
CONV_DIM = SSM_INNER + 2 * SSM_GROUPS * SSM_STATE
AB_PROJ = 4 * HGRN_WIDTH + SSM_INNER + CONV_DIM + SSM_HEADS
AB_WIDTH = HGRN_WIDTH + SSM_INNER

GLA_HEADS = 4
GLA_HEAD_K = 128
GLA_HEAD_V = 256
GLA_KEY = GLA_HEADS * GLA_HEAD_K
GLA_VAL = GLA_HEADS * GLA_HEAD_V
GK_RANK = 16
GK_NORMALIZER = 16.0
GLA_PROJ = 2 * GLA_KEY + 2 * GLA_VAL + GK_RANK

LANES = 128
SUBLANES = 8
VMEM_LIMIT_BYTES = 56 * 1024 * 1024

_Q0, _K0, _LF0, _V0, _G0 = (i * HGRN_WIDTH for i in range(5))
_Z0 = 5 * HGRN_WIDTH
_XBC0 = _Z0 + SSM_INNER
_DT0 = _XBC0 + CONV_DIM
AB_FEAT = _DT0 + LANES
_WQ, _WF, _WI, _WG, _WZ = (i * HGRN_WIDTH for i in range(5))
_WXBC = 4 * HGRN_WIDTH + SSM_INNER

_GQ0, _GK0 = 0, GLA_KEY
_GV0 = 2 * GLA_KEY
_GG0 = _GV0 + GLA_VAL
_GLF0 = _GG0 + GLA_VAL
GLA_FEAT = _GLF0 + GLA_KEY


def _rms(x, w):
    return x * lax.rsqrt(jnp.mean(x * x, axis=-1, keepdims=True) + EPS) * w


def _rms_core(x):
    return x * lax.rsqrt(jnp.mean(x * x, axis=-1, keepdims=True) + EPS)


def _silu(x):
    return x * jax.nn.sigmoid(x)


def _softplus(x):
    return jnp.maximum(x, 0.0) + jnp.log1p(jnp.exp(-jnp.abs(x)))


def _dot(a, b):
    return jnp.dot(a, b, preferred_element_type=F32)


def _bdot(a, b, ca, cb):
    return lax.dot_general(a, b, (((ca,), (cb,)), ((0,), (0,))), preferred_element_type=F32)


def _split3(x):
    hi = x.astype(BF16)
    r1 = x - hi.astype(F32)
    mid = r1.astype(BF16)
    lo = (r1 - mid.astype(F32)).astype(BF16)
    return hi, mid, lo


def _exact_bdot_lhs01(m01, x, ca, cb):
    return sum(_bdot(m01, p, ca, cb) for p in _split3(x))


def _exact_bdot_rhs01(x, m01, ca, cb):
    return sum(_bdot(p, m01, ca, cb) for p in _split3(x))


def _causal(nb, c):
    r = lax.broadcasted_iota(jnp.int32, (nb, c, c), 1)
    col = lax.broadcasted_iota(jnp.int32, (nb, c, c), 2)
    return r >= col


BF16_ROWS = 2 * SUBLANES


def _triangle3(nb, c, wide_axis):
    shape = (1, c, 3 * c) if wide_axis == 2 else (1, 3 * c, c)
    wide = lax.broadcasted_iota(jnp.int32, shape, wide_axis)
    narrow = lax.broadcasted_iota(jnp.int32, shape, 3 - wide_axis)
    hit = None
    for k in range(3):
        wk = wide - k * c
        term = (wk >= 0) & (wk < c) & (wk <= narrow)
        hit = term if hit is None else hit | term
    return jnp.broadcast_to(hit.astype(BF16), (nb,) + shape[1:])


def _chunk_cumsum(x, causal):
    nb, c, _ = x.shape
    if c % BF16_ROWS:
        return _exact_bdot_lhs01(causal.astype(BF16), x, 2, 1)
    return _bdot(_triangle3(nb, c, 2), jnp.concatenate(_split3(x), axis=1), 2, 1)


def _chunk_cumsum_t(x):
    nb, c, _ = x.shape
    if c % BF16_ROWS:
        return _exact_bdot_rhs01(x, _upper(nb, c), 1, 1)
    return _bdot(jnp.concatenate(_split3(x), axis=1), _triangle3(nb, c, 1), 1, 1)


def _col_bcast(row, lanes=LANES):
    nb, _, k = row.shape
    hi, mid, lo = (p.astype(F32) for p in _split3(row))
    r = lax.broadcasted_iota(jnp.int32, (1, BF16_ROWS, k), 1)
    stacked = jnp.where(r == 0, hi, jnp.where(r == 1, mid, jnp.where(r == 2, lo, 0.0))).astype(BF16)
    return _bdot(stacked, jnp.ones((nb, BF16_ROWS, lanes), BF16), 1, 1)


MXU_DIM = 256
FF_TILES = ((0, 6 * MXU_DIM), (6 * MXU_DIM, D_FF))
assert all(lo % MXU_DIM == 0 and hi % MXU_DIM == 0 for lo, hi in FF_TILES)


def _ffn_half(x, norm_w, w_in_ref, w_out_ref):
    hb = _rms(x, norm_w).astype(BF16)
    acc = None
    for lo, hi in FF_TILES:
        gate = _dot(hb, w_in_ref[:, lo:hi])
        up = _dot(hb, w_in_ref[:, D_FF + lo:D_FF + hi])
        act = (_silu(gate) * up).astype(BF16)
        part = _dot(act, w_out_ref[lo:hi, :])
        acc = part if acc is None else acc + part
    return 0.5 * acc


def _pick_group(first_steps, a, b):
    return jnp.where(pl.program_id(0) < first_steps, a, b)


def _cast_rows(src_ref, *dst_refs):
    if len(dst_refs) == 1:
        dst_refs[0][...] = src_ref[...].astype(BF16)
        return
    main_ref, tail_ref = dst_refs
    main = main_ref.shape[-1]
    rest = src_ref.shape[-1] - main
    main_ref[...] = src_ref[:, :main].astype(BF16)
    tail_ref[...] = jnp.zeros(tail_ref.shape, BF16)
    tail_ref[:, :rest] = src_ref[:, main:].astype(BF16)


def _ab_features(hb, wp_ref, wt_ref, lb_ref, hn_ref, dtb_ref, feat_ref):
    def group(w0):
        return _dot(hb, wp_ref[:, w0:w0 + HGRN_WIDTH])

    def put(c0, val):
        feat_ref[:, c0:c0 + val.shape[-1]] = val

    lb = _lower_bound(lb_ref[...])
    f = lb + (1.0 - lb) * jax.nn.sigmoid(group(_WF))
    put(_K0, 1.0 - f)
    put(_LF0, jnp.log(f))
    put(_Q0, _silu(group(_WQ)))
    put(_G0, _silu(group(_WG)) * hn_ref[...])
    put(_Z0, _silu(group(_WZ)))
    put(_DT0, _softplus(_dot(hb, wt_ref[...]) + dtb_ref[...]))
    put(_XBC0, _dot(hb, wp_ref[:, _WXBC:_WXBC + CONV_DIM]))
    put(_V0, group(_WI))


def _gla_features(hb, wp_ref, wt_ref, wgk_ref, bgk_ref, gn_ref, feat_ref):
    def put(c0, val):
        feat_ref[:, c0:c0 + val.shape[-1]] = val

    gk_low = _dot(hb, wt_ref[...])[:, :GK_RANK].astype(BF16)
    gk = _dot(gk_low, wgk_ref[...]) + bgk_ref[...]
    put(_GLF0, -_softplus(-gk) / GK_NORMALIZER)
    put(_GG0, _silu(_dot(hb, wp_ref[:, _GG0:_GG0 + GLA_VAL])) * gn_ref[...])
    put(_GQ0, _dot(hb, wp_ref[:, _GQ0:_GQ0 + GLA_KEY]) * (GLA_HEAD_K ** -0.5))
    put(_GK0, _dot(hb, wp_ref[:, _GK0:_GK0 + GLA_KEY]))
    put(_GV0, _dot(hb, wp_ref[:, _GV0:_GV0 + GLA_VAL]))


_FEATURES = {"ab": _ab_features, "gla": _gla_features}
N_FEATURE_PARAMS = 3


def _pre_kernel(*refs, first_steps, cast_arity, mode):
    n_x = 1 if first_steps is None else 2
    x_refs, refs = refs[:n_x], refs[n_x:]
    nf_ref, w_in_ref, w_out_ref, nm_ref, wp_ref, wt_ref = refs[:6]
    feat_params, refs = refs[6:6 + N_FEATURE_PARAMS], refs[6 + N_FEATURE_PARAMS:]
    cast_src = refs[:len(cast_arity)]
    x1_ref, proj_ref = refs[len(cast_arity):2 + len(cast_arity)]
    cast_dst = list(refs[2 + len(cast_arity):])
    if first_steps is None:
        x = x_refs[0][...]
    else:
        x = _pick_group(first_steps, x_refs[0][...], x_refs[1][...])
    x1 = x + _ffn_half(x, nf_ref[...], w_in_ref, w_out_ref)
    x1_ref[...] = x1
    hb = _rms(x1, nm_ref[...]).astype(BF16)
    _FEATURES[mode](hb, wp_ref, wt_ref, *feat_params, proj_ref)
    for src_ref, arity in zip(cast_src, cast_arity):
        _cast_rows(src_ref, *cast_dst[:arity])
        cast_dst = cast_dst[arity:]


def _post_kernel(x_ref, oa_ref, ob_ref, wo_ref, nf_ref, w_in_ref, w_out_ref, nfin_ref, *y_refs, first_steps):
    o = _pick_group(first_steps, oa_ref[...].astype(BF16), ob_ref[...].astype(BF16))
    x2 = x_ref[...] + _dot(o, wo_ref[...])
    y = x2 + _ffn_half(x2, nf_ref[...], w_in_ref, w_out_ref)
    if len(y_refs) == 1:
        y_refs[0][...] = y
    else:
        y = _rms(y, nfin_ref[...])
        ya_ref, yb_ref = y_refs

        @pl.when(pl.program_id(0) < first_steps)
        def _():
            ya_ref[...] = y

        @pl.when(pl.program_id(0) >= first_steps)
        def _():
            yb_ref[...] = y


def _resident(shape, layer=None):
    if layer is None:
        return pl.BlockSpec(shape, lambda *_: (0,) * len(shape), pipeline_mode=pl.Buffered(1))
    return pl.BlockSpec((None,) + tuple(shape[1:]), lambda *_: (layer,) + (0,) * (len(shape) - 1),
                        pipeline_mode=pl.Buffered(1))


def _row_tile(rows, want):
    t = min(rows, want)
    assert rows % t == 0
    return t


PRE0_ROWS = 256
PRE1_ROWS = 512
POST_ROWS = 512


def _group_specs(tm, first_steps, width):
    first = pl.BlockSpec((tm, width), lambda i: (jnp.minimum(i, first_steps - 1), 0))
    second = pl.BlockSpec((tm, width), lambda i: (jnp.maximum(i - first_steps, 0), 0))
    return first, second


def _weight_spec(w, layer):
    return _resident(w.shape, layer if w.ndim == 3 else None)


class _Cast:
    def __init__(self, src, rows_per_step, layer=None, split=False):
        self.src, self.rps, self.layer, self.split = src, rows_per_step, layer, split
        rows = src.shape[-2]
        assert rows % rows_per_step == 0 and rows_per_step % (2 * SUBLANES) == 0
        self.steps = rows // rows_per_step
        assert not split or layer is not None or src.ndim == 2

    def _index(self, lead):
        last = self.steps - 1
        return lambda i: lead + (jnp.minimum(i, last), 0)

    def in_spec(self):
        cols = self.src.shape[-1]
        if self.src.ndim == 2:
            return pl.BlockSpec((self.rps, cols), self._index(()))
        if self.layer is None:
            return pl.BlockSpec((self.src.shape[0], self.rps, cols), self._index((0,)))
        return pl.BlockSpec((None, self.rps, cols), self._index((self.layer,)))

    def outs(self):
        rows, cols = self.src.shape[-2:]
        if self.src.ndim == 3 and self.layer is None:
            n = self.src.shape[0]
            return [(jax.ShapeDtypeStruct((n, rows, cols), BF16), pl.BlockSpec((n, self.rps, cols), self._index((0,))))]
        widths = [cols // LANES * LANES, LANES] if self.split else [cols]
        return [(jax.ShapeDtypeStruct((rows, w), BF16), pl.BlockSpec((self.rps, w), self._index(()))) for w in widths]


def _pre_call(xa, xb, layer, nf, w_in, w_out, nm, wp, wt, feat_params, casts=(), *, mode, row_tile, name):
    ra = xa.shape[0]
    rb = 0 if xb is None else xb.shape[0]
    rows = ra + rb
    tm = _row_tile(rb if rb else ra, row_tile)
    assert ra % tm == 0
    assert all(cast.steps <= rows // tm for cast in casts)
    assert len(feat_params) == N_FEATURE_PARAMS
    npad = {"ab": AB_FEAT, "gla": GLA_FEAT}[mode]
    tok = lambda n: pl.BlockSpec((tm, n), lambda i: (i, 0))
    if xb is None:
        first_steps, x_specs, xs = None, [tok(D_MODEL)], (xa,)
    else:
        first_steps = ra // tm
        x_specs, xs = list(_group_specs(tm, first_steps, D_MODEL)), (xa, xb)
    cast_outs = [cast.outs() for cast in casts]
    flat_outs = [o for outs in cast_outs for o in outs]
    return pl.pallas_call(
        functools.partial(_pre_kernel, first_steps=first_steps, cast_arity=tuple(len(o) for o in cast_outs),
                          mode=mode),
        grid=(rows // tm,),
        in_specs=x_specs + [_resident(nf.shape, layer), _weight_spec(w_in, layer), _weight_spec(w_out, layer),
                            _resident(nm.shape, layer), _resident(wp.shape), _resident(wt.shape)]
                         + [_resident(p.shape) for p in feat_params] + [cast.in_spec() for cast in casts],
        out_specs=[tok(D_MODEL), tok(npad)] + [spec for _, spec in flat_outs],
        out_shape=[jax.ShapeDtypeStruct((rows, D_MODEL), F32), jax.ShapeDtypeStruct((rows, npad), F32)]
                  + [shape for shape, _ in flat_outs],
        compiler_params=pltpu.CompilerParams(dimension_semantics=("arbitrary",), vmem_limit_bytes=VMEM_LIMIT_BYTES),
        name=name,
    )(*xs, nf, w_in, w_out, nm, wp, wt, *feat_params, *[cast.src for cast in casts])


def _post_call(x, oa, ob, layer, wo, nf, w_in, w_out, nfin, *, split_output, name):
    rows = x.shape[0]
    ra, rb = oa.shape[0], ob.shape[0]
    assert ra + rb == rows
    tm = _row_tile(rb, POST_ROWS)
    assert ra % tm == 0
    first_steps = ra // tm
    tok = lambda n: pl.BlockSpec((tm, n), lambda i: (i, 0))
    spec_a, spec_b = _group_specs(tm, first_steps, oa.shape[1])
    if split_output:
        out_specs = list(_group_specs(tm, first_steps, D_MODEL))
        out_shape = [jax.ShapeDtypeStruct((ra, D_MODEL), F32), jax.ShapeDtypeStruct((rb, D_MODEL), F32)]
    else:
        out_specs = [tok(D_MODEL)]
        out_shape = [jax.ShapeDtypeStruct((rows, D_MODEL), F32)]
    return pl.pallas_call(
        functools.partial(_post_kernel, first_steps=first_steps),
        grid=(rows // tm,),
        in_specs=[tok(D_MODEL), spec_a, spec_b, _resident(wo.shape), _resident(nf.shape, layer),
                  _resident(w_in.shape, layer), _resident(w_out.shape, layer), _resident(nfin.shape)],
        out_specs=out_specs,
        out_shape=out_shape,
        compiler_params=pltpu.CompilerParams(dimension_semantics=("arbitrary",), vmem_limit_bytes=VMEM_LIMIT_BYTES),
        name=name,
    )(x, oa, ob, wo, nf, w_in, w_out, nfin)


def _gla_heads(q, k, v, log_f, gate, n_heads, dk, dv, causal, sequential, read_state, write_state,
               o_ref, o_col0):
    nb, c, _ = q.shape
    g = _chunk_cumsum(log_f, causal)
    g_last = g[:, c - 1:c, :]
    q_dec = (q * jnp.exp(g)).astype(BF16)
    k_inv_f = k * jnp.exp(-g)
    k_inv = k_inv_f.astype(BF16)
    decay = jnp.exp(g_last)
    k_end = (k_inv_f * decay).astype(BF16)
    vb = v.astype(BF16)
    heads = range(n_heads)
    ks = [slice(h * dk, (h + 1) * dk) for h in heads]
    vs = [slice(h * dv, (h + 1) * dv) for h in heads]
    scores = [_bdot(q_dec[:, :, ks[h]], k_inv[:, :, ks[h]], 2, 2) for h in heads]
    scores = [jnp.where(causal, sc, 0.0).astype(BF16) for sc in scores]
    o_intra = [_bdot(scores[h], vb[:, :, vs[h]], 2, 1) for h in heads]

    def emit(h, b, o):
        cols = slice(o_col0 + h * dv, o_col0 + (h + 1) * dv)
        o_ref[b, :, cols] = (_rms_core(o) * gate[b, :, vs[h]]).astype(o_ref.dtype)

    if sequential:
        kv_t = [_bdot(vb[:, :, vs[h]], k_end[:, :, ks[h]], 1, 1) for h in heads]
        for b in range(nb):
            for h in heads:
                s_t = read_state(h)
                o_inter = lax.dot_general(q_dec[b, :, ks[h]], s_t.astype(BF16), (((1,), (1,)), ((), ())),
                                          preferred_element_type=F32)
                write_state(h, decay[b, :, ks[h]] * s_t + kv_t[h][b])
                emit(h, b, o_intra[h][b] + o_inter)
    else:
        for h in heads:
            kv = _bdot(k_end[:, :, ks[h]], vb[:, :, vs[h]], 1, 1)
            decay_col = jnp.exp(_col_bcast(g_last[:, :, ks[h]]))
            decay_col = jnp.concatenate([decay_col] * (dv // LANES), axis=-1)
            s0 = read_state(h)
            o_inter = _bdot(q_dec[:, :, ks[h]], s0.astype(BF16), 2, 1)
            write_state(h, decay_col * s0 + kv)
            emit(h, slice(None), o_intra[h] + o_inter)


def _expand_heads(x, expand):
    return _exact_bdot_rhs01(x, expand, 2, 1)


def _ssd_heads(xs, bs, cs, z, dt, a_row, d_row, norm_w, causal, sequential, read_state, write_state,
               o_ref, o_col0):
    nb, c, _ = xs.shape
    hrow = lax.broadcasted_iota(jnp.int32, (SSM_HEADS, SSM_INNER), 0)
    hcol = lax.broadcasted_iota(jnp.int32, (SSM_HEADS, SSM_INNER), 1) // SSM_HEAD_DIM
    expand2d = (hrow == hcol).astype(BF16)
    expand = jnp.broadcast_to(expand2d[None], (nb, SSM_HEADS, SSM_INNER))
    d_x = sum(_dot(p, expand2d) for p in _split3(d_row))
    dta = dt * a_row
    cum = _chunk_cumsum(dta, causal)
    cum_t = _chunk_cumsum_t(dta)
    dt_x = _expand_heads(dt, expand)
    cum_x = _expand_heads(cum, expand)
    xdt = xs * dt_x
    cum_last = cum_x[:, c - 1:c, :]
    x_end = (xdt * jnp.exp(cum_last - cum_x)).astype(BF16)
    chunk_dec = jnp.exp(cum_x)
    bsb = bs.astype(BF16)
    csb = cs.astype(BF16)
    lane_head = lax.broadcasted_iota(jnp.int32, (nb, c, SSM_GROUP_WIDTH), 2) // SSM_HEAD_DIM
    groups = range(SSM_GROUPS)
    gl = [slice(g * SSM_STATE, (g + 1) * SSM_STATE) for g in groups]
    hl = [slice(g * SSM_GROUP_WIDTH, (g + 1) * SSM_GROUP_WIDTH) for g in groups]
    cb = [_bdot(csb[:, :, gl[g]], bsb[:, :, gl[g]], 2, 2) for g in groups]
    y_intra = []
    for g in groups:
        xdt_g = xdt[:, :, hl[g]]
        y = None
        for r in range(HEADS_PER_GROUP):
            h = g * HEADS_PER_GROUP + r
            col = jnp.broadcast_to(cum[:, :, h:h + 1], (nb, c, c))
            row = cum_t[:, h:h + 1, :]
            dec = jnp.where(causal, jnp.exp(col - row), 0.0)
            lmat = (cb[g] * dec).astype(BF16)
            xm = jnp.where(lane_head == r, xdt_g, 0.0).astype(BF16)
            part = _bdot(lmat, xm, 2, 1)
            y = part if y is None else y + part
        y_intra.append(y)

    def finish(g, y_in, y_inter, b):
        y_all = y_in + y_inter * chunk_dec[b, :, hl[g]] + d_x[:, hl[g]] * xs[b, :, hl[g]]
        y_all = _rms(y_all * z[b, :, hl[g]], norm_w[:, hl[g]])
        cols = slice(o_col0 + g * SSM_GROUP_WIDTH, o_col0 + (g + 1) * SSM_GROUP_WIDTH)
        o_ref[b, :, cols] = y_all.astype(o_ref.dtype)

    if sequential:
        kv_t = [_bdot(bsb[:, :, gl[g]], x_end[:, :, hl[g]], 1, 1) for g in groups]
        for b in range(nb):
            for g in groups:
                s_t = read_state(g)
                y_inter = _dot(csb[b, :, gl[g]], s_t.astype(BF16))
                write_state(g, chunk_dec[b, c - 1:c, hl[g]] * s_t + kv_t[g][b])
                finish(g, y_intra[g][b], y_inter, b)
    else:
        for g in groups:
            kv = _bdot(x_end[:, :, hl[g]], bsb[:, :, gl[g]], 1, 1)
            decay = jnp.exp(_col_bcast(cum_last[:, :, hl[g]], SSM_STATE))
            s0 = read_state(g)
            y_inter = _bdot(csb[:, :, gl[g]], s0.astype(BF16), 2, 2)
            write_state(g, decay * s0 + kv)
            finish(g, y_intra[g], y_inter, slice(None))


def _upper(nb, c):
    r = lax.broadcasted_iota(jnp.int32, (nb, c, c), 1)
    col = lax.broadcasted_iota(jnp.int32, (nb, c, c), 2)
    return (r <= col).astype(BF16)


def _lower_bound(lb_logits):
    m = jnp.max(lb_logits, axis=0, keepdims=True)
    e = jnp.exp(lb_logits - m)
    return e[0:1, :] / jnp.sum(e, axis=0, keepdims=True)


def _ab_math(feat, conv, alog_ref, d_ref, sn_ref, sequential, read_h, write_h, read_s, write_s, o_ref):
    nb, c, _ = o_ref.shape
    causal = _causal(nb, c)
    w = HGRN_WIDTH
    _gla_heads(feat[:, :, _Q0:_Q0 + w], feat[:, :, _K0:_K0 + w], feat[:, :, _V0:_V0 + w], feat[:, :, _LF0:_LF0 + w],
               feat[:, :, _G0:_G0 + w], HGRN_HEADS, HGRN_HEAD_DIM, HGRN_HEAD_DIM, causal, sequential,
               read_h, write_h, o_ref, 0)
    act = _silu(conv)
    _ssd_heads(act[:, :, :SSM_INNER], act[:, :, SSM_INNER:SSM_INNER + SSM_GROUPS * SSM_STATE],
               act[:, :, SSM_INNER + SSM_GROUPS * SSM_STATE:], feat[:, :, _Z0:_Z0 + SSM_INNER],
               feat[:, :, _DT0:_DT0 + SSM_HEADS], -jnp.exp(alog_ref[...]), d_ref[...], sn_ref[...], causal,
               sequential, read_s, write_s, o_ref, HGRN_WIDTH)


def _ab_seq_kernel(proj_ref, cw_ref, cbias_ref, alog_ref, d_ref, sn_ref,
                   o_ref, sh_out, ss_out, sc_out, sh, ss, xpad):
    t = pl.program_id(1)
    nb, c, _ = o_ref.shape
    rows = nb * c

    @pl.when(t == 0)
    def _():
        sh[...] = jnp.zeros_like(sh)
        ss[...] = jnp.zeros_like(ss)
        xpad[0:SUBLANES, :] = jnp.zeros((SUBLANES, CONV_DIM), F32)

    xpad[SUBLANES:SUBLANES + rows, :] = proj_ref[:, :, _XBC0:_XBC0 + CONV_DIM].reshape(rows, CONV_DIM)
    padded = xpad[...]
    conv = cbias_ref[...] + padded[SUBLANES:] * cw_ref[CONV_W - 1:CONV_W, :]
    for d in range(1, CONV_W):
        conv = conv + pltpu.roll(padded, d, 0)[SUBLANES:] * cw_ref[CONV_W - 1 - d:CONV_W - d, :]
    xpad[0:SUBLANES, :] = padded[rows:rows + SUBLANES]
    conv = conv.reshape(nb, c, CONV_DIM)

    def read_h(h):
        return sh[h]

    def write_h(h, s):
        sh[h] = s

    def read_s(g):
        return ss[g]

    def write_s(g, s):
        ss[g] = s

    _ab_math(proj_ref, conv, alog_ref, d_ref, sn_ref, True, read_h, write_h, read_s, write_s, o_ref)

    @pl.when(t == pl.num_programs(1) - 1)
    def _():
        for h in range(HGRN_HEADS):
            sh_out[0, h] = sh[h].T
        for g in range(SSM_GROUPS):
            ss_out[0, g] = ss[g].T
        sc_out[0] = xpad[SUBLANES - (CONV_W - 1):SUBLANES, :]


def _ab_par_kernel(proj_ref, sh_in, ss_in, sc_in, cw_ref, cbias_ref, alog_ref, d_ref, sn_ref,
                   o_ref, sh_out, ss_out, sc_out, xpad, o3):
    nb, c, _ = o3.shape
    nbuf = CONV_W - 1
    proj = proj_ref[...].reshape(nb, c, proj_ref.shape[-1])
    xpad[:, 0:nbuf, :] = sc_in[...]
    xpad[:, nbuf:nbuf + c, :] = proj[:, :, _XBC0:_XBC0 + CONV_DIM]
    conv = cbias_ref[...]
    for k in range(CONV_W):
        conv = conv + xpad[:, k:k + c, :] * cw_ref[k:k + 1, :]
    sc_out[...] = xpad[:, c:c + nbuf, :]

    def read_h(h):
        return sh_in[:, h]

    def write_h(h, s):
        sh_out[:, h] = s

    def read_s(g):
        return ss_in[:, g]

    def write_s(g, s):
        ss_out[:, g] = s

    _ab_math(proj, conv, alog_ref, d_ref, sn_ref, False, read_h, write_h, read_s, write_s, o3)
    o_ref[...] = o3[...].reshape(o_ref.shape)


def _gla_math(feat, sequential, read_g, write_g, o_ref):
    nb, c, _ = o_ref.shape
    _gla_heads(feat[:, :, _GQ0:_GQ0 + GLA_KEY], feat[:, :, _GK0:_GK0 + GLA_KEY], feat[:, :, _GV0:_GV0 + GLA_VAL],
               feat[:, :, _GLF0:_GLF0 + GLA_KEY], feat[:, :, _GG0:_GG0 + GLA_VAL], GLA_HEADS, GLA_HEAD_K,
               GLA_HEAD_V, _causal(nb, c), sequential, read_g, write_g, o_ref, 0)


def _gla_seq_kernel(proj_ref, o_ref, sg_out, sg):
    t = pl.program_id(1)

    @pl.when(t == 0)
    def _():
        sg[...] = jnp.zeros_like(sg)

    def read_g(h):
        return sg[h]

    def write_g(h, s):
        sg[h] = s

    _gla_math(proj_ref, True, read_g, write_g, o_ref)

    @pl.when(t == pl.num_programs(1) - 1)
    def _():
        for h in range(GLA_HEADS):
            sg_out[0, h] = sg[h].T


def _gla_par_kernel(proj_ref, sg_in, o_ref, sg_out, o3):
    def read_g(h):
        return sg_in[:, h]

    def write_g(h, s):
        sg_out[:, h] = s

    nb, c, _ = o3.shape
    proj = proj_ref[...].reshape(nb, c, proj_ref.shape[-1])
    _gla_math(proj, False, read_g, write_g, o3)
    o_ref[...] = o3[...].reshape(o_ref.shape)


SEQ_TILE_CHUNKS = 8
GLA_SEQ_TILE_CHUNKS = 8
PAR_TILE_SEQS = 8


def _full(shape):
    return pl.BlockSpec(shape, lambda *_: (0,) * len(shape))


def _mixer_o_dtype(c):
    return BF16 if c % (2 * SUBLANES) == 0 else F32


def _ab_seq_call(proj, batch, length, params, *, name):
    c = math.gcd(length, CHUNK)
    nb = SEQ_TILE_CHUNKS
    tiles = length // (c * nb)
    assert tiles * c * nb == length
    width = HEADS_PER_GROUP * SSM_HEAD_DIM
    blk = lambda n: pl.BlockSpec((nb, c, n), lambda b, t: (b * tiles + t, 0, 0))
    out_shapes = [
        jax.ShapeDtypeStruct((batch * length // c, c, AB_WIDTH), _mixer_o_dtype(c)),
        jax.ShapeDtypeStruct((batch, HGRN_HEADS, HGRN_HEAD_DIM, HGRN_HEAD_DIM), F32),
        jax.ShapeDtypeStruct((batch, SSM_GROUPS, width, SSM_STATE), F32),
        jax.ShapeDtypeStruct((batch, CONV_W - 1, CONV_DIM), F32),
    ]
    out_specs = [
        blk(AB_WIDTH),
        pl.BlockSpec((1, HGRN_HEADS, HGRN_HEAD_DIM, HGRN_HEAD_DIM), lambda b, t: (b, 0, 0, 0)),
        pl.BlockSpec((1, SSM_GROUPS, width, SSM_STATE), lambda b, t: (b, 0, 0, 0)),
        pl.BlockSpec((1, CONV_W - 1, CONV_DIM), lambda b, t: (b, 0, 0)),
    ]
    return pl.pallas_call(
        _ab_seq_kernel,
        grid=(batch, tiles),
        in_specs=[blk(AB_FEAT)] + [_full(p.shape) for p in params],
        out_specs=out_specs,
        out_shape=out_shapes,
        scratch_shapes=[
            pltpu.VMEM((HGRN_HEADS, HGRN_HEAD_DIM, HGRN_HEAD_DIM), F32),
            pltpu.VMEM((SSM_GROUPS, SSM_STATE, width), F32),
            pltpu.VMEM((nb * c + SUBLANES, CONV_DIM), F32),
        ],
        compiler_params=pltpu.CompilerParams(dimension_semantics=("arbitrary", "arbitrary"),
                                             vmem_limit_bytes=VMEM_LIMIT_BYTES),
        name=name,
    )(proj, *params)


def _ab_par_call(proj, row0, c, s_hgrn, s_ssm, s_conv, params, *, name):
    batch = s_hgrn.shape[0]
    nb = _row_tile(batch, PAR_TILE_SEQS)
    rows = nb * c
    assert row0 % rows == 0
    width = HEADS_PER_GROUP * SSM_HEAD_DIM
    s_ssm = s_ssm.reshape(batch, SSM_GROUPS, width, SSM_STATE)
    blk3 = lambda a, n: pl.BlockSpec((nb, a, n), lambda b: (b, 0, 0))
    blk4 = lambda a, r, n: pl.BlockSpec((nb, a, r, n), lambda b: (b, 0, 0, 0))
    state_specs = [blk4(HGRN_HEADS, HGRN_HEAD_DIM, HGRN_HEAD_DIM), blk4(SSM_GROUPS, width, SSM_STATE),
                   blk3(CONV_W - 1, CONV_DIM)]
    out_shapes = [
        jax.ShapeDtypeStruct((batch * c, AB_WIDTH), F32),
        jax.ShapeDtypeStruct(s_hgrn.shape, F32),
        jax.ShapeDtypeStruct(s_ssm.shape, F32),
        jax.ShapeDtypeStruct(s_conv.shape, F32),
    ]
    return pl.pallas_call(
        _ab_par_kernel,
        grid=(batch // nb,),
        in_specs=[pl.BlockSpec((rows, proj.shape[1]), lambda b: (row0 // rows + b, 0))] + state_specs
                 + [_full(p.shape) for p in params],
        out_specs=[pl.BlockSpec((rows, AB_WIDTH), lambda b: (b, 0))] + state_specs,
        out_shape=out_shapes,
        scratch_shapes=[pltpu.VMEM((nb, c + CONV_W - 1, CONV_DIM), F32), pltpu.VMEM((nb, c, AB_WIDTH), F32)],
        compiler_params=pltpu.CompilerParams(dimension_semantics=("arbitrary",),
                                             vmem_limit_bytes=VMEM_LIMIT_BYTES),
        name=name,
    )(proj, s_hgrn, s_ssm, s_conv, *params)


def _gla_seq_call(proj, batch, length, params, *, name):
    c = math.gcd(length, CHUNK)
    nb = GLA_SEQ_TILE_CHUNKS
    tiles = length // (c * nb)
    assert tiles * c * nb == length
    blk = lambda n: pl.BlockSpec((nb, c, n), lambda b, t: (b * tiles + t, 0, 0))
    return pl.pallas_call(
        _gla_seq_kernel,
        grid=(batch, tiles),
        in_specs=[blk(GLA_FEAT)] + [_full(p.shape) for p in params],
        out_specs=[blk(GLA_VAL),
                   pl.BlockSpec((1, GLA_HEADS, GLA_HEAD_K, GLA_HEAD_V), lambda b, t: (b, 0, 0, 0))],
        out_shape=[jax.ShapeDtypeStruct((batch * length // c, c, GLA_VAL), _mixer_o_dtype(c)),
                   jax.ShapeDtypeStruct((batch, GLA_HEADS, GLA_HEAD_K, GLA_HEAD_V), F32)],
        scratch_shapes=[pltpu.VMEM((GLA_HEADS, GLA_HEAD_V, GLA_HEAD_K), F32)],
        compiler_params=pltpu.CompilerParams(dimension_semantics=("arbitrary", "arbitrary"),
                                             vmem_limit_bytes=VMEM_LIMIT_BYTES),
        name=name,
    )(proj, *params)


def _gla_par_call(proj, row0, c, s_gla, params, *, name):
    batch = s_gla.shape[0]
    nb = _row_tile(batch, PAR_TILE_SEQS)
    rows = nb * c
    assert row0 % rows == 0
    sspec = pl.BlockSpec((nb, GLA_HEADS, GLA_HEAD_K, GLA_HEAD_V), lambda b: (b, 0, 0, 0))
    return pl.pallas_call(
        _gla_par_kernel,
        grid=(batch // nb,),
        in_specs=[pl.BlockSpec((rows, proj.shape[1]), lambda b: (row0 // rows + b, 0)), sspec]
                 + [_full(p.shape) for p in params],
        out_specs=[pl.BlockSpec((rows, GLA_VAL), lambda b: (b, 0)), sspec],
        out_shape=[jax.ShapeDtypeStruct((batch * c, GLA_VAL), F32), jax.ShapeDtypeStruct(s_gla.shape, F32)],
        scratch_shapes=[pltpu.VMEM((nb, c, GLA_VAL), F32)],
        compiler_params=pltpu.CompilerParams(dimension_semantics=("arbitrary",),
                                             vmem_limit_bytes=VMEM_LIMIT_BYTES),
        name=name,
    )(proj, s_gla, *params)


def _split_cols(w):
    main = w.shape[1] // LANES * LANES
    tail = jnp.pad(w[:, main:].astype(BF16), ((0, 0), (0, LANES - (w.shape[1] - main))))
    return w[:, :main].astype(BF16), tail


def kernel(x_prompt, x_sample, state_hgrn, state_ssm, state_conv, state_gla, norm_ffn1, norm_mix, norm_ffn2, norm_final, ffn1_w_in, ffn1_w_out, ffn2_w_in, ffn2_w_out, ab_w_in, ab_w_out, hgrn_lb_logits, hgrn_norm, ssm_conv_w, ssm_conv_b, ssm_dt_bias, ssm_a_log, ssm_d, ssm_norm, gla_w_in, gla_w_gk, gla_b_gk, gla_norm, gla_w_out):
    assert ab_w_in.shape[0] == 1 and gla_w_in.shape[0] == 1, "one HGRN2/SSD layer and one GLA layer"
    bp, lp, _ = x_prompt.shape
    bs, ls, _ = x_sample.shape
    rp, rs = bp * lp, bs * ls
    cp, cs = math.gcd(lp, CHUNK), math.gcd(ls, CHUNK)
    assert rs % cp == 0
    row = lambda v: v.reshape(1, -1)
    stack_row = lambda v: v.reshape(v.shape[0], 1, v.shape[1])
    nf1, nm, nf2 = stack_row(norm_ffn1), stack_row(norm_mix), stack_row(norm_ffn2)
    in_rows, out_rows = D_MODEL // 64, D_FF // 16
    xp = x_prompt.reshape(rp, D_MODEL)
    xs = x_sample.reshape(rs, D_MODEL)

    wp, wt = _split_cols(ab_w_in[0])
    ab_feat_params = (hgrn_lb_logits, jnp.tile(hgrn_norm[0], HGRN_HEADS).reshape(1, -1),
                      jnp.pad(ssm_dt_bias[0], (0, LANES - SSM_HEADS)).reshape(1, -1))
    casts = (_Cast(ffn2_w_in, in_rows), _Cast(ffn2_w_out, out_rows), _Cast(ffn1_w_in, in_rows, layer=1),
             _Cast(ffn1_w_out, out_rows, layer=1), _Cast(ab_w_out, in_rows, layer=0),
             _Cast(gla_w_in[0], in_rows, split=True), _Cast(gla_w_out, in_rows, layer=0))
    (x1, feat, w2_in, w2_out, w1_in, w1_out, ab_wo, gla_wp, gla_wt, gla_wo) = _pre_call(
        xp, xs, 0, nf1, ffn1_w_in[0].astype(BF16), ffn1_w_out[0].astype(BF16), nm, wp, wt, ab_feat_params, casts,
        mode="ab", row_tile=PRE0_ROWS, name="pre0")
    ab_params = (ssm_conv_w[0], row(ssm_conv_b[0]), row(ssm_a_log[0]), row(ssm_d[0]), row(ssm_norm[0]))
    o_p, hgrn_p, ssm_p, conv_p = _ab_seq_call(feat.reshape((rp + rs) // cp, cp, AB_FEAT), bp, lp, ab_params,
                                              name="mix0_prompt")
    o_s, hgrn_s, ssm_s, conv_s = _ab_par_call(feat, rp, cs, state_hgrn[0], state_ssm[0], state_conv[0], ab_params,
                                              name="mix0_sample")
    (x3,) = _post_call(x1, o_p.reshape(rp, AB_WIDTH), o_s, 0, ab_wo, nf2,
                       w2_in, w2_out, row(norm_final), split_output=False, name="post0")

    gla_feat_params = (gla_w_gk[0].astype(BF16), row(gla_b_gk[0]), jnp.tile(gla_norm[0], GLA_HEADS).reshape(1, -1))
    x4, feat = _pre_call(x3, None, 1, nf1, w1_in, w1_out, nm, gla_wp, gla_wt, gla_feat_params, mode="gla",
                         row_tile=PRE1_ROWS, name="pre1")
    o_p, gla_p = _gla_seq_call(feat.reshape((rp + rs) // cp, cp, GLA_FEAT), bp, lp, (), name="mix1_prompt")
    o_s, gla_s = _gla_par_call(feat, rp, cs, state_gla[0], (), name="mix1_sample")
    y_p, y_s = _post_call(x4, o_p.reshape(rp, GLA_VAL), o_s, 1, gla_wo, nf2,
                          w2_in, w2_out, row(norm_final), split_output=True, name="post1")

    ssm_shape = (1, -1, SSM_HEADS, SSM_HEAD_DIM, SSM_STATE)
    return (y_p.reshape(bp, lp, D_MODEL), y_s.reshape(bs, ls, D_MODEL), hgrn_p[None], hgrn_s[None],
            ssm_p.reshape(ssm_shape), ssm_s.reshape(ssm_shape), conv_p[None], conv_s[None], gla_p[None], gla_s[None])
```

```python
import functools
import math

import jax
import jax.numpy as jnp
from jax import lax
from jax.experimental import pallas as pl
from jax.experimental.pallas import tpu as pltpu

F32 = jnp.float32
BF16 = jnp.bfloat16

D_MODEL = 1024
D_FF = 2816
EPS = 1e-6
CHUNK = 64

HGRN_HEADS = 4
HGRN_HEAD_DIM = 128
HGRN_WIDTH = HGRN_HEADS * HGRN_HEAD_DIM

SSM_HEADS = 8
SSM_HEAD_DIM = 64
SSM_INNER = SSM_HEADS * SSM_HEAD_DIM
SSM_GROUPS = 2
SSM_STATE = 128
SSM_GROUP_WIDTH = SSM_INNER // SSM_GROUPS
HEADS_PER_GROUP = SSM_HEADS // SSM_GROUPS
CONV_W = 4
CONV_DIM = SSM_INNER + 2 * SSM_GROUPS * SSM_STATE
AB_PROJ = 4 * HGRN_WIDTH + SSM_INNER + CONV_DIM + SSM_HEADS
AB_WIDTH = HGRN_WIDTH + SSM_INNER

GLA_HEADS = 4
GLA_HEAD_K = 128
GLA_HEAD_V = 256
GLA_KEY = GLA_HEADS * GLA_HEAD_K
GLA_VAL = GLA_HEADS * GLA_HEAD_V
GK_RANK = 16
GK_NORMALIZER = 16.0
GLA_PROJ = 2 * GLA_KEY + 2 * GLA_VAL + GK_RANK

LANES = 128
SUBLANES = 8
VMEM_LIMIT_BYTES = 56 * 1024 * 1024

_Q0, _K0, _LF0, _V0, _G0 = (i * HGRN_WIDTH for i in range(5))
_Z0 = 5 * HGRN_WIDTH
_XBC0 = _Z0 + SSM_INNER
_DT0 = _XBC0 + CONV_DIM
AB_FEAT = _DT0 + LANES
_WQ, _WF, _WI, _WG, _WZ = (i * HGRN_WIDTH for i in range(5))
_WXBC = 4 * HGRN_WIDTH + SSM_INNER

_GQ0, _GK0 = 0, GLA_KEY
_GG0 = 2 * GLA_KEY
_GLF0 = _GG0 + GLA_VAL
GLA_FEAT = _GLF0 + GLA_KEY
_GWQ, _GWK = 0, GLA_KEY
_GWV = 2 * GLA_KEY
_GWG = _GWV + GLA_VAL


def _rms(x, w):
    return x * lax.rsqrt(jnp.mean(x * x, axis=-1, keepdims=True) + EPS) * w


def _rms_core(x):
    return x * lax.rsqrt(jnp.mean(x * x, axis=-1, keepdims=True) + EPS)


def _silu(x):
    return x * jax.nn.sigmoid(x)


def _softplus(x):
    return jnp.maximum(x, 0.0) + jnp.log1p(jnp.exp(-jnp.abs(x)))


def _dot(a, b):
    return jnp.dot(a, b, preferred_element_type=F32)


def _bdot(a, b, ca, cb):
    return lax.dot_general(a, b, (((ca,), (cb,)), ((0,), (0,))), preferred_element_type=F32)


def _split3(x):
    hi = x.astype(BF16)
    r1 = x - hi.astype(F32)
    mid = r1.astype(BF16)
    lo = (r1 - mid.astype(F32)).astype(BF16)
    return hi, mid, lo


def _exact_bdot_lhs01(m01, x, ca, cb):
    return sum(_bdot(m01, p, ca, cb) for p in _split3(x))


def _exact_bdot_rhs01(x, m01, ca, cb):
    return sum(_bdot(p, m01, ca, cb) for p in _split3(x))


def _causal(nb, c):
    r = lax.broadcasted_iota(jnp.int32, (nb, c, c), 1)
    col = lax.broadcasted_iota(jnp.int32, (nb, c, c), 2)
    return r >= col


BF16_ROWS = 2 * SUBLANES


def _triangle3(nb, c, wide_axis):
    shape = (1, c, 3 * c) if wide_axis == 2 else (1, 3 * c, c)
    wide = lax.broadcasted_iota(jnp.int32, shape, wide_axis)
    narrow = lax.broadcasted_iota(jnp.int32, shape, 3 - wide_axis)
    hit = None
    for k in range(3):
        wk = wide - k * c
        term = (wk >= 0) & (wk < c) & (wk <= narrow)
        hit = term if hit is None else hit | term
    return jnp.broadcast_to(hit.astype(BF16), (nb,) + shape[1:])


def _chunk_cumsum(x, causal):
    nb, c, _ = x.shape
    if c % BF16_ROWS:
        return _exact_bdot_lhs01(causal.astype(BF16), x, 2, 1)
    return _bdot(_triangle3(nb, c, 2), jnp.concatenate(_split3(x), axis=1), 2, 1)


def _chunk_cumsum_t(x):
    nb, c, _ = x.shape
    if c % BF16_ROWS:
        return _exact_bdot_rhs01(x, _upper(nb, c), 1, 1)
    return _bdot(jnp.concatenate(_split3(x), axis=1), _triangle3(nb, c, 1), 1, 1)


def _col_bcast(row, lanes=LANES):
    nb, _, k = row.shape
    hi, mid, lo = (p.astype(F32) for p in _split3(row))
    r = lax.broadcasted_iota(jnp.int32, (1, BF16_ROWS, k), 1)
    stacked = jnp.where(r == 0, hi, jnp.where(r == 1, mid, jnp.where(r == 2, lo, 0.0))).astype(BF16)
    return _bdot(stacked, jnp.ones((nb, BF16_ROWS, lanes), BF16), 1, 1)


MXU_DIM = 256
FF_TILES = ((0, 6 * MXU_DIM), (6 * MXU_DIM, D_FF))
assert all(lo % MXU_DIM == 0 and hi % MXU_DIM == 0 for lo, hi in FF_TILES)


def _ffn_half(x, norm_w, w_in_ref, w_out_ref):
    hb = _rms(x, norm_w).astype(BF16)
    acc = None
    for lo, hi in FF_TILES:
        gate = _dot(hb, w_in_ref[:, lo:hi])
        up = _dot(hb, w_in_ref[:, D_FF + lo:D_FF + hi])
        act = (_silu(gate) * up).astype(BF16)
        part = _dot(act, w_out_ref[lo:hi, :])
        acc = part if acc is None else acc + part
    return 0.5 * acc


def _pick_group(first_steps, a, b):
    return jnp.where(pl.program_id(0) < first_steps, a, b)


def _cast_rows(src_ref, *dst_refs):
    if len(dst_refs) == 1:
        dst_refs[0][...] = src_ref[...].astype(BF16)
        return
    main_ref, tail_ref = dst_refs
    main = main_ref.shape[-1]
    rest = src_ref.shape[-1] - main
    main_ref[...] = src_ref[:, :main].astype(BF16)
    tail_ref[...] = jnp.zeros(tail_ref.shape, BF16)
    tail_ref[:, :rest] = src_ref[:, main:].astype(BF16)


def _ab_features(hb, wp_ref, wt_ref, lb_ref, hn_ref, dtb_ref, feat_ref):
    def group(w0):
        return _dot(hb, wp_ref[:, w0:w0 + HGRN_WIDTH])

    def put(c0, val):
        feat_ref[:, c0:c0 + val.shape[-1]] = val

    lb = _lower_bound(lb_ref[...])
    f = lb + (1.0 - lb) * jax.nn.sigmoid(group(_WF))
    put(_K0, 1.0 - f)
    put(_LF0, jnp.log(f))
    put(_Q0, _silu(group(_WQ)))
    put(_G0, _silu(group(_WG)) * hn_ref[...])
    put(_Z0, _silu(group(_WZ)))
    put(_DT0, _softplus(_dot(hb, wt_ref[...]) + dtb_ref[...]))
    put(_XBC0, _dot(hb, wp_ref[:, _WXBC:_WXBC + CONV_DIM]))
    put(_V0, group(_WI))


def _gla_features(hb, wp_ref, wt_ref, wgk_ref, bgk_ref, gn_ref, feat_ref, v_ref):
    def put(c0, val):
        feat_ref[:, c0:c0 + val.shape[-1]] = val

    gk_low = _dot(hb, wt_ref[...])[:, :GK_RANK].astype(BF16)
    gk = _dot(gk_low, wgk_ref[...]) + bgk_ref[...]
    put(_GLF0, -_softplus(-gk) / GK_NORMALIZER)
    put(_GG0, _silu(_dot(hb, wp_ref[:, _GWG:_GWG + GLA_VAL])) * gn_ref[...])
    put(_GQ0, _dot(hb, wp_ref[:, _GWQ:_GWQ + GLA_KEY]) * (GLA_HEAD_K ** -0.5))
    put(_GK0, _dot(hb, wp_ref[:, _GWK:_GWK + GLA_KEY]))
    v_ref[...] = _dot(hb, wp_ref[:, _GWV:_GWV + GLA_VAL]).astype(BF16)


_FEATURES = {
    "ab": (_ab_features, ((AB_FEAT, F32),)),
    "gla": (_gla_features, ((GLA_FEAT, F32), (GLA_VAL, BF16))),
}
N_FEATURE_PARAMS = 3


def _pre_kernel(*refs, first_steps, cast_arity, mode):
    n_x = 1 if first_steps is None else 2
    x_refs, refs = refs[:n_x], refs[n_x:]
    nf_ref, w_in_ref, w_out_ref, nm_ref, wp_ref, wt_ref = refs[:6]
    feat_params, refs = refs[6:6 + N_FEATURE_PARAMS], refs[6 + N_FEATURE_PARAMS:]
    feature_fn, feature_outs = _FEATURES[mode]
    cast_src, refs = refs[:len(cast_arity)], refs[len(cast_arity):]
    x1_ref, feat_refs, cast_dst = refs[0], refs[1:1 + len(feature_outs)], list(refs[1 + len(feature_outs):])
    if first_steps is None:
        x = x_refs[0][...]
    else:
        x = _pick_group(first_steps, x_refs[0][...], x_refs[1][...])
    x1 = x + _ffn_half(x, nf_ref[...], w_in_ref, w_out_ref)
    x1_ref[...] = x1
    hb = _rms(x1, nm_ref[...]).astype(BF16)
    feature_fn(hb, wp_ref, wt_ref, *feat_params, *feat_refs)
    for src_ref, arity in zip(cast_src, cast_arity):
        _cast_rows(src_ref, *cast_dst[:arity])
        cast_dst = cast_dst[arity:]


def _post_kernel(x_ref, oa_ref, ob_ref, wo_ref, nf_ref, w_in_ref, w_out_ref, nfin_ref, *y_refs, first_steps):
    o = _pick_group(first_steps, oa_ref[...].astype(BF16), ob_ref[...].astype(BF16))
    x2 = x_ref[...] + _dot(o, wo_ref[...])
    y = x2 + _ffn_half(x2, nf_ref[...], w_in_ref, w_out_ref)
    if len(y_refs) == 1:
        y_refs[0][...] = y
    else:
        y = _rms(y, nfin_ref[...])
        ya_ref, yb_ref = y_refs

        @pl.when(pl.program_id(0) < first_steps)
        def _():
            ya_ref[...] = y

        @pl.when(pl.program_id(0) >= first_steps)
        def _():
            yb_ref[...] = y


def _resident(shape, layer=None):
    if layer is None:
        return pl.BlockSpec(shape, lambda *_: (0,) * len(shape), pipeline_mode=pl.Buffered(1))
    return pl.BlockSpec((None,) + tuple(shape[1:]), lambda *_: (layer,) + (0,) * (len(shape) - 1),
                        pipeline_mode=pl.Buffered(1))


def _row_tile(rows, want):
    t = min(rows, want)
    assert rows % t == 0
    return t


PRE0_ROWS = 256
PRE1_ROWS = 512
POST_ROWS = 512


def _group_specs(tm, first_steps, width):
    first = pl.BlockSpec((tm, width), lambda i: (jnp.minimum(i, first_steps - 1), 0))
    second = pl.BlockSpec((tm, width), lambda i: (jnp.maximum(i - first_steps, 0), 0))
    return first, second


def _weight_spec(w, layer):
    return _resident(w.shape, layer if w.ndim == 3 else None)


class _Cast:
    def __init__(self, src, rows_per_step, layer=None, split=False):
        self.src, self.rps, self.layer, self.split = src, rows_per_step, layer, split
        rows = src.shape[-2]
        assert rows % rows_per_step == 0 and rows_per_step % (2 * SUBLANES) == 0
        self.steps = rows // rows_per_step
        assert not split or layer is not None or src.ndim == 2

    def _index(self, lead):
        last = self.steps - 1
        return lambda i: lead + (jnp.minimum(i, last), 0)

    def in_spec(self):
        cols = self.src.shape[-1]
        if self.src.ndim == 2:
            return pl.BlockSpec((self.rps, cols), self._index(()))
        if self.layer is None:
            return pl.BlockSpec((self.src.shape[0], self.rps, cols), self._index((0,)))
        return pl.BlockSpec((None, self.rps, cols), self._index((self.layer,)))

    def outs(self):
        rows, cols = self.src.shape[-2:]
        if self.src.ndim == 3 and self.layer is None:
            n = self.src.shape[0]
            return [(jax.ShapeDtypeStruct((n, rows, cols), BF16), pl.BlockSpec((n, self.rps, cols), self._index((0,))))]
        widths = [cols // LANES * LANES, LANES] if self.split else [cols]
        return [(jax.ShapeDtypeStruct((rows, w), BF16), pl.BlockSpec((self.rps, w), self._index(()))) for w in widths]


def _pre_call(xa, xb, layer, nf, w_in, w_out, nm, wp, wt, feat_params, casts=(), *, mode, row_tile, name):
    ra = xa.shape[0]
    rb = 0 if xb is None else xb.shape[0]
    rows = ra + rb
    tm = _row_tile(rb if rb else ra, row_tile)
    assert ra % tm == 0
    assert all(cast.steps <= rows // tm for cast in casts)
    assert len(feat_params) == N_FEATURE_PARAMS
    feature_outs = _FEATURES[mode][1]
    tok = lambda n: pl.BlockSpec((tm, n), lambda i: (i, 0))
    if xb is None:
        first_steps, x_specs, xs = None, [tok(D_MODEL)], (xa,)
    else:
        first_steps = ra // tm
        x_specs, xs = list(_group_specs(tm, first_steps, D_MODEL)), (xa, xb)
    cast_outs = [cast.outs() for cast in casts]
    flat_outs = [o for outs in cast_outs for o in outs]
    return pl.pallas_call(
        functools.partial(_pre_kernel, first_steps=first_steps, cast_arity=tuple(len(o) for o in cast_outs),
                          mode=mode),
        grid=(rows // tm,),
        in_specs=x_specs + [_resident(nf.shape, layer), _weight_spec(w_in, layer), _weight_spec(w_out, layer),
                            _resident(nm.shape, layer), _resident(wp.shape), _resident(wt.shape)]
                         + [_resident(p.shape) for p in feat_params] + [cast.in_spec() for cast in casts],
        out_specs=[tok(D_MODEL)] + [tok(w) for w, _ in feature_outs] + [spec for _, spec in flat_outs],
        out_shape=[jax.ShapeDtypeStruct((rows, D_MODEL), F32)]
                  + [jax.ShapeDtypeStruct((rows, w), dt) for w, dt in feature_outs] + [shape for shape, _ in flat_outs],
        compiler_params=pltpu.CompilerParams(dimension_semantics=("arbitrary",), vmem_limit_bytes=VMEM_LIMIT_BYTES),
        name=name,
    )(*xs, nf, w_in, w_out, nm, wp, wt, *feat_params, *[cast.src for cast in casts])


def _post_call(x, oa, ob, layer, wo, nf, w_in, w_out, nfin, *, split_output, name):
    rows = x.shape[0]
    ra, rb = oa.shape[0], ob.shape[0]
    assert ra + rb == rows
    tm = _row_tile(rb, POST_ROWS)
    assert ra % tm == 0
    first_steps = ra // tm
    tok = lambda n: pl.BlockSpec((tm, n), lambda i: (i, 0))
    spec_a, spec_b = _group_specs(tm, first_steps, oa.shape[1])
    if split_output:
        out_specs = list(_group_specs(tm, first_steps, D_MODEL))
        out_shape = [jax.ShapeDtypeStruct((ra, D_MODEL), F32), jax.ShapeDtypeStruct((rb, D_MODEL), F32)]
    else:
        out_specs = [tok(D_MODEL)]
        out_shape = [jax.ShapeDtypeStruct((rows, D_MODEL), F32)]
    return pl.pallas_call(
        functools.partial(_post_kernel, first_steps=first_steps),
        grid=(rows // tm,),
        in_specs=[tok(D_MODEL), spec_a, spec_b, _resident(wo.shape), _resident(nf.shape, layer),
                  _resident(w_in.shape, layer), _resident(w_out.shape, layer), _resident(nfin.shape)],
        out_specs=out_specs,
        out_shape=out_shape,
        compiler_params=pltpu.CompilerParams(dimension_semantics=("arbitrary",), vmem_limit_bytes=VMEM_LIMIT_BYTES),
        name=name,
    )(x, oa, ob, wo, nf, w_in, w_out, nfin)


def _gla_heads(q, k, v, log_f, gate, n_heads, dk, dv, causal, sequential, read_state, write_state,
               o_ref, o_col0):
    nb, c, _ = q.shape
    g = _chunk_cumsum(log_f, causal)
    g_last = g[:, c - 1:c, :]
    q_dec = (q * jnp.exp(g)).astype(BF16)
    k_inv_f = k * jnp.exp(-g)
    k_inv = k_inv_f.astype(BF16)
    decay = jnp.exp(g_last)
    k_end = (k_inv_f * decay).astype(BF16)
    vb = v.astype(BF16)
    heads = range(n_heads)
    ks = [slice(h * dk, (h + 1) * dk) for h in heads]
    vs = [slice(h * dv, (h + 1) * dv) for h in heads]
    scores = [_bdot(q_dec[:, :, ks[h]], k_inv[:, :, ks[h]], 2, 2) for h in heads]
    scores = [jnp.where(causal, sc, 0.0).astype(BF16) for sc in scores]
    o_intra = [_bdot(scores[h], vb[:, :, vs[h]], 2, 1) for h in heads]

    def emit(h, b, o):
        cols = slice(o_col0 + h * dv, o_col0 + (h + 1) * dv)
        o_ref[b, :, cols] = (_rms_core(o) * gate[b, :, vs[h]]).astype(o_ref.dtype)

    if sequential:
        kv_t = [_bdot(vb[:, :, vs[h]], k_end[:, :, ks[h]], 1, 1) for h in heads]
        for b in range(nb):
            for h in heads:
                s_t = read_state(h)
                o_inter = lax.dot_general(q_dec[b, :, ks[h]], s_t.astype(BF16), (((1,), (1,)), ((), ())),
                                          preferred_element_type=F32)
                write_state(h, decay[b, :, ks[h]] * s_t + kv_t[h][b])
                emit(h, b, o_intra[h][b] + o_inter)
    else:
        for h in heads:
            kv = _bdot(k_end[:, :, ks[h]], vb[:, :, vs[h]], 1, 1)
            decay_col = jnp.exp(_col_bcast(g_last[:, :, ks[h]]))
            decay_col = jnp.concatenate([decay_col] * (dv // LANES), axis=-1)
            s0 = read_state(h)
            o_inter = _bdot(q_dec[:, :, ks[h]], s0.astype(BF16), 2, 1)
            write_state(h, decay_col * s0 + kv)
            emit(h, slice(None), o_intra[h] + o_inter)


def _expand_heads(x, expand):
    return _exact_bdot_rhs01(x, expand, 2, 1)


def _ssd_heads(xs, bs, cs, z, dt, a_row, d_row, norm_w, causal, sequential, read_state, write_state,
               o_ref, o_col0):
    nb, c, _ = xs.shape
    hrow = lax.broadcasted_iota(jnp.int32, (SSM_HEADS, SSM_INNER), 0)
    hcol = lax.broadcasted_iota(jnp.int32, (SSM_HEADS, SSM_INNER), 1) // SSM_HEAD_DIM
    expand2d = (hrow == hcol).astype(BF16)
    expand = jnp.broadcast_to(expand2d[None], (nb, SSM_HEADS, SSM_INNER))
    d_x = sum(_dot(p, expand2d) for p in _split3(d_row))
    dta = dt * a_row
    cum = _chunk_cumsum(dta, causal)
    cum_t = _chunk_cumsum_t(dta)
    dt_x = _expand_heads(dt, expand)
    cum_x = _expand_heads(cum, expand)
    xdt = xs * dt_x
    cum_last = cum_x[:, c - 1:c, :]
    x_end = (xdt * jnp.exp(cum_last - cum_x)).astype(BF16)
    chunk_dec = jnp.exp(cum_x)
    bsb = bs.astype(BF16)
    csb = cs.astype(BF16)
    lane_head = lax.broadcasted_iota(jnp.int32, (nb, c, SSM_GROUP_WIDTH), 2) // SSM_HEAD_DIM
    groups = range(SSM_GROUPS)
    gl = [slice(g * SSM_STATE, (g + 1) * SSM_STATE) for g in groups]
    hl = [slice(g * SSM_GROUP_WIDTH, (g + 1) * SSM_GROUP_WIDTH) for g in groups]
    cb = [_bdot(csb[:, :, gl[g]], bsb[:, :, gl[g]], 2, 2) for g in groups]
    y_intra = []
    for g in groups:
        xdt_g = xdt[:, :, hl[g]]
        y = None
        for r in range(HEADS_PER_GROUP):
            h = g * HEADS_PER_GROUP + r
            col = jnp.broadcast_to(cum[:, :, h:h + 1], (nb, c, c))
            row = cum_t[:, h:h + 1, :]
            dec = jnp.where(causal, jnp.exp(col - row), 0.0)
            lmat = (cb[g] * dec).astype(BF16)
            xm = jnp.where(lane_head == r, xdt_g, 0.0).astype(BF16)
            part = _bdot(lmat, xm, 2, 1)
            y = part if y is None else y + part
        y_intra.append(y)

    def finish(g, y_in, y_inter, b):
        y_all = y_in + y_inter * chunk_dec[b, :, hl[g]] + d_x[:, hl[g]] * xs[b, :, hl[g]]
        y_all = _rms(y_all * z[b, :, hl[g]], norm_w[:, hl[g]])
        cols = slice(o_col0 + g * SSM_GROUP_WIDTH, o_col0 + (g + 1) * SSM_GROUP_WIDTH)
        o_ref[b, :, cols] = y_all.astype(o_ref.dtype)

    if sequential:
        kv_t = [_bdot(bsb[:, :, gl[g]], x_end[:, :, hl[g]], 1, 1) for g in groups]
        for b in range(nb):
            for g in groups:
                s_t = read_state(g)
                y_inter = _dot(csb[b, :, gl[g]], s_t.astype(BF16))
                write_state(g, chunk_dec[b, c - 1:c, hl[g]] * s_t + kv_t[g][b])
                finish(g, y_intra[g][b], y_inter, b)
    else:
        for g in groups:
            kv = _bdot(x_end[:, :, hl[g]], bsb[:, :, gl[g]], 1, 1)
            decay = jnp.exp(_col_bcast(cum_last[:, :, hl[g]], SSM_STATE))
            s0 = read_state(g)
            y_inter = _bdot(csb[:, :, gl[g]], s0.astype(BF16), 2, 2)
            write_state(g, decay * s0 + kv)
            finish(g, y_intra[g], y_inter, slice(None))


def _upper(nb, c):
    r = lax.broadcasted_iota(jnp.int32, (nb, c, c), 1)
    col = lax.broadcasted_iota(jnp.int32, (nb, c, c), 2)
    return (r <= col).astype(BF16)


def _lower_bound(lb_logits):
    m = jnp.max(lb_logits, axis=0, keepdims=True)
    e = jnp.exp(lb_logits - m)
    return e[0:1, :] / jnp.sum(e, axis=0, keepdims=True)


def _ab_math(feat, conv, alog_ref, d_ref, sn_ref, sequential, read_h, write_h, read_s, write_s, o_ref):
    nb, c, _ = o_ref.shape
    causal = _causal(nb, c)
    w = HGRN_WIDTH
    _gla_heads(feat[:, :, _Q0:_Q0 + w], feat[:, :, _K0:_K0 + w], feat[:, :, _V0:_V0 + w], feat[:, :, _LF0:_LF0 + w],
               feat[:, :, _G0:_G0 + w], HGRN_HEADS, HGRN_HEAD_DIM, HGRN_HEAD_DIM, causal, sequential,
               read_h, write_h, o_ref, 0)
    act = _silu(conv)
    _ssd_heads(act[:, :, :SSM_INNER], act[:, :, SSM_INNER:SSM_INNER + SSM_GROUPS * SSM_STATE],
               act[:, :, SSM_INNER + SSM_GROUPS * SSM_STATE:], feat[:, :, _Z0:_Z0 + SSM_INNER],
               feat[:, :, _DT0:_DT0 + SSM_HEADS], -jnp.exp(alog_ref[...]), d_ref[...], sn_ref[...], causal,
               sequential, read_s, write_s, o_ref, HGRN_WIDTH)


def _ab_seq_kernel(proj_ref, cw_ref, cbias_ref, alog_ref, d_ref, sn_ref,
                   o_ref, sh_out, ss_out, sc_out, sh, ss, xpad):
    t = pl.program_id(1)
    nb, c, _ = o_ref.shape
    rows = nb * c

    @pl.when(t == 0)
    def _():
        sh[...] = jnp.zeros_like(sh)
        ss[...] = jnp.zeros_like(ss)
        xpad[0:SUBLANES, :] = jnp.zeros((SUBLANES, CONV_DIM), F32)

    xpad[SUBLANES:SUBLANES + rows, :] = proj_ref[:, :, _XBC0:_XBC0 + CONV_DIM].reshape(rows, CONV_DIM)
    padded = xpad[...]
    conv = cbias_ref[...] + padded[SUBLANES:] * cw_ref[CONV_W - 1:CONV_W, :]
    for d in range(1, CONV_W):
        conv = conv + pltpu.roll(padded, d, 0)[SUBLANES:] * cw_ref[CONV_W - 1 - d:CONV_W - d, :]
    xpad[0:SUBLANES, :] = padded[rows:rows + SUBLANES]
    conv = conv.reshape(nb, c, CONV_DIM)

    def read_h(h):
        return sh[h]

    def write_h(h, s):
        sh[h] = s

    def read_s(g):
        return ss[g]

    def write_s(g, s):
        ss[g] = s

    _ab_math(proj_ref, conv, alog_ref, d_ref, sn_ref, True, read_h, write_h, read_s, write_s, o_ref)

    @pl.when(t == pl.num_programs(1) - 1)
    def _():
        for h in range(HGRN_HEADS):
            sh_out[0, h] = sh[h].T
        for g in range(SSM_GROUPS):
            ss_out[0, g] = ss[g].T
        sc_out[0] = xpad[SUBLANES - (CONV_W - 1):SUBLANES, :]


def _ab_par_kernel(proj_ref, sh_in, ss_in, sc_in, cw_ref, cbias_ref, alog_ref, d_ref, sn_ref,
                   o_ref, sh_out, ss_out, sc_out, xpad, o3):
    nb, c, _ = o3.shape
    nbuf = CONV_W - 1
    proj = proj_ref[...].reshape(nb, c, proj_ref.shape[-1])
    xpad[:, 0:nbuf, :] = sc_in[...]
    xpad[:, nbuf:nbuf + c, :] = proj[:, :, _XBC0:_XBC0 + CONV_DIM]
    conv = cbias_ref[...]
    for k in range(CONV_W):
        conv = conv + xpad[:, k:k + c, :] * cw_ref[k:k + 1, :]
    sc_out[...] = xpad[:, c:c + nbuf, :]

    def read_h(h):
        return sh_in[:, h]

    def write_h(h, s):
        sh_out[:, h] = s

    def read_s(g):
        return ss_in[:, g]

    def write_s(g, s):
        ss_out[:, g] = s

    _ab_math(proj, conv, alog_ref, d_ref, sn_ref, False, read_h, write_h, read_s, write_s, o3)
    o_ref[...] = o3[...].reshape(o_ref.shape)


def _gla_math(feat, v, sequential, read_g, write_g, o_ref):
    nb, c, _ = o_ref.shape
    _gla_heads(feat[:, :, _GQ0:_GQ0 + GLA_KEY], feat[:, :, _GK0:_GK0 + GLA_KEY], v[...],
               feat[:, :, _GLF0:_GLF0 + GLA_KEY], feat[:, :, _GG0:_GG0 + GLA_VAL], GLA_HEADS, GLA_HEAD_K,
               GLA_HEAD_V, _causal(nb, c), sequential, read_g, write_g, o_ref, 0)


def _gla_seq_kernel(proj_ref, v_ref, o_ref, sg_out, sg):
    t = pl.program_id(1)

    @pl.when(t == 0)
    def _():
        sg[...] = jnp.zeros_like(sg)

    def read_g(h):
        return sg[h]

    def write_g(h, s):
        sg[h] = s

    _gla_math(proj_ref, v_ref, True, read_g, write_g, o_ref)

    @pl.when(t == pl.num_programs(1) - 1)
    def _():
        for h in range(GLA_HEADS):
            sg_out[0, h] = sg[h].T


def _gla_par_kernel(proj_ref, v_ref, sg_in, o_ref, sg_out, o3):
    def read_g(h):
        return sg_in[:, h]

    def write_g(h, s):
        sg_out[:, h] = s

    nb, c, _ = o3.shape
    proj = proj_ref[...].reshape(nb, c, proj_ref.shape[-1])
    v = v_ref[...].astype(F32).reshape(nb, c, v_ref.shape[-1])
    _gla_math(proj, v, False, read_g, write_g, o3)
    o_ref[...] = o3[...].reshape(o_ref.shape)


SEQ_TILE_CHUNKS = 8
GLA_SEQ_TILE_CHUNKS = 8
PAR_TILE_SEQS = 8


def _full(shape):
    return pl.BlockSpec(shape, lambda *_: (0,) * len(shape))


def _mixer_o_dtype(c):
    return BF16 if c % (2 * SUBLANES) == 0 else F32


def _ab_seq_call(proj, batch, length, params, *, name):
    c = math.gcd(length, CHUNK)
    nb = SEQ_TILE_CHUNKS
    tiles = length // (c * nb)
    assert tiles * c * nb == length
    width = HEADS_PER_GROUP * SSM_HEAD_DIM
    blk = lambda n: pl.BlockSpec((nb, c, n), lambda b, t: (b * tiles + t, 0, 0))
    out_shapes = [
        jax.ShapeDtypeStruct((batch * length // c, c, AB_WIDTH), _mixer_o_dtype(c)),
        jax.ShapeDtypeStruct((batch, HGRN_HEADS, HGRN_HEAD_DIM, HGRN_HEAD_DIM), F32),
        jax.ShapeDtypeStruct((batch, SSM_GROUPS, width, SSM_STATE), F32),
        jax.ShapeDtypeStruct((batch, CONV_W - 1, CONV_DIM), F32),
    ]
    out_specs = [
        blk(AB_WIDTH),
        pl.BlockSpec((1, HGRN_HEADS, HGRN_HEAD_DIM, HGRN_HEAD_DIM), lambda b, t: (b, 0, 0, 0)),
        pl.BlockSpec((1, SSM_GROUPS, width, SSM_STATE), lambda b, t: (b, 0, 0, 0)),
        pl.BlockSpec((1, CONV_W - 1, CONV_DIM), lambda b, t: (b, 0, 0)),
    ]
    return pl.pallas_call(
        _ab_seq_kernel,
        grid=(batch, tiles),
        in_specs=[blk(AB_FEAT)] + [_full(p.shape) for p in params],
        out_specs=out_specs,
        out_shape=out_shapes,
        scratch_shapes=[
            pltpu.VMEM((HGRN_HEADS, HGRN_HEAD_DIM, HGRN_HEAD_DIM), F32),
            pltpu.VMEM((SSM_GROUPS, SSM_STATE, width), F32),
            pltpu.VMEM((nb * c + SUBLANES, CONV_DIM), F32),
        ],
        compiler_params=pltpu.CompilerParams(dimension_semantics=("arbitrary", "arbitrary"),
                                             vmem_limit_bytes=VMEM_LIMIT_BYTES),
        name=name,
    )(proj, *params)


def _ab_par_call(proj, row0, c, s_hgrn, s_ssm, s_conv, params, *, name):
    batch = s_hgrn.shape[0]
    nb = _row_tile(batch, PAR_TILE_SEQS)
    rows = nb * c
    assert row0 % rows == 0
    width = HEADS_PER_GROUP * SSM_HEAD_DIM
    s_ssm = s_ssm.reshape(batch, SSM_GROUPS, width, SSM_STATE)
    blk3 = lambda a, n: pl.BlockSpec((nb, a, n), lambda b: (b, 0, 0))
    blk4 = lambda a, r, n: pl.BlockSpec((nb, a, r, n), lambda b: (b, 0, 0, 0))
    state_specs = [blk4(HGRN_HEADS, HGRN_HEAD_DIM, HGRN_HEAD_DIM), blk4(SSM_GROUPS, width, SSM_STATE),
                   blk3(CONV_W - 1, CONV_DIM)]
    out_shapes = [
        jax.ShapeDtypeStruct((batch * c, AB_WIDTH), F32),
        jax.ShapeDtypeStruct(s_hgrn.shape, F32),
        jax.ShapeDtypeStruct(s_ssm.shape, F32),
        jax.ShapeDtypeStruct(s_conv.shape, F32),
    ]
    return pl.pallas_call(
        _ab_par_kernel,
        grid=(batch // nb,),
        in_specs=[pl.BlockSpec((rows, proj.shape[1]), lambda b: (row0 // rows + b, 0))] + state_specs
                 + [_full(p.shape) for p in params],
        out_specs=[pl.BlockSpec((rows, AB_WIDTH), lambda b: (b, 0))] + state_specs,
        out_shape=out_shapes,
        scratch_shapes=[pltpu.VMEM((nb, c + CONV_W - 1, CONV_DIM), F32), pltpu.VMEM((nb, c, AB_WIDTH), F32)],
        compiler_params=pltpu.CompilerParams(dimension_semantics=("arbitrary",),
                                             vmem_limit_bytes=VMEM_LIMIT_BYTES),
        name=name,
    )(proj, s_hgrn, s_ssm, s_conv, *params)


def _gla_seq_call(proj, v, batch, length, *, name):
    c = math.gcd(length, CHUNK)
    nb = GLA_SEQ_TILE_CHUNKS
    tiles = length // (c * nb)
    assert tiles * c * nb == length
    blk = lambda n: pl.BlockSpec((nb, c, n), lambda b, t: (b * tiles + t, 0, 0))
    return pl.pallas_call(
        _gla_seq_kernel,
        grid=(batch, tiles),
        in_specs=[blk(GLA_FEAT), blk(GLA_VAL)],
        out_specs=[blk(GLA_VAL),
                   pl.BlockSpec((1, GLA_HEADS, GLA_HEAD_K, GLA_HEAD_V), lambda b, t: (b, 0, 0, 0))],
        out_shape=[jax.ShapeDtypeStruct((batch * length // c, c, GLA_VAL), _mixer_o_dtype(c)),
                   jax.ShapeDtypeStruct((batch, GLA_HEADS, GLA_HEAD_K, GLA_HEAD_V), F32)],
        scratch_shapes=[pltpu.VMEM((GLA_HEADS, GLA_HEAD_V, GLA_HEAD_K), F32)],
        compiler_params=pltpu.CompilerParams(dimension_semantics=("arbitrary", "arbitrary"),
                                             vmem_limit_bytes=VMEM_LIMIT_BYTES),
        name=name,
    )(proj, v)


def _gla_par_call(proj, v, row0, c, s_gla, *, name):
    batch = s_gla.shape[0]
    nb = _row_tile(batch, PAR_TILE_SEQS)
    rows = nb * c
    assert row0 % rows == 0
    sspec = pl.BlockSpec((nb, GLA_HEADS, GLA_HEAD_K, GLA_HEAD_V), lambda b: (b, 0, 0, 0))
    return pl.pallas_call(
        _gla_par_kernel,
        grid=(batch // nb,),
        in_specs=[pl.BlockSpec((rows, proj.shape[1]), lambda b: (row0 // rows + b, 0)),
                  pl.BlockSpec((rows, v.shape[1]), lambda b: (row0 // rows + b, 0)), sspec],
        out_specs=[pl.BlockSpec((rows, GLA_VAL), lambda b: (b, 0)), sspec],
        out_shape=[jax.ShapeDtypeStruct((batch * c, GLA_VAL), F32), jax.ShapeDtypeStruct(s_gla.shape, F32)],
        scratch_shapes=[pltpu.VMEM((nb, c, GLA_VAL), F32)],
        compiler_params=pltpu.CompilerParams(dimension_semantics=("arbitrary",),
                                             vmem_limit_bytes=VMEM_LIMIT_BYTES),
        name=name,
    )(proj, v, s_gla)


def _split_cols(w):
    main = w.shape[1] // LANES * LANES
    tail = jnp.pad(w[:, main:].astype(BF16), ((0, 0), (0, LANES - (w.shape[1] - main))))
    return w[:, :main].astype(BF16), tail


def kernel(x_prompt, x_sample, state_hgrn, state_ssm, state_conv, state_gla, norm_ffn1, norm_mix, norm_ffn2, norm_final, ffn1_w_in, ffn1_w_out, ffn2_w_in, ffn2_w_out, ab_w_in, ab_w_out, hgrn_lb_logits, hgrn_norm, ssm_conv_w, ssm_conv_b, ssm_dt_bias, ssm_a_log, ssm_d, ssm_norm, gla_w_in, gla_w_gk, gla_b_gk, gla_norm, gla_w_out):
    assert ab_w_in.shape[0] == 1 and gla_w_in.shape[0] == 1, "one HGRN2/SSD layer and one GLA layer"
    bp, lp, _ = x_prompt.shape
    bs, ls, _ = x_sample.shape
    rp, rs = bp * lp, bs * ls
    cp, cs = math.gcd(lp, CHUNK), math.gcd(ls, CHUNK)
    assert rs % cp == 0
    row = lambda v: v.reshape(1, -1)
    stack_row = lambda v: v.reshape(v.shape[0], 1, v.shape[1])
    nf1, nm, nf2 = stack_row(norm_ffn1), stack_row(norm_mix), stack_row(norm_ffn2)
    in_rows, out_rows = D_MODEL // 64, D_FF // 16
    xp = x_prompt.reshape(rp, D_MODEL)
    xs = x_sample.reshape(rs, D_MODEL)

    wp, wt = _split_cols(ab_w_in[0])
    ab_feat_params = (hgrn_lb_logits, jnp.tile(hgrn_norm[0], HGRN_HEADS).reshape(1, -1),
                      jnp.pad(ssm_dt_bias[0], (0, LANES - SSM_HEADS)).reshape(1, -1))
    casts = (_Cast(ffn2_w_in, in_rows), _Cast(ffn2_w_out, out_rows), _Cast(ffn1_w_in, in_rows, layer=1),
             _Cast(ffn1_w_out, out_rows, layer=1), _Cast(ab_w_out, in_rows, layer=0),
             _Cast(gla_w_in[0], in_rows, split=True), _Cast(gla_w_out, in_rows, layer=0))
    (x1, feat, w2_in, w2_out, w1_in, w1_out, ab_wo, gla_wp, gla_wt, gla_wo) = _pre_call(
        xp, xs, 0, nf1, ffn1_w_in[0].astype(BF16), ffn1_w_out[0].astype(BF16), nm, wp, wt, ab_feat_params, casts,
        mode="ab", row_tile=PRE0_ROWS, name="pre0")
    ab_params = (ssm_conv_w[0], row(ssm_conv_b[0]), row(ssm_a_log[0]), row(ssm_d[0]), row(ssm_norm[0]))
    o_p, hgrn_p, ssm_p, conv_p = _ab_seq_call(feat.reshape((rp + rs) // cp, cp, AB_FEAT), bp, lp, ab_params,
                                              name="mix0_prompt")
    o_s, hgrn_s, ssm_s, conv_s = _ab_par_call(feat, rp, cs, state_hgrn[0], state_ssm[0], state_conv[0], ab_params,
                                              name="mix0_sample")
    (x3,) = _post_call(x1, o_p.reshape(rp, AB_WIDTH), o_s, 0, ab_wo, nf2,
                       w2_in, w2_out, row(norm_final), split_output=False, name="post0")

    gla_feat_params = (gla_w_gk[0].astype(BF16), row(gla_b_gk[0]), jnp.tile(gla_norm[0], GLA_HEADS).reshape(1, -1))
    x4, feat, v16 = _pre_call(x3, None, 1, nf1, w1_in, w1_out, nm, gla_wp, gla_wt, gla_feat_params, mode="gla",
                              row_tile=PRE1_ROWS, name="pre1")
    o_p, gla_p = _gla_seq_call(feat.reshape((rp + rs) // cp, cp, GLA_FEAT), v16.reshape((rp + rs) // cp, cp, GLA_VAL),
                               bp, lp, name="mix1_prompt")
    o_s, gla_s = _gla_par_call(feat, v16, rp, cs, state_gla[0], name="mix1_sample")
    y_p, y_s = _post_call(x4, o_p.reshape(rp, GLA_VAL), o_s, 1, gla_wo, nf2,
                          w2_in, w2_out, row(norm_final), split_output=True, name="post1")

    ssm_shape = (1, -1, SSM_HEADS, SSM_HEAD_DIM, SSM_STATE)
    return (y_p.reshape(bp, lp, D_MODEL), y_s.reshape(bs, ls, D_MODEL), hgrn_p[None], hgrn_s[None],
            ssm_p.reshape(ssm_shape), ssm_s.reshape(ssm_shape), conv_p[None], conv_s[None], gla_p[None], gla_s[None])
```

```python
import functools
import math

import jax
import jax.numpy as jnp
from jax import lax
from jax.experimental import pallas as pl
from jax.experimental.pallas import tpu as pltpu

F32 = jnp.float32
BF16 = jnp.bfloat16

D_MODEL = 1024
D_FF = 2816
EPS = 1e-6
CHUNK = 64

HGRN_HEADS = 4
HGRN_HEAD_DIM = 128
HGRN_WIDTH = HGRN_HEADS * HGRN_HEAD_DIM

SSM_HEADS = 8
SSM_HEAD_DIM = 64
SSM_INNER = SSM_HEADS * SSM_HEAD_DIM
SSM_GROUPS = 2
SSM_STATE = 128
SSM_GROUP_WIDTH = SSM_INNER // SSM_GROUPS
HEADS_PER_GROUP = SSM_HEADS // SSM_GROUPS
CONV_W = 4
CONV_DIM = SSM_INNER + 2 * SSM_GROUPS * SSM_STATE
AB_WIDTH = HGRN_WIDTH + SSM_INNER

GLA_HEADS = 4
GLA_HEAD_K = 128
GLA_HEAD_V = 256
GLA_KEY = GLA_HEADS * GLA_HEAD_K
GLA_VAL = GLA_HEADS * GLA_HEAD_V
GK_RANK = 16
GK_NORMALIZER = 16.0

LANES = 128
SUBLANES = 8
VMEM_LIMIT_BYTES = 56 * 1024 * 1024

_Q0, _K0, _LF0, _V0, _G0 = (i * HGRN_WIDTH for i in range(5))
_Z0 = 5 * HGRN_WIDTH
_XBC0 = _Z0 + SSM_INNER
_DT0 = _XBC0 + CONV_DIM
AB_FEAT = _DT0 + LANES
_WQ, _WF, _WI, _WG, _WZ = (i * HGRN_WIDTH for i in range(5))
_WXBC = 4 * HGRN_WIDTH + SSM_INNER

_GQ0, _GK0, _GLF0 = 0, GLA_KEY, 2 * GLA_KEY
GLA_FEAT = 3 * GLA_KEY
_GQD0, _GKI0, _GKE0 = 0, GLA_KEY, 2 * GLA_KEY
GLA_READY = 3 * GLA_KEY
_GWQ, _GWK = 0, GLA_KEY
_GWV = 2 * GLA_KEY
_GWG = _GWV + GLA_VAL


def _rms(x, w):
    return x * lax.rsqrt(jnp.mean(x * x, axis=-1, keepdims=True) + EPS) * w


def _rms_core(x):
    return x * lax.rsqrt(jnp.mean(x * x, axis=-1, keepdims=True) + EPS)


def _silu(x):
    return x * jax.nn.sigmoid(x)


def _softplus(x):
    return jnp.maximum(x, 0.0) + jnp.log1p(jnp.exp(-jnp.abs(x)))


def _dot(a, b):
    return jnp.dot(a, b, preferred_element_type=F32)


def _bdot(a, b, ca, cb):
    return lax.dot_general(a, b, (((ca,), (cb,)), ((0,), (0,))), preferred_element_type=F32)


def _split3(x):
    hi = x.astype(BF16)
    r1 = x - hi.astype(F32)
    mid = r1.astype(BF16)
    lo = (r1 - mid.astype(F32)).astype(BF16)
    return hi, mid, lo


def _exact_bdot_lhs01(m01, x, ca, cb):
    return sum(_bdot(m01, p, ca, cb) for p in _split3(x))


def _exact_bdot_rhs01(x, m01, ca, cb):
    return sum(_bdot(p, m01, ca, cb) for p in _split3(x))


def _causal(nb, c):
    r = lax.broadcasted_iota(jnp.int32, (nb, c, c), 1)
    col = lax.broadcasted_iota(jnp.int32, (nb, c, c), 2)
    return r >= col


BF16_ROWS = 2 * SUBLANES


def _triangle3(nb, c, wide_axis):
    shape = (1, c, 3 * c) if wide_axis == 2 else (1, 3 * c, c)
    wide = lax.broadcasted_iota(jnp.int32, shape, wide_axis)
    narrow = lax.broadcasted_iota(jnp.int32, shape, 3 - wide_axis)
    hit = None
    for k in range(3):
        wk = wide - k * c
        term = (wk >= 0) & (wk < c) & (wk <= narrow)
        hit = term if hit is None else hit | term
    return jnp.broadcast_to(hit.astype(BF16), (nb,) + shape[1:])


def _chunk_cumsum(x, causal):
    nb, c, _ = x.shape
    if c % BF16_ROWS:
        return _exact_bdot_lhs01(causal.astype(BF16), x, 2, 1)
    return _bdot(_triangle3(nb, c, 2), jnp.concatenate(_split3(x), axis=1), 2, 1)


def _chunk_cumsum_t(x):
    nb, c, _ = x.shape
    if c % BF16_ROWS:
        return _exact_bdot_rhs01(x, _upper(nb, c), 1, 1)
    return _bdot(jnp.concatenate(_split3(x), axis=1), _triangle3(nb, c, 1), 1, 1)


def _col_bcast(row, lanes=LANES):
    nb, _, k = row.shape
    hi, mid, lo = (p.astype(F32) for p in _split3(row))
    r = lax.broadcasted_iota(jnp.int32, (1, BF16_ROWS, k), 1)
    stacked = jnp.where(r == 0, hi, jnp.where(r == 1, mid, jnp.where(r == 2, lo, 0.0))).astype(BF16)
    return _bdot(stacked, jnp.ones((nb, BF16_ROWS, lanes), BF16), 1, 1)


MXU_DIM = 256
FF_TILES = ((0, 6 * MXU_DIM), (6 * MXU_DIM, D_FF))
assert all(lo % MXU_DIM == 0 and hi % MXU_DIM == 0 for lo, hi in FF_TILES)


def _ffn_half(x, norm_w, w_in_ref, w_out_ref):
    hb = _rms(x, norm_w).astype(BF16)
    acc = None
    for lo, hi in FF_TILES:
        gate = _dot(hb, w_in_ref[:, lo:hi])
        up = _dot(hb, w_in_ref[:, D_FF + lo:D_FF + hi])
        act = (_silu(gate) * up).astype(BF16)
        part = _dot(act, w_out_ref[lo:hi, :])
        acc = part if acc is None else acc + part
    return 0.5 * acc


def _pick_group(first_steps, a, b):
    return jnp.where(pl.program_id(0) < first_steps, a, b)


def _cast_rows(src_ref, *dst_refs):
    if len(dst_refs) == 1:
        dst_refs[0][...] = src_ref[...].astype(BF16)
        return
    main_ref, tail_ref = dst_refs
    main = main_ref.shape[-1]
    rest = src_ref.shape[-1] - main
    main_ref[...] = src_ref[:, :main].astype(BF16)
    tail_ref[...] = jnp.zeros(tail_ref.shape, BF16)
    tail_ref[:, :rest] = src_ref[:, main:].astype(BF16)


def _ab_features(hb, wp_ref, wt_ref, lb_ref, hn_ref, dtb_ref, feat_ref):
    def group(w0):
        return _dot(hb, wp_ref[:, w0:w0 + HGRN_WIDTH])

    def put(c0, val):
        feat_ref[:, c0:c0 + val.shape[-1]] = val

    lb = _lower_bound(lb_ref[...])
    f = lb + (1.0 - lb) * jax.nn.sigmoid(group(_WF))
    put(_K0, 1.0 - f)
    put(_LF0, jnp.log(f))
    put(_Q0, _silu(group(_WQ)))
    put(_G0, _silu(group(_WG)) * hn_ref[...])
    put(_Z0, _silu(group(_WZ)))
    put(_DT0, _softplus(_dot(hb, wt_ref[...]) + dtb_ref[...]))
    put(_XBC0, _dot(hb, wp_ref[:, _WXBC:_WXBC + CONV_DIM]))
    put(_V0, group(_WI))


def _decayed_operands(q, k, log_f):
    nb, c, _ = q.shape
    g = _chunk_cumsum(log_f, _causal(nb, c))
    g_last = g[:, c - 1:c, :]
    q_dec = (q * jnp.exp(g)).astype(BF16)
    k_inv_f = k * jnp.exp(-g)
    decay = jnp.exp(g_last)
    k_end = (k_inv_f * decay).astype(BF16)
    return q_dec, k_inv_f.astype(BF16), k_end, decay, g_last


def _gla_features(hb, wp_ref, wt_ref, wgk_ref, bgk_ref, gn_ref, feat_ref, gate_ref, v_ref, ready_ref, decay_ref):
    def put(c0, val):
        feat_ref[:, c0:c0 + val.shape[-1]] = val

    rows = hb.shape[0]
    nb = rows // CHUNK
    gk_low = _dot(hb, wt_ref[...])[:, :GK_RANK].astype(BF16)
    gk = _dot(gk_low, wgk_ref[...]) + bgk_ref[...]
    log_f = -_softplus(-gk) / GK_NORMALIZER
    q = _dot(hb, wp_ref[:, _GWQ:_GWQ + GLA_KEY]) * (GLA_HEAD_K ** -0.5)
    k = _dot(hb, wp_ref[:, _GWK:_GWK + GLA_KEY])
    put(_GLF0, log_f)
    put(_GQ0, q)
    put(_GK0, k)
    chunked = lambda t: t.reshape(nb, CHUNK, GLA_KEY)
    q_dec, k_inv, k_end, decay, _ = _decayed_operands(chunked(q), chunked(k), chunked(log_f))
    for c0, val in ((_GQD0, q_dec), (_GKI0, k_inv), (_GKE0, k_end)):
        ready_ref[:, c0:c0 + GLA_KEY] = val.reshape(rows, GLA_KEY)
    decay_ref[...] = decay.reshape(nb, GLA_KEY)
    gate_ref[...] = _silu(_dot(hb, wp_ref[:, _GWG:_GWG + GLA_VAL])) * gn_ref[...]
    v_ref[...] = _dot(hb, wp_ref[:, _GWV:_GWV + GLA_VAL]).astype(BF16)


_FEATURES = {
    "ab": (_ab_features, ((AB_FEAT, F32, 1),)),
    "gla": (_gla_features, ((GLA_FEAT, F32, 1), (GLA_VAL, F32, 1), (GLA_VAL, BF16, 1), (GLA_READY, BF16, 1),
                            (GLA_KEY, F32, CHUNK))),
}
N_FEATURE_PARAMS = 3


def _pre_kernel(*refs, first_steps, cast_arity, mode):
    n_x = 1 if first_steps is None else 2
    x_refs, refs = refs[:n_x], refs[n_x:]
    nf_ref, w_in_ref, w_out_ref, nm_ref, wp_ref, wt_ref = refs[:6]
    feat_params, refs = refs[6:6 + N_FEATURE_PARAMS], refs[6 + N_FEATURE_PARAMS:]
    feature_fn, feature_outs = _FEATURES[mode]
    cast_src, refs = refs[:len(cast_arity)], refs[len(cast_arity):]
    x1_ref, feat_refs, cast_dst = refs[0], refs[1:1 + len(feature_outs)], list(refs[1 + len(feature_outs):])
    if first_steps is None:
        x = x_refs[0][...]
    else:
        x = _pick_group(first_steps, x_refs[0][...], x_refs[1][...])
    x1 = x + _ffn_half(x, nf_ref[...], w_in_ref, w_out_ref)
    x1_ref[...] = x1
    hb = _rms(x1, nm_ref[...]).astype(BF16)
    feature_fn(hb, wp_ref, wt_ref, *feat_params, *feat_refs)
    for src_ref, arity in zip(cast_src, cast_arity):
        _cast_rows(src_ref, *cast_dst[:arity])
        cast_dst = cast_dst[arity:]


def _post_kernel(x_ref, oa_ref, ob_ref, wo_ref, nf_ref, w_in_ref, w_out_ref, nfin_ref, *y_refs, first_steps):
    o = _pick_group(first_steps, oa_ref[...].astype(BF16), ob_ref[...].astype(BF16))
    x2 = x_ref[...] + _dot(o, wo_ref[...])
    y = x2 + _ffn_half(x2, nf_ref[...], w_in_ref, w_out_ref)
    if len(y_refs) == 1:
        y_refs[0][...] = y
    else:
        y = _rms(y, nfin_ref[...])
        ya_ref, yb_ref = y_refs

        @pl.when(pl.program_id(0) < first_steps)
        def _():
            ya_ref[...] = y

        @pl.when(pl.program_id(0) >= first_steps)
        def _():
            yb_ref[...] = y


def _resident(shape, layer=None):
    if layer is None:
        return pl.BlockSpec(shape, lambda *_: (0,) * len(shape), pipeline_mode=pl.Buffered(1))
    return pl.BlockSpec((None,) + tuple(shape[1:]), lambda *_: (layer,) + (0,) * (len(shape) - 1),
                        pipeline_mode=pl.Buffered(1))


def _row_tile(rows, want):
    t = min(rows, want)
    assert rows % t == 0
    return t


PRE0_ROWS = 256
PRE1_ROWS = 512
POST_ROWS = 512


def _group_specs(tm, first_steps, width):
    first = pl.BlockSpec((tm, width), lambda i: (jnp.minimum(i, first_steps - 1), 0))
    second = pl.BlockSpec((tm, width), lambda i: (jnp.maximum(i - first_steps, 0), 0))
    return first, second


def _weight_spec(w, layer):
    return _resident(w.shape, layer if w.ndim == 3 else None)


class _Cast:
    def __init__(self, src, rows_per_step, layer=None, split=False):
        self.src, self.rps, self.layer, self.split = src, rows_per_step, layer, split
        rows = src.shape[-2]
        assert rows % rows_per_step == 0 and rows_per_step % (2 * SUBLANES) == 0
        self.steps = rows // rows_per_step
        assert not split or layer is not None or src.ndim == 2

    def _index(self, lead):
        last = self.steps - 1
        return lambda i: lead + (jnp.minimum(i, last), 0)

    def in_spec(self):
        cols = self.src.shape[-1]
        if self.src.ndim == 2:
            return pl.BlockSpec((self.rps, cols), self._index(()))
        if self.layer is None:
            return pl.BlockSpec((self.src.shape[0], self.rps, cols), self._index((0,)))
        return pl.BlockSpec((None, self.rps, cols), self._index((self.layer,)))

    def outs(self):
        rows, cols = self.src.shape[-2:]
        if self.src.ndim == 3 and self.layer is None:
            n = self.src.shape[0]
            return [(jax.ShapeDtypeStruct((n, rows, cols), BF16), pl.BlockSpec((n, self.rps, cols), self._index((0,))))]
        widths = [cols // LANES * LANES, LANES] if self.split else [cols]
        return [(jax.ShapeDtypeStruct((rows, w), BF16), pl.BlockSpec((self.rps, w), self._index(()))) for w in widths]


def _pre_call(xa, xb, layer, nf, w_in, w_out, nm, wp, wt, feat_params, casts=(), *, mode, row_tile, name):
    ra = xa.shape[0]
    rb = 0 if xb is None else xb.shape[0]
    rows = ra + rb
    tm = _row_tile(rb if rb else ra, row_tile)
    assert ra % tm == 0
    assert all(cast.steps <= rows // tm for cast in casts)
    assert len(feat_params) == N_FEATURE_PARAMS
    feature_outs = _FEATURES[mode][1]
    tok = lambda n: pl.BlockSpec((tm, n), lambda i: (i, 0))
    if xb is None:
        first_steps, x_specs, xs = None, [tok(D_MODEL)], (xa,)
    else:
        first_steps = ra // tm
        x_specs, xs = list(_group_specs(tm, first_steps, D_MODEL)), (xa, xb)
    cast_outs = [cast.outs() for cast in casts]
    flat_outs = [o for outs in cast_outs for o in outs]
    return pl.pallas_call(
        functools.partial(_pre_kernel, first_steps=first_steps, cast_arity=tuple(len(o) for o in cast_outs),
                          mode=mode),
        grid=(rows // tm,),
        in_specs=x_specs + [_resident(nf.shape, layer), _weight_spec(w_in, layer), _weight_spec(w_out, layer),
                            _resident(nm.shape, layer), _resident(wp.shape), _resident(wt.shape)]
                         + [_resident(p.shape) for p in feat_params] + [cast.in_spec() for cast in casts],
        out_specs=[tok(D_MODEL)] + [pl.BlockSpec((tm // div, w), lambda i: (i, 0)) for w, _, div in feature_outs]
                  + [spec for _, spec in flat_outs],
        out_shape=[jax.ShapeDtypeStruct((rows, D_MODEL), F32)]
                  + [jax.ShapeDtypeStruct((rows // div, w), dt) for w, dt, div in feature_outs]
                  + [shape for shape, _ in flat_outs],
        compiler_params=pltpu.CompilerParams(dimension_semantics=("arbitrary",), vmem_limit_bytes=VMEM_LIMIT_BYTES),
        name=name,
    )(*xs, nf, w_in, w_out, nm, wp, wt, *feat_params, *[cast.src for cast in casts])


def _post_call(x, oa, ob, layer, wo, nf, w_in, w_out, nfin, *, split_output, name):
    rows = x.shape[0]
    ra, rb = oa.shape[0], ob.shape[0]
    assert ra + rb == rows
    tm = _row_tile(rb, POST_ROWS)
    assert ra % tm == 0
    first_steps = ra // tm
    tok = lambda n: pl.BlockSpec((tm, n), lambda i: (i, 0))
    spec_a, spec_b = _group_specs(tm, first_steps, oa.shape[1])
    if split_output:
        out_specs = list(_group_specs(tm, first_steps, D_MODEL))
        out_shape = [jax.ShapeDtypeStruct((ra, D_MODEL), F32), jax.ShapeDtypeStruct((rb, D_MODEL), F32)]
    else:
        out_specs = [tok(D_MODEL)]
        out_shape = [jax.ShapeDtypeStruct((rows, D_MODEL), F32)]
    return pl.pallas_call(
        functools.partial(_post_kernel, first_steps=first_steps),
        grid=(rows // tm,),
        in_specs=[tok(D_MODEL), spec_a, spec_b, _resident(wo.shape), _resident(nf.shape, layer),
                  _resident(w_in.shape, layer), _resident(w_out.shape, layer), _resident(nfin.shape)],
        out_specs=out_specs,
        out_shape=out_shape,
        compiler_params=pltpu.CompilerParams(dimension_semantics=("arbitrary",), vmem_limit_bytes=VMEM_LIMIT_BYTES),
        name=name,
    )(x, oa, ob, wo, nf, w_in, w_out, nfin)


def _gla_heads(q, k, v, log_f, gate, n_heads, dk, dv, causal, sequential, read_state, write_state,
               o_ref, o_col0):
    q_dec, k_inv, k_end, decay, g_last = _decayed_operands(q, k, log_f)
    _gla_core(q_dec, k_inv, k_end, decay, g_last, v, gate, n_heads, dk, dv, causal, sequential, read_state,
              write_state, o_ref, o_col0)


def _gla_core(q_dec, k_inv, k_end, decay, g_last, v, gate, n_heads, dk, dv, causal, sequential, read_state,
              write_state, o_ref, o_col0):
    nb, c, _ = q_dec.shape
    vb = v.astype(BF16)
    heads = range(n_heads)
    ks = [slice(h * dk, (h + 1) * dk) for h in heads]
    vs = [slice(h * dv, (h + 1) * dv) for h in heads]
    scores = [_bdot(q_dec[:, :, ks[h]], k_inv[:, :, ks[h]], 2, 2) for h in heads]
    scores = [jnp.where(causal, sc, 0.0).astype(BF16) for sc in scores]
    o_intra = [_bdot(scores[h], vb[:, :, vs[h]], 2, 1) for h in heads]

    def emit(h, b, o):
        cols = slice(o_col0 + h * dv, o_col0 + (h + 1) * dv)
        o_ref[b, :, cols] = (_rms_core(o) * gate[b, :, vs[h]]).astype(o_ref.dtype)

    if sequential:
        kv_t = [_bdot(vb[:, :, vs[h]], k_end[:, :, ks[h]], 1, 1) for h in heads]
        for b in range(nb):
            for h in heads:
                s_t = read_state(h)
                o_inter = lax.dot_general(q_dec[b, :, ks[h]], s_t.astype(BF16), (((1,), (1,)), ((), ())),
                                          preferred_element_type=F32)
                write_state(h, decay[b, :, ks[h]] * s_t + kv_t[h][b])
                emit(h, b, o_intra[h][b] + o_inter)
    else:
        for h in heads:
            kv = _bdot(k_end[:, :, ks[h]], vb[:, :, vs[h]], 1, 1)
            decay_col = jnp.exp(_col_bcast(g_last[:, :, ks[h]]))
            decay_col = jnp.concatenate([decay_col] * (dv // LANES), axis=-1)
            s0 = read_state(h)
            o_inter = _bdot(q_dec[:, :, ks[h]], s0.astype(BF16), 2, 1)
            write_state(h, decay_col * s0 + kv)
            emit(h, slice(None), o_intra[h] + o_inter)


def _expand_heads(x, expand):
    return _exact_bdot_rhs01(x, expand, 2, 1)


def _ssd_heads(xs, bs, cs, z, dt, a_row, d_row, norm_w, causal, sequential, read_state, write_state,
               o_ref, o_col0):
    nb, c, _ = xs.shape
    hrow = lax.broadcasted_iota(jnp.int32, (SSM_HEADS, SSM_INNER), 0)
    hcol = lax.broadcasted_iota(jnp.int32, (SSM_HEADS, SSM_INNER), 1) // SSM_HEAD_DIM
    expand2d = (hrow == hcol).astype(BF16)
    expand = jnp.broadcast_to(expand2d[None], (nb, SSM_HEADS, SSM_INNER))
    d_x = sum(_dot(p, expand2d) for p in _split3(d_row))
    dta = dt * a_row
    cum = _chunk_cumsum(dta, causal)
    cum_t = _chunk_cumsum_t(dta)
    dt_x = _expand_heads(dt, expand)
    cum_x = _expand_heads(cum, expand)
    xdt = xs * dt_x
    cum_last = cum_x[:, c - 1:c, :]
    x_end = (xdt * jnp.exp(cum_last - cum_x)).astype(BF16)
    chunk_dec = jnp.exp(cum_x)
    bsb = bs.astype(BF16)
    csb = cs.astype(BF16)
    lane_head = lax.broadcasted_iota(jnp.int32, (nb, c, SSM_GROUP_WIDTH), 2) // SSM_HEAD_DIM
    groups = range(SSM_GROUPS)
    gl = [slice(g * SSM_STATE, (g + 1) * SSM_STATE) for g in groups]
    hl = [slice(g * SSM_GROUP_WIDTH, (g + 1) * SSM_GROUP_WIDTH) for g in groups]
    cb = [_bdot(csb[:, :, gl[g]], bsb[:, :, gl[g]], 2, 2) for g in groups]
    y_intra = []
    for g in groups:
        xdt_g = xdt[:, :, hl[g]]
        y = None
        for r in range(HEADS_PER_GROUP):
            h = g * HEADS_PER_GROUP + r
            col = jnp.broadcast_to(cum[:, :, h:h + 1], (nb, c, c))
            row = cum_t[:, h:h + 1, :]
            dec = jnp.where(causal, jnp.exp(col - row), 0.0)
            lmat = (cb[g] * dec).astype(BF16)
            xm = jnp.where(lane_head == r, xdt_g, 0.0).astype(BF16)
            part = _bdot(lmat, xm, 2, 1)
            y = part if y is None else y + part
        y_intra.append(y)

    def finish(g, y_in, y_inter, b):
        y_all = y_in + y_inter * chunk_dec[b, :, hl[g]] + d_x[:, hl[g]] * xs[b, :, hl[g]]
        y_all = _rms(y_all * z[b, :, hl[g]], norm_w[:, hl[g]])
        cols = slice(o_col0 + g * SSM_GROUP_WIDTH, o_col0 + (g + 1) * SSM_GROUP_WIDTH)
        o_ref[b, :, cols] = y_all.astype(o_ref.dtype)

    if sequential:
        kv_t = [_bdot(bsb[:, :, gl[g]], x_end[:, :, hl[g]], 1, 1) for g in groups]
        for b in range(nb):
            for g in groups:
                s_t = read_state(g)
                y_inter = _dot(csb[b, :, gl[g]], s_t.astype(BF16))
                write_state(g, chunk_dec[b, c - 1:c, hl[g]] * s_t + kv_t[g][b])
                finish(g, y_intra[g][b], y_inter, b)
    else:
        for g in groups:
            kv = _bdot(x_end[:, :, hl[g]], bsb[:, :, gl[g]], 1, 1)
            decay = jnp.exp(_col_bcast(cum_last[:, :, hl[g]], SSM_STATE))
            s0 = read_state(g)
            y_inter = _bdot(csb[:, :, gl[g]], s0.astype(BF16), 2, 2)
            write_state(g, decay * s0 + kv)
            finish(g, y_intra[g], y_inter, slice(None))


def _upper(nb, c):
    r = lax.broadcasted_iota(jnp.int32, (nb, c, c), 1)
    col = lax.broadcasted_iota(jnp.int32, (nb, c, c), 2)
    return (r <= col).astype(BF16)


def _lower_bound(lb_logits):
    m = jnp.max(lb_logits, axis=0, keepdims=True)
    e = jnp.exp(lb_logits - m)
    return e[0:1, :] / jnp.sum(e, axis=0, keepdims=True)


def _ab_math(feat, conv, alog_ref, d_ref, sn_ref, sequential, read_h, write_h, read_s, write_s, o_ref):
    nb, c, _ = o_ref.shape
    causal = _causal(nb, c)
    w = HGRN_WIDTH
    _gla_heads(feat[:, :, _Q0:_Q0 + w], feat[:, :, _K0:_K0 + w], feat[:, :, _V0:_V0 + w], feat[:, :, _LF0:_LF0 + w],
               feat[:, :, _G0:_G0 + w], HGRN_HEADS, HGRN_HEAD_DIM, HGRN_HEAD_DIM, causal, sequential,
               read_h, write_h, o_ref, 0)
    act = _silu(conv)
    _ssd_heads(act[:, :, :SSM_INNER], act[:, :, SSM_INNER:SSM_INNER + SSM_GROUPS * SSM_STATE],
               act[:, :, SSM_INNER + SSM_GROUPS * SSM_STATE:], feat[:, :, _Z0:_Z0 + SSM_INNER],
               feat[:, :, _DT0:_DT0 + SSM_HEADS], -jnp.exp(alog_ref[...]), d_ref[...], sn_ref[...], causal,
               sequential, read_s, write_s, o_ref, HGRN_WIDTH)


def _ab_seq_kernel(proj_ref, cw_ref, cbias_ref, alog_ref, d_ref, sn_ref,
                   o_ref, sh_out, ss_out, sc_out, sh, ss, xpad):
    t = pl.program_id(1)
    nb, c, _ = o_ref.shape
    rows = nb * c

    @pl.when(t == 0)
    def _():
        sh[...] = jnp.zeros_like(sh)
        ss[...] = jnp.zeros_like(ss)
        xpad[0:SUBLANES, :] = jnp.zeros((SUBLANES, CONV_DIM), F32)

    xpad[SUBLANES:SUBLANES + rows, :] = proj_ref[:, :, _XBC0:_XBC0 + CONV_DIM].reshape(rows, CONV_DIM)
    padded = xpad[...]
    conv = cbias_ref[...] + padded[SUBLANES:] * cw_ref[CONV_W - 1:CONV_W, :]
    for d in range(1, CONV_W):
        conv = conv + pltpu.roll(padded, d, 0)[SUBLANES:] * cw_ref[CONV_W - 1 - d:CONV_W - d, :]
    xpad[0:SUBLANES, :] = padded[rows:rows + SUBLANES]
    conv = conv.reshape(nb, c, CONV_DIM)

    def read_h(h):
        return sh[h]

    def write_h(h, s):
        sh[h] = s

    def read_s(g):
        return ss[g]

    def write_s(g, s):
        ss[g] = s

    _ab_math(proj_ref, conv, alog_ref, d_ref, sn_ref, True, read_h, write_h, read_s, write_s, o_ref)

    @pl.when(t == pl.num_programs(1) - 1)
    def _():
        for h in range(HGRN_HEADS):
            sh_out[0, h] = sh[h].T
        for g in range(SSM_GROUPS):
            ss_out[0, g] = ss[g].T
        sc_out[0] = xpad[SUBLANES - (CONV_W - 1):SUBLANES, :]


def _ab_par_kernel(proj_ref, sh_in, ss_in, sc_in, cw_ref, cbias_ref, alog_ref, d_ref, sn_ref,
                   o_ref, sh_out, ss_out, sc_out, xpad, o3):
    nb, c, _ = o3.shape
    nbuf = CONV_W - 1
    proj = proj_ref[...].reshape(nb, c, proj_ref.shape[-1])
    xpad[:, 0:nbuf, :] = sc_in[...]
    xpad[:, nbuf:nbuf + c, :] = proj[:, :, _XBC0:_XBC0 + CONV_DIM]
    conv = cbias_ref[...]
    for k in range(CONV_W):
        conv = conv + xpad[:, k:k + c, :] * cw_ref[k:k + 1, :]
    sc_out[...] = xpad[:, c:c + nbuf, :]

    def read_h(h):
        return sh_in[:, h]

    def write_h(h, s):
        sh_out[:, h] = s

    def read_s(g):
        return ss_in[:, g]

    def write_s(g, s):
        ss_out[:, g] = s

    _ab_math(proj, conv, alog_ref, d_ref, sn_ref, False, read_h, write_h, read_s, write_s, o3)
    o_ref[...] = o3[...].reshape(o_ref.shape)


def _gla_seq_kernel(ready_ref, decay_ref, gate_ref, v_ref, o_ref, sg_out, sg):
    t = pl.program_id(1)

    @pl.when(t == 0)
    def _():
        sg[...] = jnp.zeros_like(sg)

    def read_g(h):
        return sg[h]

    def write_g(h, s):
        sg[h] = s

    nb, c, _ = o_ref.shape
    _gla_core(ready_ref[:, :, _GQD0:_GQD0 + GLA_KEY], ready_ref[:, :, _GKI0:_GKI0 + GLA_KEY],
              ready_ref[:, :, _GKE0:_GKE0 + GLA_KEY], decay_ref[...].reshape(nb, 1, GLA_KEY), None, v_ref[...],
              gate_ref, GLA_HEADS, GLA_HEAD_K, GLA_HEAD_V, _causal(nb, c), True, read_g, write_g, o_ref, 0)

    @pl.when(t == pl.num_programs(1) - 1)
    def _():
        for h in range(GLA_HEADS):
            sg_out[0, h] = sg[h].T


def _gla_par_kernel(proj_ref, gate_ref, v_ref, sg_in, o_ref, sg_out, o3):
    def read_g(h):
        return sg_in[:, h]

    def write_g(h, s):
        sg_out[:, h] = s

    nb, c, _ = o3.shape
    feat = proj_ref[...].reshape(nb, c, proj_ref.shape[-1])
    gate = gate_ref[...].reshape(nb, c, gate_ref.shape[-1])
    v = v_ref[...].astype(F32).reshape(nb, c, v_ref.shape[-1])
    _gla_heads(feat[:, :, _GQ0:_GQ0 + GLA_KEY], feat[:, :, _GK0:_GK0 + GLA_KEY], v,
               feat[:, :, _GLF0:_GLF0 + GLA_KEY], gate, GLA_HEADS, GLA_HEAD_K, GLA_HEAD_V, _causal(nb, c), False,
               read_g, write_g, o3, 0)
    o_ref[...] = o3[...].reshape(o_ref.shape)


SEQ_TILE_CHUNKS = 8
GLA_SEQ_TILE_CHUNKS = 8
PAR_TILE_SEQS = 8


def _full(shape):
    return pl.BlockSpec(shape, lambda *_: (0,) * len(shape))


def _mixer_o_dtype(c):
    return BF16 if c % (2 * SUBLANES) == 0 else F32


def _ab_seq_call(proj, batch, length, params, *, name):
    c = math.gcd(length, CHUNK)
    nb = SEQ_TILE_CHUNKS
    tiles = length // (c * nb)
    assert tiles * c * nb == length
    width = HEADS_PER_GROUP * SSM_HEAD_DIM
    blk = lambda n: pl.BlockSpec((nb, c, n), lambda b, t: (b * tiles + t, 0, 0))
    out_shapes = [
        jax.ShapeDtypeStruct((batch * length // c, c, AB_WIDTH), _mixer_o_dtype(c)),
        jax.ShapeDtypeStruct((batch, HGRN_HEADS, HGRN_HEAD_DIM, HGRN_HEAD_DIM), F32),
        jax.ShapeDtypeStruct((batch, SSM_GROUPS, width, SSM_STATE), F32),
        jax.ShapeDtypeStruct((batch, CONV_W - 1, CONV_DIM), F32),
    ]
    out_specs = [
        blk(AB_WIDTH),
        pl.BlockSpec((1, HGRN_HEADS, HGRN_HEAD_DIM, HGRN_HEAD_DIM), lambda b, t: (b, 0, 0, 0)),
        pl.BlockSpec((1, SSM_GROUPS, width, SSM_STATE), lambda b, t: (b, 0, 0, 0)),
        pl.BlockSpec((1, CONV_W - 1, CONV_DIM), lambda b, t: (b, 0, 0)),
    ]
    return pl.pallas_call(
        _ab_seq_kernel,
        grid=(batch, tiles),
        in_specs=[blk(AB_FEAT)] + [_full(p.shape) for p in params],
        out_specs=out_specs,
        out_shape=out_shapes,
        scratch_shapes=[
            pltpu.VMEM((HGRN_HEADS, HGRN_HEAD_DIM, HGRN_HEAD_DIM), F32),
            pltpu.VMEM((SSM_GROUPS, SSM_STATE, width), F32),
            pltpu.VMEM((nb * c + SUBLANES, CONV_DIM), F32),
        ],
        compiler_params=pltpu.CompilerParams(dimension_semantics=("arbitrary", "arbitrary"),
                                             vmem_limit_bytes=VMEM_LIMIT_BYTES),
        name=name,
    )(proj, *params)


def _ab_par_call(proj, row0, c, s_hgrn, s_ssm, s_conv, params, *, name):
    batch = s_hgrn.shape[0]
    nb = _row_tile(batch, PAR_TILE_SEQS)
    rows = nb * c
    assert row0 % rows == 0
    width = HEADS_PER_GROUP * SSM_HEAD_DIM
    s_ssm = s_ssm.reshape(batch, SSM_GROUPS, width, SSM_STATE)
    blk3 = lambda a, n: pl.BlockSpec((nb, a, n), lambda b: (b, 0, 0))
    blk4 = lambda a, r, n: pl.BlockSpec((nb, a, r, n), lambda b: (b, 0, 0, 0))
    state_specs = [blk4(HGRN_HEADS, HGRN_HEAD_DIM, HGRN_HEAD_DIM), blk4(SSM_GROUPS, width, SSM_STATE),
                   blk3(CONV_W - 1, CONV_DIM)]
    out_shapes = [
        jax.ShapeDtypeStruct((batch * c, AB_WIDTH), F32),
        jax.ShapeDtypeStruct(s_hgrn.shape, F32),
        jax.ShapeDtypeStruct(s_ssm.shape, F32),
        jax.ShapeDtypeStruct(s_conv.shape, F32),
    ]
    return pl.pallas_call(
        _ab_par_kernel,
        grid=(batch // nb,),
        in_specs=[pl.BlockSpec((rows, proj.shape[1]), lambda b: (row0 // rows + b, 0))] + state_specs
                 + [_full(p.shape) for p in params],
        out_specs=[pl.BlockSpec((rows, AB_WIDTH), lambda b: (b, 0))] + state_specs,
        out_shape=out_shapes,
        scratch_shapes=[pltpu.VMEM((nb, c + CONV_W - 1, CONV_DIM), F32), pltpu.VMEM((nb, c, AB_WIDTH), F32)],
        compiler_params=pltpu.CompilerParams(dimension_semantics=("arbitrary",),
                                             vmem_limit_bytes=VMEM_LIMIT_BYTES),
        name=name,
    )(proj, s_hgrn, s_ssm, s_conv, *params)


def _gla_seq_call(ready, decay, gate, v, batch, length, *, name):
    c = math.gcd(length, CHUNK)
    assert c == CHUNK
    nb = GLA_SEQ_TILE_CHUNKS
    tiles = length // (c * nb)
    assert tiles * c * nb == length
    blk = lambda n: pl.BlockSpec((nb, c, n), lambda b, t: (b * tiles + t, 0, 0))
    return pl.pallas_call(
        _gla_seq_kernel,
        grid=(batch, tiles),
        in_specs=[blk(GLA_READY), pl.BlockSpec((nb, GLA_KEY), lambda b, t: (b * tiles + t, 0)), blk(GLA_VAL),
                  blk(GLA_VAL)],
        out_specs=[blk(GLA_VAL),
                   pl.BlockSpec((1, GLA_HEADS, GLA_HEAD_K, GLA_HEAD_V), lambda b, t: (b, 0, 0, 0))],
        out_shape=[jax.ShapeDtypeStruct((batch * length // c, c, GLA_VAL), _mixer_o_dtype(c)),
                   jax.ShapeDtypeStruct((batch, GLA_HEADS, GLA_HEAD_K, GLA_HEAD_V), F32)],
        scratch_shapes=[pltpu.VMEM((GLA_HEADS, GLA_HEAD_V, GLA_HEAD_K), F32)],
        compiler_params=pltpu.CompilerParams(dimension_semantics=("arbitrary", "arbitrary"),
                                             vmem_limit_bytes=VMEM_LIMIT_BYTES),
        name=name,
    )(ready, decay, gate, v)


def _gla_par_call(proj, gate, v, row0, c, s_gla, *, name):
    batch = s_gla.shape[0]
    nb = _row_tile(batch, PAR_TILE_SEQS)
    rows = nb * c
    assert row0 % rows == 0
    sspec = pl.BlockSpec((nb, GLA_HEADS, GLA_HEAD_K, GLA_HEAD_V), lambda b: (b, 0, 0, 0))
    return pl.pallas_call(
        _gla_par_kernel,
        grid=(batch // nb,),
        in_specs=[pl.BlockSpec((rows, a.shape[1]), lambda b: (row0 // rows + b, 0)) for a in (proj, gate, v)] + [sspec],
        out_specs=[pl.BlockSpec((rows, GLA_VAL), lambda b: (b, 0)), sspec],
        out_shape=[jax.ShapeDtypeStruct((batch * c, GLA_VAL), F32), jax.ShapeDtypeStruct(s_gla.shape, F32)],
        scratch_shapes=[pltpu.VMEM((nb, c, GLA_VAL), F32)],
        compiler_params=pltpu.CompilerParams(dimension_semantics=("arbitrary",),
                                             vmem_limit_bytes=VMEM_LIMIT_BYTES),
        name=name,
    )(proj, gate, v, s_gla)


def _split_cols(w):
    main = w.shape[1] // LANES * LANES
    tail = jnp.pad(w[:, main:].astype(BF16), ((0, 0), (0, LANES - (w.shape[1] - main))))
    return w[:, :main].astype(BF16), tail


def kernel(x_prompt, x_sample, state_hgrn, state_ssm, state_conv, state_gla, norm_ffn1, norm_mix, norm_ffn2, norm_final, ffn1_w_in, ffn1_w_out, ffn2_w_in, ffn2_w_out, ab_w_in, ab_w_out, hgrn_lb_logits, hgrn_norm, ssm_conv_w, ssm_conv_b, ssm_dt_bias, ssm_a_log, ssm_d, ssm_norm, gla_w_in, gla_w_gk, gla_b_gk, gla_norm, gla_w_out):
    assert ab_w_in.shape[0] == 1 and gla_w_in.shape[0] == 1, "one HGRN2/SSD layer and one GLA layer"
    bp, lp, _ = x_prompt.shape
    bs, ls, _ = x_sample.shape
    rp, rs = bp * lp, bs * ls
    cp, cs = math.gcd(lp, CHUNK), math.gcd(ls, CHUNK)
    assert rs % cp == 0
    row = lambda v: v.reshape(1, -1)
    stack_row = lambda v: v.reshape(v.shape[0], 1, v.shape[1])
    nf1, nm, nf2 = stack_row(norm_ffn1), stack_row(norm_mix), stack_row(norm_ffn2)
    in_rows, out_rows = D_MODEL // 64, D_FF // 16
    xp = x_prompt.reshape(rp, D_MODEL)
    xs = x_sample.reshape(rs, D_MODEL)

    wp, wt = _split_cols(ab_w_in[0])
    ab_feat_params = (hgrn_lb_logits, jnp.tile(hgrn_norm[0], HGRN_HEADS).reshape(1, -1),
                      jnp.pad(ssm_dt_bias[0], (0, LANES - SSM_HEADS)).reshape(1, -1))
    casts = (_Cast(ffn2_w_in, in_rows), _Cast(ffn2_w_out, out_rows), _Cast(ffn1_w_in, in_rows, layer=1),
             _Cast(ffn1_w_out, out_rows, layer=1), _Cast(ab_w_out, in_rows, layer=0),
             _Cast(gla_w_in[0], in_rows, split=True), _Cast(gla_w_out, in_rows, layer=0))
    (x1, feat, w2_in, w2_out, w1_in, w1_out, ab_wo, gla_wp, gla_wt, gla_wo) = _pre_call(
        xp, xs, 0, nf1, ffn1_w_in[0].astype(BF16), ffn1_w_out[0].astype(BF16), nm, wp, wt, ab_feat_params, casts,
        mode="ab", row_tile=PRE0_ROWS, name="pre0")
    ab_params = (ssm_conv_w[0], row(ssm_conv_b[0]), row(ssm_a_log[0]), row(ssm_d[0]), row(ssm_norm[0]))
    o_p, hgrn_p, ssm_p, conv_p = _ab_seq_call(feat.reshape((rp + rs) // cp, cp, AB_FEAT), bp, lp, ab_params,
                                              name="mix0_prompt")
    o_s, hgrn_s, ssm_s, conv_s = _ab_par_call(feat, rp, cs, state_hgrn[0], state_ssm[0], state_conv[0], ab_params,
                                              name="mix0_sample")
    (x3,) = _post_call(x1, o_p.reshape(rp, AB_WIDTH), o_s, 0, ab_wo, nf2,
                       w2_in, w2_out, row(norm_final), split_output=False, name="post0")

    gla_feat_params = (gla_w_gk[0].astype(BF16), row(gla_b_gk[0]), jnp.tile(gla_norm[0], GLA_HEADS).reshape(1, -1))
    x4, feat, gate, v16, ready, decay = _pre_call(x3, None, 1, nf1, w1_in, w1_out, nm, gla_wp, gla_wt, gla_feat_params,
                                                  mode="gla", row_tile=PRE1_ROWS, name="pre1")
    chunked = lambda a: a.reshape((rp + rs) // cp, cp, a.shape[1])
    o_p, gla_p = _gla_seq_call(chunked(ready), decay, chunked(gate), chunked(v16), bp, lp, name="mix1_prompt")
    o_s, gla_s = _gla_par_call(feat, gate, v16, rp, cs, state_gla[0], name="mix1_sample")
    y_p, y_s = _post_call(x4, o_p.reshape(rp, GLA_VAL), o_s, 1, gla_wo, nf2,
                          w2_in, w2_out, row(norm_final), split_output=True, name="post1")

    ssm_shape = (1, -1, SSM_HEADS, SSM_HEAD_DIM, SSM_STATE)
    return (y_p.reshape(bp, lp, D_MODEL), y_s.reshape(bs, ls, D_MODEL), hgrn_p[None], hgrn_s[None],
            ssm_p.reshape(ssm_shape), ssm_s.reshape(ssm_shape), conv_p[None], conv_s[None], gla_p[None], gla_s[None])
```

```python
import functools
import math

import jax
import jax.numpy as jnp
from jax import lax
from jax.experimental import pallas as pl
from jax.experimental.pallas import tpu as pltpu

F32 = jnp.float32
BF16 = jnp.bfloat16

D_MODEL = 1024
D_FF = 2816
EPS = 1e-6
CHUNK = 64

HGRN_HEADS = 4
HGRN_HEAD_DIM = 128
HGRN_WIDTH = HGRN_HEADS * HGRN_HEAD_DIM

SSM_HEADS = 8
SSM_HEAD_DIM = 64
SSM_INNER = SSM_HEADS * SSM_HEAD_DIM
SSM_GROUPS = 2
SSM_STATE = 128
SSM_GROUP_WIDTH = SSM_INNER // SSM_GROUPS
HEADS_PER_GROUP = SSM_HEADS // SSM_GROUPS
CONV_W = 4
CONV_DIM = SSM_INNER + 2 * SSM_GROUPS * SSM_STATE
AB_WIDTH = HGRN_WIDTH + SSM_INNER

GLA_HEADS = 4
GLA_HEAD_K = 128
GLA_HEAD_V = 256
GLA_KEY = GLA_HEADS * GLA_HEAD_K
GLA_VAL = GLA_HEADS * GLA_HEAD_V
GK_RANK = 16
GK_NORMALIZER = 16.0

LANES = 128
SUBLANES = 8
VMEM_LIMIT_BYTES = 56 * 1024 * 1024

_Q0, _K0, _LF0, _V0, _G0 = (i * HGRN_WIDTH for i in range(5))
_Z0 = 5 * HGRN_WIDTH
_XBC0 = _Z0 + SSM_INNER
_DT0 = _XBC0 + CONV_DIM
AB_FEAT = _DT0 + LANES
_WQ, _WF, _WI, _WG, _WZ = (i * HGRN_WIDTH for i in range(5))
_WXBC = 4 * HGRN_WIDTH + SSM_INNER

_GQ0, _GK0, _GLF0 = 0, GLA_KEY, 2 * GLA_KEY
GLA_FEAT = 3 * GLA_KEY
_GQD0, _GKI0, _GKE0 = 0, GLA_KEY, 2 * GLA_KEY
GLA_READY = 3 * GLA_KEY
_GWQ, _GWK = 0, GLA_KEY
_GWV = 2 * GLA_KEY
_GWG = _GWV + GLA_VAL


def _rms(x, w):
    return x * lax.rsqrt(jnp.mean(x * x, axis=-1, keepdims=True) + EPS) * w


def _rms_core(x):
    return x * lax.rsqrt(jnp.mean(x * x, axis=-1, keepdims=True) + EPS)


def _silu(x):
    return x * jax.nn.sigmoid(x)


def _softplus(x):
    return jnp.maximum(x, 0.0) + jnp.log1p(jnp.exp(-jnp.abs(x)))


def _dot(a, b):
    return jnp.dot(a, b, preferred_element_type=F32)


def _bdot(a, b, ca, cb):
    return lax.dot_general(a, b, (((ca,), (cb,)), ((0,), (0,))), preferred_element_type=F32)


def _split3(x):
    hi = x.astype(BF16)
    r1 = x - hi.astype(F32)
    mid = r1.astype(BF16)
    lo = (r1 - mid.astype(F32)).astype(BF16)
    return hi, mid, lo


def _exact_bdot_lhs01(m01, x, ca, cb):
    return sum(_bdot(m01, p, ca, cb) for p in _split3(x))


def _exact_bdot_rhs01(x, m01, ca, cb):
    return sum(_bdot(p, m01, ca, cb) for p in _split3(x))


def _causal(nb, c):
    r = lax.broadcasted_iota(jnp.int32, (nb, c, c), 1)
    col = lax.broadcasted_iota(jnp.int32, (nb, c, c), 2)
    return r >= col


BF16_ROWS = 2 * SUBLANES


def _triangle3(nb, c, wide_axis):
    shape = (1, c, 3 * c) if wide_axis == 2 else (1, 3 * c, c)
    wide = lax.broadcasted_iota(jnp.int32, shape, wide_axis)
    narrow = lax.broadcasted_iota(jnp.int32, shape, 3 - wide_axis)
    hit = None
    for k in range(3):
        wk = wide - k * c
        term = (wk >= 0) & (wk < c) & (wk <= narrow)
        hit = term if hit is None else hit | term
    return jnp.broadcast_to(hit.astype(BF16), (nb,) + shape[1:])


def _chunk_cumsum(x, causal):
    nb, c, _ = x.shape
    if c % BF16_ROWS:
        return _exact_bdot_lhs01(causal.astype(BF16), x, 2, 1)
    return _bdot(_triangle3(nb, c, 2), jnp.concatenate(_split3(x), axis=1), 2, 1)


def _chunk_cumsum_t(x):
    nb, c, _ = x.shape
    if c % BF16_ROWS:
        return _exact_bdot_rhs01(x, _upper(nb, c), 1, 1)
    return _bdot(jnp.concatenate(_split3(x), axis=1), _triangle3(nb, c, 1), 1, 1)


def _col_bcast(row, lanes=LANES):
    nb, _, k = row.shape
    hi, mid, lo = (p.astype(F32) for p in _split3(row))
    r = lax.broadcasted_iota(jnp.int32, (1, BF16_ROWS, k), 1)
    stacked = jnp.where(r == 0, hi, jnp.where(r == 1, mid, jnp.where(r == 2, lo, 0.0))).astype(BF16)
    return _bdot(stacked, jnp.ones((nb, BF16_ROWS, lanes), BF16), 1, 1)


MXU_DIM = 256
FF_TILES = ((0, 6 * MXU_DIM), (6 * MXU_DIM, D_FF))
assert all(lo % MXU_DIM == 0 and hi % MXU_DIM == 0 for lo, hi in FF_TILES)


def _ffn_half(x, norm_w, w_in_ref, w_out_ref):
    hb = _rms(x, norm_w).astype(BF16)
    acc = None
    for lo, hi in FF_TILES:
        gate = _dot(hb, w_in_ref[:, lo:hi])
        up = _dot(hb, w_in_ref[:, D_FF + lo:D_FF + hi])
        act = (_silu(gate) * up).astype(BF16)
        part = _dot(act, w_out_ref[lo:hi, :])
        acc = part if acc is None else acc + part
    return 0.5 * acc


def _pick_group(first_steps, a, b):
    return jnp.where(pl.program_id(0) < first_steps, a, b)


def _cast_rows(src_ref, *dst_refs):
    if len(dst_refs) == 1:
        dst_refs[0][...] = src_ref[...].astype(BF16)
        return
    main_ref, tail_ref = dst_refs
    main = main_ref.shape[-1]
    rest = src_ref.shape[-1] - main
    main_ref[...] = src_ref[:, :main].astype(BF16)
    tail_ref[...] = jnp.zeros(tail_ref.shape, BF16)
    tail_ref[:, :rest] = src_ref[:, main:].astype(BF16)


def _ab_features(hb, wp_ref, wt_ref, lb_ref, hn_ref, dtb_ref, feat_ref):
    def group(w0):
        return _dot(hb, wp_ref[:, w0:w0 + HGRN_WIDTH])

    def put(c0, val):
        feat_ref[:, c0:c0 + val.shape[-1]] = val

    lb = _lower_bound(lb_ref[...])
    f = lb + (1.0 - lb) * jax.nn.sigmoid(group(_WF))
    put(_K0, 1.0 - f)
    put(_LF0, jnp.log(f))
    put(_Q0, _silu(group(_WQ)))
    put(_G0, _silu(group(_WG)) * hn_ref[...])
    put(_Z0, _silu(group(_WZ)))
    put(_DT0, _softplus(_dot(hb, wt_ref[...]) + dtb_ref[...]))
    put(_XBC0, _dot(hb, wp_ref[:, _WXBC:_WXBC + CONV_DIM]))
    put(_V0, group(_WI))


def _decayed_operands(q, k, log_f):
    nb, c, _ = q.shape
    g = _chunk_cumsum(log_f, _causal(nb, c))
    g_last = g[:, c - 1:c, :]
    q_dec = (q * jnp.exp(g)).astype(BF16)
    k_inv_f = k * jnp.exp(-g)
    decay = jnp.exp(g_last)
    k_end = (k_inv_f * decay).astype(BF16)
    return q_dec, k_inv_f.astype(BF16), k_end, decay, g_last


def _gla_features(hb, wp_ref, wt_ref, wgk_ref, bgk_ref, gn_ref, feat_ref, gate_ref, v_ref, ready_ref, decay_ref):
    def put(c0, val):
        feat_ref[:, c0:c0 + val.shape[-1]] = val

    rows = hb.shape[0]
    nb = rows // CHUNK
    gk_low = _dot(hb, wt_ref[...])[:, :GK_RANK].astype(BF16)
    gk = _dot(gk_low, wgk_ref[...]) + bgk_ref[...]
    log_f = -_softplus(-gk) / GK_NORMALIZER
    q = _dot(hb, wp_ref[:, _GWQ:_GWQ + GLA_KEY]) * (GLA_HEAD_K ** -0.5)
    k = _dot(hb, wp_ref[:, _GWK:_GWK + GLA_KEY])
    put(_GLF0, log_f)
    put(_GQ0, q)
    put(_GK0, k)
    chunked = lambda t: t.reshape(nb, CHUNK, GLA_KEY)
    q_dec, k_inv, k_end, decay, _ = _decayed_operands(chunked(q), chunked(k), chunked(log_f))
    for c0, val in ((_GQD0, q_dec), (_GKI0, k_inv), (_GKE0, k_end)):
        ready_ref[:, c0:c0 + GLA_KEY] = val.reshape(rows, GLA_KEY)
    decay_ref[...] = decay.reshape(nb, GLA_KEY)
    gate_ref[...] = (_silu(_dot(hb, wp_ref[:, _GWG:_GWG + GLA_VAL])) * gn_ref[...]).astype(BF16)
    v_ref[...] = _dot(hb, wp_ref[:, _GWV:_GWV + GLA_VAL]).astype(BF16)


_FEATURES = {
    "ab": (_ab_features, ((AB_FEAT, F32, 1),)),
    "gla": (_gla_features, ((GLA_FEAT, F32, 1), (GLA_VAL, BF16, 1), (GLA_VAL, BF16, 1), (GLA_READY, BF16, 1),
                            (GLA_KEY, F32, CHUNK))),
}
N_FEATURE_PARAMS = 3


def _pre_kernel(*refs, first_steps, cast_arity, mode):
    n_x = 1 if first_steps is None else 2
    x_refs, refs = refs[:n_x], refs[n_x:]
    nf_ref, w_in_ref, w_out_ref, nm_ref, wp_ref, wt_ref = refs[:6]
    feat_params, refs = refs[6:6 + N_FEATURE_PARAMS], refs[6 + N_FEATURE_PARAMS:]
    feature_fn, feature_outs = _FEATURES[mode]
    cast_src, refs = refs[:len(cast_arity)], refs[len(cast_arity):]
    x1_ref, feat_refs, cast_dst = refs[0], refs[1:1 + len(feature_outs)], list(refs[1 + len(feature_outs):])
    if first_steps is None:
        x = x_refs[0][...]
    else:
        x = _pick_group(first_steps, x_refs[0][...], x_refs[1][...])
    x1 = x + _ffn_half(x, nf_ref[...], w_in_ref, w_out_ref)
    x1_ref[...] = x1
    hb = _rms(x1, nm_ref[...]).astype(BF16)
    feature_fn(hb, wp_ref, wt_ref, *feat_params, *feat_refs)
    for src_ref, arity in zip(cast_src, cast_arity):
        _cast_rows(src_ref, *cast_dst[:arity])
        cast_dst = cast_dst[arity:]


def _post_kernel(x_ref, oa_ref, ob_ref, wo_ref, nf_ref, w_in_ref, w_out_ref, nfin_ref, *y_refs, first_steps):
    o = _pick_group(first_steps, oa_ref[...].astype(BF16), ob_ref[...].astype(BF16))
    x2 = x_ref[...] + _dot(o, wo_ref[...])
    y = x2 + _ffn_half(x2, nf_ref[...], w_in_ref, w_out_ref)
    if len(y_refs) == 1:
        y_refs[0][...] = y
    else:
        y = _rms(y, nfin_ref[...])
        ya_ref, yb_ref = y_refs

        @pl.when(pl.program_id(0) < first_steps)
        def _():
            ya_ref[...] = y

        @pl.when(pl.program_id(0) >= first_steps)
        def _():
            yb_ref[...] = y


def _resident(shape, layer=None):
    if layer is None:
        return pl.BlockSpec(shape, lambda *_: (0,) * len(shape), pipeline_mode=pl.Buffered(1))
    return pl.BlockSpec((None,) + tuple(shape[1:]), lambda *_: (layer,) + (0,) * (len(shape) - 1),
                        pipeline_mode=pl.Buffered(1))


def _row_tile(rows, want):
    t = min(rows, want)
    assert rows % t == 0
    return t


PRE0_ROWS = 256
PRE1_ROWS = 512
POST_ROWS = 512


def _group_specs(tm, first_steps, width):
    first = pl.BlockSpec((tm, width), lambda i: (jnp.minimum(i, first_steps - 1), 0))
    second = pl.BlockSpec((tm, width), lambda i: (jnp.maximum(i - first_steps, 0), 0))
    return first, second


def _weight_spec(w, layer):
    return _resident(w.shape, layer if w.ndim == 3 else None)


class _Cast:
    def __init__(self, src, rows_per_step, layer=None, split=False):
        self.src, self.rps, self.layer, self.split = src, rows_per_step, layer, split
        rows = src.shape[-2]
        assert rows % rows_per_step == 0 and rows_per_step % (2 * SUBLANES) == 0
        self.steps = rows // rows_per_step
        assert not split or layer is not None or src.ndim == 2

    def _index(self, lead):
        last = self.steps - 1
        return lambda i: lead + (jnp.minimum(i, last), 0)

    def in_spec(self):
        cols = self.src.shape[-1]
        if self.src.ndim == 2:
            return pl.BlockSpec((self.rps, cols), self._index(()))
        if self.layer is None:
            return pl.BlockSpec((self.src.shape[0], self.rps, cols), self._index((0,)))
        return pl.BlockSpec((None, self.rps, cols), self._index((self.layer,)))

    def outs(self):
        rows, cols = self.src.shape[-2:]
        if self.src.ndim == 3 and self.layer is None:
            n = self.src.shape[0]
            return [(jax.ShapeDtypeStruct((n, rows, cols), BF16), pl.BlockSpec((n, self.rps, cols), self._index((0,))))]
        widths = [cols // LANES * LANES, LANES] if self.split else [cols]
        return [(jax.ShapeDtypeStruct((rows, w), BF16), pl.BlockSpec((self.rps, w), self._index(()))) for w in widths]


def _pre_call(xa, xb, layer, nf, w_in, w_out, nm, wp, wt, feat_params, casts=(), *, mode, row_tile, name):
    ra = xa.shape[0]
    rb = 0 if xb is None else xb.shape[0]
    rows = ra + rb
    tm = _row_tile(rb if rb else ra, row_tile)
    assert ra % tm == 0
    assert all(cast.steps <= rows // tm for cast in casts)
    assert len(feat_params) == N_FEATURE_PARAMS
    feature_outs = _FEATURES[mode][1]
    tok = lambda n: pl.BlockSpec((tm, n), lambda i: (i, 0))
    if xb is None:
        first_steps, x_specs, xs = None, [tok(D_MODEL)], (xa,)
    else:
        first_steps = ra // tm
        x_specs, xs = list(_group_specs(tm, first_steps, D_MODEL)), (xa, xb)
    cast_outs = [cast.outs() for cast in casts]
    flat_outs = [o for outs in cast_outs for o in outs]
    return pl.pallas_call(
        functools.partial(_pre_kernel, first_steps=first_steps, cast_arity=tuple(len(o) for o in cast_outs),
                          mode=mode),
        grid=(rows // tm,),
        in_specs=x_specs + [_resident(nf.shape, layer), _weight_spec(w_in, layer), _weight_spec(w_out, layer),
                            _resident(nm.shape, layer), _resident(wp.shape), _resident(wt.shape)]
                         + [_resident(p.shape) for p in feat_params] + [cast.in_spec() for cast in casts],
        out_specs=[tok(D_MODEL)] + [pl.BlockSpec((tm // div, w), lambda i: (i, 0)) for w, _, div in feature_outs]
                  + [spec for _, spec in flat_outs],
        out_shape=[jax.ShapeDtypeStruct((rows, D_MODEL), F32)]
                  + [jax.ShapeDtypeStruct((rows // div, w), dt) for w, dt, div in feature_outs]
                  + [shape for shape, _ in flat_outs],
        compiler_params=pltpu.CompilerParams(dimension_semantics=("arbitrary",), vmem_limit_bytes=VMEM_LIMIT_BYTES),
        name=name,
    )(*xs, nf, w_in, w_out, nm, wp, wt, *feat_params, *[cast.src for cast in casts])


def _post_call(x, oa, ob, layer, wo, nf, w_in, w_out, nfin, *, split_output, name):
    rows = x.shape[0]
    ra, rb = oa.shape[0], ob.shape[0]
    assert ra + rb == rows
    tm = _row_tile(rb, POST_ROWS)
    assert ra % tm == 0
    first_steps = ra // tm
    tok = lambda n: pl.BlockSpec((tm, n), lambda i: (i, 0))
    spec_a, spec_b = _group_specs(tm, first_steps, oa.shape[1])
    if split_output:
        out_specs = list(_group_specs(tm, first_steps, D_MODEL))
        out_shape = [jax.ShapeDtypeStruct((ra, D_MODEL), F32), jax.ShapeDtypeStruct((rb, D_MODEL), F32)]
    else:
        out_specs = [tok(D_MODEL)]
        out_shape = [jax.ShapeDtypeStruct((rows, D_MODEL), F32)]
    return pl.pallas_call(
        functools.partial(_post_kernel, first_steps=first_steps),
        grid=(rows // tm,),
        in_specs=[tok(D_MODEL), spec_a, spec_b, _resident(wo.shape), _resident(nf.shape, layer),
                  _resident(w_in.shape, layer), _resident(w_out.shape, layer), _resident(nfin.shape)],
        out_specs=out_specs,
        out_shape=out_shape,
        compiler_params=pltpu.CompilerParams(dimension_semantics=("arbitrary",), vmem_limit_bytes=VMEM_LIMIT_BYTES),
        name=name,
    )(x, oa, ob, wo, nf, w_in, w_out, nfin)


def _gla_heads(q, k, v, log_f, gate, n_heads, dk, dv, causal, sequential, read_state, write_state,
               o_ref, o_col0):
    q_dec, k_inv, k_end, decay, g_last = _decayed_operands(q, k, log_f)
    _gla_core(q_dec, k_inv, k_end, decay, g_last, v, gate, n_heads, dk, dv, causal, sequential, read_state,
              write_state, o_ref, o_col0)


def _gla_core(q_dec, k_inv, k_end, decay, g_last, v, gate, n_heads, dk, dv, causal, sequential, read_state,
              write_state, o_ref, o_col0):
    nb, c, _ = q_dec.shape
    vb = v.astype(BF16)
    heads = range(n_heads)
    ks = [slice(h * dk, (h + 1) * dk) for h in heads]
    vs = [slice(h * dv, (h + 1) * dv) for h in heads]
    scores = [_bdot(q_dec[:, :, ks[h]], k_inv[:, :, ks[h]], 2, 2) for h in heads]
    scores = [jnp.where(causal, sc, 0.0).astype(BF16) for sc in scores]
    o_intra = [_bdot(scores[h], vb[:, :, vs[h]], 2, 1) for h in heads]

    def emit(h, b, o):
        cols = slice(o_col0 + h * dv, o_col0 + (h + 1) * dv)
        o_ref[b, :, cols] = (_rms_core(o) * gate[b, :, vs[h]]).astype(o_ref.dtype)

    if sequential:
        kv_t = [_bdot(vb[:, :, vs[h]], k_end[:, :, ks[h]], 1, 1) for h in heads]
        for b in range(nb):
            for h in heads:
                s_t = read_state(h)
                o_inter = lax.dot_general(q_dec[b, :, ks[h]], s_t.astype(BF16), (((1,), (1,)), ((), ())),
                                          preferred_element_type=F32)
                write_state(h, decay[b, :, ks[h]] * s_t + kv_t[h][b])
                emit(h, b, o_intra[h][b] + o_inter)
    else:
        for h in heads:
            kv = _bdot(k_end[:, :, ks[h]], vb[:, :, vs[h]], 1, 1)
            decay_col = jnp.exp(_col_bcast(g_last[:, :, ks[h]]))
            decay_col = jnp.concatenate([decay_col] * (dv // LANES), axis=-1)
            s0 = read_state(h)
            o_inter = _bdot(q_dec[:, :, ks[h]], s0.astype(BF16), 2, 1)
            write_state(h, decay_col * s0 + kv)
            emit(h, slice(None), o_intra[h] + o_inter)


def _expand_heads(x, expand):
    return _exact_bdot_rhs01(x, expand, 2, 1)


def _ssd_heads(xs, bs, cs, z, dt, a_row, d_row, norm_w, causal, sequential, read_state, write_state,
               o_ref, o_col0):
    nb, c, _ = xs.shape
    hrow = lax.broadcasted_iota(jnp.int32, (SSM_HEADS, SSM_INNER), 0)
    hcol = lax.broadcasted_iota(jnp.int32, (SSM_HEADS, SSM_INNER), 1) // SSM_HEAD_DIM
    expand2d = (hrow == hcol).astype(BF16)
    expand = jnp.broadcast_to(expand2d[None], (nb, SSM_HEADS, SSM_INNER))
    d_x = sum(_dot(p, expand2d) for p in _split3(d_row))
    dta = dt * a_row
    cum = _chunk_cumsum(dta, causal)
    cum_t = _chunk_cumsum_t(dta)
    dt_x = _expand_heads(dt, expand)
    cum_x = _expand_heads(cum, expand)
    xdt = xs * dt_x
    cum_last = cum_x[:, c - 1:c, :]
    x_end = (xdt * jnp.exp(cum_last - cum_x)).astype(BF16)
    chunk_dec = jnp.exp(cum_x)
    bsb = bs.astype(BF16)
    csb = cs.astype(BF16)
    lane_head = lax.broadcasted_iota(jnp.int32, (nb, c, SSM_GROUP_WIDTH), 2) // SSM_HEAD_DIM
    groups = range(SSM_GROUPS)
    gl = [slice(g * SSM_STATE, (g + 1) * SSM_STATE) for g in groups]
    hl = [slice(g * SSM_GROUP_WIDTH, (g + 1) * SSM_GROUP_WIDTH) for g in groups]
    cb = [_bdot(csb[:, :, gl[g]], bsb[:, :, gl[g]], 2, 2) for g in groups]
    y_intra = []
    for g in groups:
        xdt_g = xdt[:, :, hl[g]]
        y = None
        for r in range(HEADS_PER_GROUP):
            h = g * HEADS_PER_GROUP + r
            col = jnp.broadcast_to(cum[:, :, h:h + 1], (nb, c, c))
            row = cum_t[:, h:h + 1, :]
            dec = jnp.where(causal, jnp.exp(col - row), 0.0)
            lmat = (cb[g] * dec).astype(BF16)
            xm = jnp.where(lane_head == r, xdt_g, 0.0).astype(BF16)
            part = _bdot(lmat, xm, 2, 1)
            y = part if y is None else y + part
        y_intra.append(y)

    def finish(g, y_in, y_inter, b):
        y_all = y_in + y_inter * chunk_dec[b, :, hl[g]] + d_x[:, hl[g]] * xs[b, :, hl[g]]
        y_all = _rms(y_all * z[b, :, hl[g]], norm_w[:, hl[g]])
        cols = slice(o_col0 + g * SSM_GROUP_WIDTH, o_col0 + (g + 1) * SSM_GROUP_WIDTH)
        o_ref[b, :, cols] = y_all.astype(o_ref.dtype)

    if sequential:
        kv_t = [_bdot(bsb[:, :, gl[g]], x_end[:, :, hl[g]], 1, 1) for g in groups]
        for b in range(nb):
            for g in groups:
                s_t = read_state(g)
                y_inter = _dot(csb[b, :, gl[g]], s_t.astype(BF16))
                write_state(g, chunk_dec[b, c - 1:c, hl[g]] * s_t + kv_t[g][b])
                finish(g, y_intra[g][b], y_inter, b)
    else:
        for g in groups:
            kv = _bdot(x_end[:, :, hl[g]], bsb[:, :, gl[g]], 1, 1)
            decay = jnp.exp(_col_bcast(cum_last[:, :, hl[g]], SSM_STATE))
            s0 = read_state(g)
            y_inter = _bdot(csb[:, :, gl[g]], s0.astype(BF16), 2, 2)
            write_state(g, decay * s0 + kv)
            finish(g, y_intra[g], y_inter, slice(None))


def _upper(nb, c):
    r = lax.broadcasted_iota(jnp.int32, (nb, c, c), 1)
    col = lax.broadcasted_iota(jnp.int32, (nb, c, c), 2)
    return (r <= col).astype(BF16)


def _lower_bound(lb_logits):
    m = jnp.max(lb_logits, axis=0, keepdims=True)
    e = jnp.exp(lb_logits - m)
    return e[0:1, :] / jnp.sum(e, axis=0, keepdims=True)


def _ab_math(feat, conv, alog_ref, d_ref, sn_ref, sequential, read_h, write_h, read_s, write_s, o_ref):
    nb, c, _ = o_ref.shape
    causal = _causal(nb, c)
    w = HGRN_WIDTH
    _gla_heads(feat[:, :, _Q0:_Q0 + w], feat[:, :, _K0:_K0 + w], feat[:, :, _V0:_V0 + w], feat[:, :, _LF0:_LF0 + w],
               feat[:, :, _G0:_G0 + w], HGRN_HEADS, HGRN_HEAD_DIM, HGRN_HEAD_DIM, causal, sequential,
               read_h, write_h, o_ref, 0)
    act = _silu(conv)
    _ssd_heads(act[:, :, :SSM_INNER], act[:, :, SSM_INNER:SSM_INNER + SSM_GROUPS * SSM_STATE],
               act[:, :, SSM_INNER + SSM_GROUPS * SSM_STATE:], feat[:, :, _Z0:_Z0 + SSM_INNER],
               feat[:, :, _DT0:_DT0 + SSM_HEADS], -jnp.exp(alog_ref[...]), d_ref[...], sn_ref[...], causal,
               sequential, read_s, write_s, o_ref, HGRN_WIDTH)


def _ab_seq_kernel(proj_ref, cw_ref, cbias_ref, alog_ref, d_ref, sn_ref,
                   o_ref, sh_out, ss_out, sc_out, sh, ss, xpad):
    t = pl.program_id(1)
    nb, c, _ = o_ref.shape
    rows = nb * c

    @pl.when(t == 0)
    def _():
        sh[...] = jnp.zeros_like(sh)
        ss[...] = jnp.zeros_like(ss)
        xpad[0:SUBLANES, :] = jnp.zeros((SUBLANES, CONV_DIM), F32)

    xpad[SUBLANES:SUBLANES + rows, :] = proj_ref[:, :, _XBC0:_XBC0 + CONV_DIM].reshape(rows, CONV_DIM)
    padded = xpad[...]
    conv = cbias_ref[...] + padded[SUBLANES:] * cw_ref[CONV_W - 1:CONV_W, :]
    for d in range(1, CONV_W):
        conv = conv + pltpu.roll(padded, d, 0)[SUBLANES:] * cw_ref[CONV_W - 1 - d:CONV_W - d, :]
    xpad[0:SUBLANES, :] = padded[rows:rows + SUBLANES]
    conv = conv.reshape(nb, c, CONV_DIM)

    def read_h(h):
        return sh[h]

    def write_h(h, s):
        sh[h] = s

    def read_s(g):
        return ss[g]

    def write_s(g, s):
        ss[g] = s

    _ab_math(proj_ref, conv, alog_ref, d_ref, sn_ref, True, read_h, write_h, read_s, write_s, o_ref)

    @pl.when(t == pl.num_programs(1) - 1)
    def _():
        for h in range(HGRN_HEADS):
            sh_out[0, h] = sh[h].T
        for g in range(SSM_GROUPS):
            ss_out[0, g] = ss[g].T
        sc_out[0] = xpad[SUBLANES - (CONV_W - 1):SUBLANES, :]


def _ab_par_kernel(proj_ref, sh_in, ss_in, sc_in, cw_ref, cbias_ref, alog_ref, d_ref, sn_ref,
                   o_ref, sh_out, ss_out, sc_out, xpad, o3):
    nb, c, _ = o3.shape
    nbuf = CONV_W - 1
    proj = proj_ref[...].reshape(nb, c, proj_ref.shape[-1])
    xpad[:, 0:nbuf, :] = sc_in[...]
    xpad[:, nbuf:nbuf + c, :] = proj[:, :, _XBC0:_XBC0 + CONV_DIM]
    conv = cbias_ref[...]
    for k in range(CONV_W):
        conv = conv + xpad[:, k:k + c, :] * cw_ref[k:k + 1, :]
    sc_out[...] = xpad[:, c:c + nbuf, :]

    def read_h(h):
        return sh_in[:, h]

    def write_h(h, s):
        sh_out[:, h] = s

    def read_s(g):
        return ss_in[:, g]

    def write_s(g, s):
        ss_out[:, g] = s

    _ab_math(proj, conv, alog_ref, d_ref, sn_ref, False, read_h, write_h, read_s, write_s, o3)
    o_ref[...] = o3[...].reshape(o_ref.shape)


def _gla_seq_kernel(ready_ref, decay_ref, gate_ref, v_ref, o_ref, sg_out, sg):
    t = pl.program_id(1)

    @pl.when(t == 0)
    def _():
        sg[...] = jnp.zeros_like(sg)

    def read_g(h):
        return sg[h]

    def write_g(h, s):
        sg[h] = s

    nb, c, _ = o_ref.shape
    _gla_core(ready_ref[:, :, _GQD0:_GQD0 + GLA_KEY], ready_ref[:, :, _GKI0:_GKI0 + GLA_KEY],
              ready_ref[:, :, _GKE0:_GKE0 + GLA_KEY], decay_ref[...].reshape(nb, 1, GLA_KEY), None, v_ref[...],
              gate_ref, GLA_HEADS, GLA_HEAD_K, GLA_HEAD_V, _causal(nb, c), True, read_g, write_g, o_ref, 0)

    @pl.when(t == pl.num_programs(1) - 1)
    def _():
        for h in range(GLA_HEADS):
            sg_out[0, h] = sg[h].T


def _gla_par_kernel(proj_ref, gate_ref, v_ref, sg_in, o_ref, sg_out, o3):
    def read_g(h):
        return sg_in[:, h]

    def write_g(h, s):
        sg_out[:, h] = s

    nb, c, _ = o3.shape
    feat = proj_ref[...].reshape(nb, c, proj_ref.shape[-1])
    gate = gate_ref[...].astype(F32).reshape(nb, c, gate_ref.shape[-1])
    v = v_ref[...].astype(F32).reshape(nb, c, v_ref.shape[-1])
    _gla_heads(feat[:, :, _GQ0:_GQ0 + GLA_KEY], feat[:, :, _GK0:_GK0 + GLA_KEY], v,
               feat[:, :, _GLF0:_GLF0 + GLA_KEY], gate, GLA_HEADS, GLA_HEAD_K, GLA_HEAD_V, _causal(nb, c), False,
               read_g, write_g, o3, 0)
    o_ref[...] = o3[...].reshape(o_ref.shape)


SEQ_TILE_CHUNKS = 8
GLA_SEQ_TILE_CHUNKS = 8
PAR_TILE_SEQS = 8


def _full(shape):
    return pl.BlockSpec(shape, lambda *_: (0,) * len(shape))


def _mixer_o_dtype(c):
    return BF16 if c % (2 * SUBLANES) == 0 else F32


def _ab_seq_call(proj, batch, length, params, *, name):
    c = math.gcd(length, CHUNK)
    nb = SEQ_TILE_CHUNKS
    tiles = length // (c * nb)
    assert tiles * c * nb == length
    width = HEADS_PER_GROUP * SSM_HEAD_DIM
    blk = lambda n: pl.BlockSpec((nb, c, n), lambda b, t: (b * tiles + t, 0, 0))
    out_shapes = [
        jax.ShapeDtypeStruct((batch * length // c, c, AB_WIDTH), _mixer_o_dtype(c)),
        jax.ShapeDtypeStruct((batch, HGRN_HEADS, HGRN_HEAD_DIM, HGRN_HEAD_DIM), F32),
        jax.ShapeDtypeStruct((batch, SSM_GROUPS, width, SSM_STATE), F32),
        jax.ShapeDtypeStruct((batch, CONV_W - 1, CONV_DIM), F32),
    ]
    out_specs = [
        blk(AB_WIDTH),
        pl.BlockSpec((1, HGRN_HEADS, HGRN_HEAD_DIM, HGRN_HEAD_DIM), lambda b, t: (b, 0, 0, 0)),
        pl.BlockSpec((1, SSM_GROUPS, width, SSM_STATE), lambda b, t: (b, 0, 0, 0)),
        pl.BlockSpec((1, CONV_W - 1, CONV_DIM), lambda b, t: (b, 0, 0)),
    ]
    return pl.pallas_call(
        _ab_seq_kernel,
        grid=(batch, tiles),
        in_specs=[blk(AB_FEAT)] + [_full(p.shape) for p in params],
        out_specs=out_specs,
        out_shape=out_shapes,
        scratch_shapes=[
            pltpu.VMEM((HGRN_HEADS, HGRN_HEAD_DIM, HGRN_HEAD_DIM), F32),
            pltpu.VMEM((SSM_GROUPS, SSM_STATE, width), F32),
            pltpu.VMEM((nb * c + SUBLANES, CONV_DIM), F32),
        ],
        compiler_params=pltpu.CompilerParams(dimension_semantics=("arbitrary", "arbitrary"),
                                             vmem_limit_bytes=VMEM_LIMIT_BYTES),
        name=name,
    )(proj, *params)


def _ab_par_call(proj, row0, c, s_hgrn, s_ssm, s_conv, params, *, name):
    batch = s_hgrn.shape[0]
    nb = _row_tile(batch, PAR_TILE_SEQS)
    rows = nb * c
    assert row0 % rows == 0
    width = HEADS_PER_GROUP * SSM_HEAD_DIM
    s_ssm = s_ssm.reshape(batch, SSM_GROUPS, width, SSM_STATE)
    blk3 = lambda a, n: pl.BlockSpec((nb, a, n), lambda b: (b, 0, 0))
    blk4 = lambda a, r, n: pl.BlockSpec((nb, a, r, n), lambda b: (b, 0, 0, 0))
    state_specs = [blk4(HGRN_HEADS, HGRN_HEAD_DIM, HGRN_HEAD_DIM), blk4(SSM_GROUPS, width, SSM_STATE),
                   blk3(CONV_W - 1, CONV_DIM)]
    out_shapes = [
        jax.ShapeDtypeStruct((batch * c, AB_WIDTH), F32),
        jax.ShapeDtypeStruct(s_hgrn.shape, F32),
        jax.ShapeDtypeStruct(s_ssm.shape, F32),
        jax.ShapeDtypeStruct(s_conv.shape, F32),
    ]
    return pl.pallas_call(
        _ab_par_kernel,
        grid=(batch // nb,),
        in_specs=[pl.BlockSpec((rows, proj.shape[1]), lambda b: (row0 // rows + b, 0))] + state_specs
                 + [_full(p.shape) for p in params],
        out_specs=[pl.BlockSpec((rows, AB_WIDTH), lambda b: (b, 0))] + state_specs,
        out_shape=out_shapes,
        scratch_shapes=[pltpu.VMEM((nb, c + CONV_W - 1, CONV_DIM), F32), pltpu.VMEM((nb, c, AB_WIDTH), F32)],
        compiler_params=pltpu.CompilerParams(dimension_semantics=("arbitrary",),
                                             vmem_limit_bytes=VMEM_LIMIT_BYTES),
        name=name,
    )(proj, s_hgrn, s_ssm, s_conv, *params)


def _gla_seq_call(ready, decay, gate, v, batch, length, *, name):
    c = math.gcd(length, CHUNK)
    assert c == CHUNK
    nb = GLA_SEQ_TILE_CHUNKS
    tiles = length // (c * nb)
    assert tiles * c * nb == length
    blk = lambda n: pl.BlockSpec((nb, c, n), lambda b, t: (b * tiles + t, 0, 0))
    return pl.pallas_call(
        _gla_seq_kernel,
        grid=(batch, tiles),
        in_specs=[blk(GLA_READY), pl.BlockSpec((nb, GLA_KEY), lambda b, t: (b * tiles + t, 0)), blk(GLA_VAL),
                  blk(GLA_VAL)],
        out_specs=[blk(GLA_VAL),
                   pl.BlockSpec((1, GLA_HEADS, GLA_HEAD_K, GLA_HEAD_V), lambda b, t: (b, 0, 0, 0))],
        out_shape=[jax.ShapeDtypeStruct((batch * length // c, c, GLA_VAL), _mixer_o_dtype(c)),
                   jax.ShapeDtypeStruct((batch, GLA_HEADS, GLA_HEAD_K, GLA_HEAD_V), F32)],
        scratch_shapes=[pltpu.VMEM((GLA_HEADS, GLA_HEAD_V, GLA_HEAD_K), F32)],
        compiler_params=pltpu.CompilerParams(dimension_semantics=("arbitrary", "arbitrary"),
                                             vmem_limit_bytes=VMEM_LIMIT_BYTES),
        name=name,
    )(ready, decay, gate, v)


def _gla_par_call(proj, gate, v, row0, c, s_gla, *, name):
    batch = s_gla.shape[0]
    nb = _row_tile(batch, PAR_TILE_SEQS)
    rows = nb * c
    assert row0 % rows == 0
    sspec = pl.BlockSpec((nb, GLA_HEADS, GLA_HEAD_K, GLA_HEAD_V), lambda b: (b, 0, 0, 0))
    return pl.pallas_call(
        _gla_par_kernel,
        grid=(batch // nb,),
        in_specs=[pl.BlockSpec((rows, a.shape[1]), lambda b: (row0 // rows + b, 0)) for a in (proj, gate, v)] + [sspec],
        out_specs=[pl.BlockSpec((rows, GLA_VAL), lambda b: (b, 0)), sspec],
        out_shape=[jax.ShapeDtypeStruct((batch * c, GLA_VAL), F32), jax.ShapeDtypeStruct(s_gla.shape, F32)],
        scratch_shapes=[pltpu.VMEM((nb, c, GLA_VAL), F32)],
        compiler_params=pltpu.CompilerParams(dimension_semantics=("arbitrary",),
                                             vmem_limit_bytes=VMEM_LIMIT_BYTES),
        name=name,
    )(proj, gate, v, s_gla)


def _split_cols(w):
    main = w.shape[1] // LANES * LANES
    tail = jnp.pad(w[:, main:].astype(BF16), ((0, 0), (0, LANES - (w.shape[1] - main))))
    return w[:, :main].astype(BF16), tail


def kernel(x_prompt, x_sample, state_hgrn, state_ssm, state_conv, state_gla, norm_ffn1, norm_mix, norm_ffn2, norm_final, ffn1_w_in, ffn1_w_out, ffn2_w_in, ffn2_w_out, ab_w_in, ab_w_out, hgrn_lb_logits, hgrn_norm, ssm_conv_w, ssm_conv_b, ssm_dt_bias, ssm_a_log, ssm_d, ssm_norm, gla_w_in, gla_w_gk, gla_b_gk, gla_norm, gla_w_out):
    assert ab_w_in.shape[0] == 1 and gla_w_in.shape[0] == 1, "one HGRN2/SSD layer and one GLA layer"
    bp, lp, _ = x_prompt.shape
    bs, ls, _ = x_sample.shape
    rp, rs = bp * lp, bs * ls
    cp, cs = math.gcd(lp, CHUNK), math.gcd(ls, CHUNK)
    assert rs % cp == 0
    row = lambda v: v.reshape(1, -1)
    stack_row = lambda v: v.reshape(v.shape[0], 1, v.shape[1])
    nf1, nm, nf2 = stack_row(norm_ffn1), stack_row(norm_mix), stack_row(norm_ffn2)
    in_rows, out_rows = D_MODEL // 64, D_FF // 16
    xp = x_prompt.reshape(rp, D_MODEL)
    xs = x_sample.reshape(rs, D_MODEL)

    wp, wt = _split_cols(ab_w_in[0])
    ab_feat_params = (hgrn_lb_logits, jnp.tile(hgrn_norm[0], HGRN_HEADS).reshape(1, -1),
                      jnp.pad(ssm_dt_bias[0], (0, LANES - SSM_HEADS)).reshape(1, -1))
    casts = (_Cast(ffn2_w_in, in_rows), _Cast(ffn2_w_out, out_rows), _Cast(ffn1_w_in, in_rows, layer=1),
             _Cast(ffn1_w_out, out_rows, layer=1), _Cast(ab_w_out, in_rows, layer=0),
             _Cast(gla_w_in[0], in_rows, split=True), _Cast(gla_w_out, in_rows, layer=0))
    (x1, feat, w2_in, w2_out, w1_in, w1_out, ab_wo, gla_wp, gla_wt, gla_wo) = _pre_call(
        xp, xs, 0, nf1, ffn1_w_in[0].astype(BF16), ffn1_w_out[0].astype(BF16), nm, wp, wt, ab_feat_params, casts,
        mode="ab", row_tile=PRE0_ROWS, name="pre0")
    ab_params = (ssm_conv_w[0], row(ssm_conv_b[0]), row(ssm_a_log[0]), row(ssm_d[0]), row(ssm_norm[0]))
    o_p, hgrn_p, ssm_p, conv_p = _ab_seq_call(feat.reshape((rp + rs) // cp, cp, AB_FEAT), bp, lp, ab_params,
                                              name="mix0_prompt")
    o_s, hgrn_s, ssm_s, conv_s = _ab_par_call(feat, rp, cs, state_hgrn[0], state_ssm[0], state_conv[0], ab_params,
                                              name="mix0_sample")
    (x3,) = _post_call(x1, o_p.reshape(rp, AB_WIDTH), o_s, 0, ab_wo, nf2,
                       w2_in, w2_out, row(norm_final), split_output=False, name="post0")

    gla_feat_params = (gla_w_gk[0].astype(BF16), row(gla_b_gk[0]), jnp.tile(gla_norm[0], GLA_HEADS).reshape(1, -1))
    x4, feat, gate, v16, ready, decay = _pre_call(x3, None, 1, nf1, w1_in, w1_out, nm, gla_wp, gla_wt, gla_feat_params,
                                                  mode="gla", row_tile=PRE1_ROWS, name="pre1")
    chunked = lambda a: a.reshape((rp + rs) // cp, cp, a.shape[1])
    o_p, gla_p = _gla_seq_call(chunked(ready), decay, chunked(gate), chunked(v16), bp, lp, name="mix1_prompt")
    o_s, gla_s = _gla_par_call(feat, gate, v16, rp, cs, state_gla[0], name="mix1_sample")
    y_p, y_s = _post_call(x4, o_p.reshape(rp, GLA_VAL), o_s, 1, gla_wo, nf2,
                          w2_in, w2_out, row(norm_final), split_output=True, name="post1")

    ssm_shape = (1, -1, SSM_HEADS, SSM_HEAD_DIM, SSM_STATE)
    return (y_p.reshape(bp, lp, D_MODEL), y_s.reshape(bs, ls, D_MODEL), hgrn_p[None], hgrn_s[None],
            ssm_p.reshape(ssm_shape), ssm_s.reshape(ssm_shape), conv_p[None], conv_s[None], gla_p[None], gla_s[None])
```

```python
import functools
import math

import jax
import jax.numpy as jnp
from jax import lax
from jax.experimental import pallas as pl
from jax.experimental.pallas import tpu as pltpu

F32 = jnp.float32
BF16 = jnp.bfloat16

D_MODEL = 1024
D_FF = 2816
EPS = 1e-6
CHUNK = 64

HGRN_HEADS = 4
HGRN_HEAD_DIM = 128
HGRN_WIDTH = HGRN_HEADS * HGRN_HEAD_DIM

SSM_HEADS = 8
SSM_HEAD_DIM = 64
SSM_INNER = SSM_HEADS * SSM_HEAD_DIM
SSM_GROUPS = 2
SSM_STATE = 128
SSM_GROUP_WIDTH = SSM_INNER // SSM_GROUPS
HEADS_PER_GROUP = SSM_HEADS // SSM_GROUPS
CONV_W = 4
CONV_DIM = SSM_INNER + 2 * SSM_GROUPS * SSM_STATE
AB_WIDTH = HGRN_WIDTH + SSM_INNER

GLA_HEADS = 4
GLA_HEAD_K = 128
GLA_HEAD_V = 256
GLA_KEY = GLA_HEADS * GLA_HEAD_K
GLA_VAL = GLA_HEADS * GLA_HEAD_V
GK_RANK = 16
GK_NORMALIZER = 16.0

LANES = 128
SUBLANES = 8
VMEM_LIMIT_BYTES = 56 * 1024 * 1024

_Q0, _K0, _LF0, _V0, _G0 = (i * HGRN_WIDTH for i in range(5))
_Z0 = 5 * HGRN_WIDTH
_XBC0 = _Z0 + SSM_INNER
_DT0 = _XBC0 + CONV_DIM
AB_FEAT = _DT0 + LANES
_WQ, _WF, _WI, _WG, _WZ = (i * HGRN_WIDTH for i in range(5))
_WXBC = 4 * HGRN_WIDTH + SSM_INNER

_GQ0, _GK0, _GLF0 = 0, GLA_KEY, 2 * GLA_KEY
GLA_FEAT = 3 * GLA_KEY
_GQD0, _GKI0, _GKE0 = 0, GLA_KEY, 2 * GLA_KEY
GLA_READY = 3 * GLA_KEY
_GWQ, _GWK = 0, GLA_KEY
_GWV = 2 * GLA_KEY
_GWG = _GWV + GLA_VAL


def _rms(x, w):
    return x * lax.rsqrt(jnp.mean(x * x, axis=-1, keepdims=True) + EPS) * w


def _rms_core(x):
    return x * lax.rsqrt(jnp.mean(x * x, axis=-1, keepdims=True) + EPS)


def _silu(x):
    return x * jax.nn.sigmoid(x)


def _softplus(x):
    return jnp.maximum(x, 0.0) + jnp.log1p(jnp.exp(-jnp.abs(x)))


def _dot(a, b):
    return jnp.dot(a, b, preferred_element_type=F32)


def _bdot(a, b, ca, cb):
    return lax.dot_general(a, b, (((ca,), (cb,)), ((0,), (0,))), preferred_element_type=F32)


def _split3(x):
    hi = x.astype(BF16)
    r1 = x - hi.astype(F32)
    mid = r1.astype(BF16)
    lo = (r1 - mid.astype(F32)).astype(BF16)
    return hi, mid, lo


def _exact_bdot_lhs01(m01, x, ca, cb):
    return sum(_bdot(m01, p, ca, cb) for p in _split3(x))


def _exact_bdot_rhs01(x, m01, ca, cb):
    return sum(_bdot(p, m01, ca, cb) for p in _split3(x))


def _causal(nb, c):
    r = lax.broadcasted_iota(jnp.int32, (nb, c, c), 1)
    col = lax.broadcasted_iota(jnp.int32, (nb, c, c), 2)
    return r >= col


BF16_ROWS = 2 * SUBLANES


def _triangle3(nb, c, wide_axis):
    shape = (1, c, 3 * c) if wide_axis == 2 else (1, 3 * c, c)
    wide = lax.broadcasted_iota(jnp.int32, shape, wide_axis)
    narrow = lax.broadcasted_iota(jnp.int32, shape, 3 - wide_axis)
    hit = None
    for k in range(3):
        wk = wide - k * c
        term = (wk >= 0) & (wk < c) & (wk <= narrow)
        hit = term if hit is None else hit | term
    return jnp.broadcast_to(hit.astype(BF16), (nb,) + shape[1:])


def _chunk_cumsum(x, causal):
    nb, c, _ = x.shape
    if c % BF16_ROWS:
        return _exact_bdot_lhs01(causal.astype(BF16), x, 2, 1)
    return _bdot(_triangle3(nb, c, 2), jnp.concatenate(_split3(x), axis=1), 2, 1)


def _chunk_cumsum_t(x):
    nb, c, _ = x.shape
    if c % BF16_ROWS:
        return _exact_bdot_rhs01(x, _upper(nb, c), 1, 1)
    return _bdot(jnp.concatenate(_split3(x), axis=1), _triangle3(nb, c, 1), 1, 1)


def _col_bcast(row, lanes=LANES):
    nb, _, k = row.shape
    hi, mid, lo = (p.astype(F32) for p in _split3(row))
    r = lax.broadcasted_iota(jnp.int32, (1, BF16_ROWS, k), 1)
    stacked = jnp.where(r == 0, hi, jnp.where(r == 1, mid, jnp.where(r == 2, lo, 0.0))).astype(BF16)
    return _bdot(stacked, jnp.ones((nb, BF16_ROWS, lanes), BF16), 1, 1)


MXU_DIM = 256
FF_TILES = ((0, 6 * MXU_DIM), (6 * MXU_DIM, D_FF))
assert all(lo % MXU_DIM == 0 and hi % MXU_DIM == 0 for lo, hi in FF_TILES)


def _ffn_half(x, norm_w, w_in_ref, w_out_ref):
    hb = _rms(x, norm_w).astype(BF16)
    acc = None
    for lo, hi in FF_TILES:
        gate = _dot(hb, w_in_ref[:, lo:hi])
        up = _dot(hb, w_in_ref[:, D_FF + lo:D_FF + hi])
        act = (_silu(gate) * up).astype(BF16)
        part = _dot(act, w_out_ref[lo:hi, :])
        acc = part if acc is None else acc + part
    return 0.5 * acc


def _pick_group(first_steps, a, b):
    return jnp.where(pl.program_id(0) < first_steps, a, b)


def _cast_rows(src_ref, *dst_refs):
    if len(dst_refs) == 1:
        dst_refs[0][...] = src_ref[...].astype(BF16)
        return
    main_ref, tail_ref = dst_refs
    main = main_ref.shape[-1]
    rest = src_ref.shape[-1] - main
    main_ref[...] = src_ref[:, :main].astype(BF16)
    tail_ref[...] = jnp.zeros(tail_ref.shape, BF16)
    tail_ref[:, :rest] = src_ref[:, main:].astype(BF16)


def _ab_features(hb, wp_ref, wt_ref, lb_ref, hn_ref, dtb_ref, feat_ref):
    def group(w0):
        return _dot(hb, wp_ref[:, w0:w0 + HGRN_WIDTH])

    def put(c0, val):
        feat_ref[:, c0:c0 + val.shape[-1]] = val

    lb = _lower_bound(lb_ref[...])
    f = lb + (1.0 - lb) * jax.nn.sigmoid(group(_WF))
    put(_K0, 1.0 - f)
    put(_LF0, jnp.log(f))
    put(_Q0, _silu(group(_WQ)))
    put(_G0, _silu(group(_WG)) * hn_ref[...])
    put(_Z0, _silu(group(_WZ)))
    put(_DT0, _softplus(_dot(hb, wt_ref[...]) + dtb_ref[...]))
    put(_XBC0, _dot(hb, wp_ref[:, _WXBC:_WXBC + CONV_DIM]))
    put(_V0, group(_WI))


def _decayed_operands(q, k, log_f):
    nb, c, _ = q.shape
    g = _chunk_cumsum(log_f, _causal(nb, c))
    g_last = g[:, c - 1:c, :]
    q_dec = (q * jnp.exp(g)).astype(BF16)
    k_inv_f = k * jnp.exp(-g)
    decay = jnp.exp(g_last)
    k_end = (k_inv_f * decay).astype(BF16)
    return q_dec, k_inv_f.astype(BF16), k_end, decay, g_last


def _gla_features(hb, wp_ref, wt_ref, wgk_ref, bgk_ref, gn_ref, feat_ref, gate_ref, v_ref, ready_ref, decay_ref):
    def put(c0, val):
        feat_ref[:, c0:c0 + val.shape[-1]] = val

    rows = hb.shape[0]
    nb = rows // CHUNK
    gk_low = _dot(hb, wt_ref[...])[:, :GK_RANK].astype(BF16)
    gk = _dot(gk_low, wgk_ref[...]) + bgk_ref[...]
    log_f = -_softplus(-gk) / GK_NORMALIZER
    q = _dot(hb, wp_ref[:, _GWQ:_GWQ + GLA_KEY]) * (GLA_HEAD_K ** -0.5)
    k = _dot(hb, wp_ref[:, _GWK:_GWK + GLA_KEY])
    put(_GLF0, log_f)
    put(_GQ0, q)
    put(_GK0, k)
    chunked = lambda t: t.reshape(nb, CHUNK, GLA_KEY)
    q_dec, k_inv, k_end, decay, _ = _decayed_operands(chunked(q), chunked(k), chunked(log_f))
    for c0, val in ((_GQD0, q_dec), (_GKI0, k_inv), (_GKE0, k_end)):
        ready_ref[:, c0:c0 + GLA_KEY] = val.reshape(rows, GLA_KEY)
    decay_ref[...] = decay.reshape(nb, GLA_KEY)
    gate_ref[...] = (_silu(_dot(hb, wp_ref[:, _GWG:_GWG + GLA_VAL])) * gn_ref[...]).astype(BF16)
    v_ref[...] = _dot(hb, wp_ref[:, _GWV:_GWV + GLA_VAL]).astype(BF16)


_FEATURES = {
    "ab": (_ab_features, ((AB_FEAT, F32, 1),)),
    "gla": (_gla_features, ((GLA_FEAT, F32, 1), (GLA_VAL, BF16, 1), (GLA_VAL, BF16, 1), (GLA_READY, BF16, 1),
                            (GLA_KEY, F32, CHUNK))),
}
N_FEATURE_PARAMS = 3


def _pre_kernel(*refs, first_steps, cast_arity, mode):
    n_x = 1 if first_steps is None else 2
    x_refs, refs = refs[:n_x], refs[n_x:]
    nf_ref, w_in_ref, w_out_ref, nm_ref, wp_ref, wt_ref = refs[:6]
    feat_params, refs = refs[6:6 + N_FEATURE_PARAMS], refs[6 + N_FEATURE_PARAMS:]
    feature_fn, feature_outs = _FEATURES[mode]
    cast_src, refs = refs[:len(cast_arity)], refs[len(cast_arity):]
    x1_ref, feat_refs, cast_dst = refs[0], refs[1:1 + len(feature_outs)], list(refs[1 + len(feature_outs):])
    if first_steps is None:
        x = x_refs[0][...]
    else:
        x = _pick_group(first_steps, x_refs[0][...], x_refs[1][...])
    x1 = x + _ffn_half(x, nf_ref[...], w_in_ref, w_out_ref)
    x1_ref[...] = x1
    hb = _rms(x1, nm_ref[...]).astype(BF16)
    feature_fn(hb, wp_ref, wt_ref, *feat_params, *feat_refs)
    for src_ref, arity in zip(cast_src, cast_arity):
        _cast_rows(src_ref, *cast_dst[:arity])
        cast_dst = cast_dst[arity:]


def _post_kernel(x_ref, oa_ref, ob_ref, wo_ref, nf_ref, w_in_ref, w_out_ref, nfin_ref, *y_refs, first_steps):
    o = _pick_group(first_steps, oa_ref[...].astype(BF16), ob_ref[...].astype(BF16))
    x2 = x_ref[...] + _dot(o, wo_ref[...])
    y = x2 + _ffn_half(x2, nf_ref[...], w_in_ref, w_out_ref)
    if len(y_refs) == 1:
        y_refs[0][...] = y
    else:
        y = _rms(y, nfin_ref[...])
        ya_ref, yb_ref = y_refs

        @pl.when(pl.program_id(0) < first_steps)
        def _():
            ya_ref[...] = y

        @pl.when(pl.program_id(0) >= first_steps)
        def _():
            yb_ref[...] = y


def _resident(shape, layer=None):
    if layer is None:
        return pl.BlockSpec(shape, lambda *_: (0,) * len(shape), pipeline_mode=pl.Buffered(1))
    return pl.BlockSpec((None,) + tuple(shape[1:]), lambda *_: (layer,) + (0,) * (len(shape) - 1),
                        pipeline_mode=pl.Buffered(1))


def _row_tile(rows, want):
    t = min(rows, want)
    assert rows % t == 0
    return t


PRE0_ROWS = 256
PRE1_ROWS = 512
POST_ROWS = 512


def _group_specs(tm, first_steps, width):
    first = pl.BlockSpec((tm, width), lambda i: (jnp.minimum(i, first_steps - 1), 0))
    second = pl.BlockSpec((tm, width), lambda i: (jnp.maximum(i - first_steps, 0), 0))
    return first, second


def _weight_spec(w, layer):
    return _resident(w.shape, layer if w.ndim == 3 else None)


class _Cast:
    def __init__(self, src, rows_per_step, layer=None, split=False):
        self.src, self.rps, self.layer, self.split = src, rows_per_step, layer, split
        rows = src.shape[-2]
        assert rows % rows_per_step == 0 and rows_per_step % (2 * SUBLANES) == 0
        self.steps = rows // rows_per_step
        assert not split or layer is not None or src.ndim == 2

    def _index(self, lead):
        last = self.steps - 1
        return lambda i: lead + (jnp.minimum(i, last), 0)

    def in_spec(self):
        cols = self.src.shape[-1]
        if self.src.ndim == 2:
            return pl.BlockSpec((self.rps, cols), self._index(()))
        if self.layer is None:
            return pl.BlockSpec((self.src.shape[0], self.rps, cols), self._index((0,)))
        return pl.BlockSpec((None, self.rps, cols), self._index((self.layer,)))

    def outs(self):
        rows, cols = self.src.shape[-2:]
        if self.src.ndim == 3 and self.layer is None:
            n = self.src.shape[0]
            return [(jax.ShapeDtypeStruct((n, rows, cols), BF16), pl.BlockSpec((n, self.rps, cols), self._index((0,))))]
        widths = [cols // LANES * LANES, LANES] if self.split else [cols]
        return [(jax.ShapeDtypeStruct((rows, w), BF16), pl.BlockSpec((self.rps, w), self._index(()))) for w in widths]


def _pre_call(xa, xb, layer, nf, w_in, w_out, nm, wp, wt, feat_params, casts=(), *, mode, row_tile, name):
    ra = xa.shape[0]
    rb = 0 if xb is None else xb.shape[0]
    rows = ra + rb
    tm = _row_tile(rb if rb else ra, row_tile)
    assert ra % tm == 0
    assert all(cast.steps <= rows // tm for cast in casts)
    assert len(feat_params) == N_FEATURE_PARAMS
    feature_outs = _FEATURES[mode][1]
    tok = lambda n: pl.BlockSpec((tm, n), lambda i: (i, 0))
    if xb is None:
        first_steps, x_specs, xs = None, [tok(D_MODEL)], (xa,)
    else:
        first_steps = ra // tm
        x_specs, xs = list(_group_specs(tm, first_steps, D_MODEL)), (xa, xb)
    cast_outs = [cast.outs() for cast in casts]
    flat_outs = [o for outs in cast_outs for o in outs]
    return pl.pallas_call(
        functools.partial(_pre_kernel, first_steps=first_steps, cast_arity=tuple(len(o) for o in cast_outs),
                          mode=mode),
        grid=(rows // tm,),
        in_specs=x_specs + [_resident(nf.shape, layer), _weight_spec(w_in, layer), _weight_spec(w_out, layer),
                            _resident(nm.shape, layer), _resident(wp.shape), _resident(wt.shape)]
                         + [_resident(p.shape) for p in feat_params] + [cast.in_spec() for cast in casts],
        out_specs=[tok(D_MODEL)] + [pl.BlockSpec((tm // div, w), lambda i: (i, 0)) for w, _, div in feature_outs]
                  + [spec for _, spec in flat_outs],
        out_shape=[jax.ShapeDtypeStruct((rows, D_MODEL), F32)]
                  + [jax.ShapeDtypeStruct((rows // div, w), dt) for w, dt, div in feature_outs]
                  + [shape for shape, _ in flat_outs],
        compiler_params=pltpu.CompilerParams(dimension_semantics=("arbitrary",), vmem_limit_bytes=VMEM_LIMIT_BYTES),
        name=name,
    )(*xs, nf, w_in, w_out, nm, wp, wt, *feat_params, *[cast.src for cast in casts])


def _post_call(x, oa, ob, layer, wo, nf, w_in, w_out, nfin, *, split_output, name):
    rows = x.shape[0]
    ra, rb = oa.shape[0], ob.shape[0]
    assert ra + rb == rows
    tm = _row_tile(rb, POST_ROWS)
    assert ra % tm == 0
    first_steps = ra // tm
    tok = lambda n: pl.BlockSpec((tm, n), lambda i: (i, 0))
    spec_a, spec_b = _group_specs(tm, first_steps, oa.shape[1])
    if split_output:
        out_specs = list(_group_specs(tm, first_steps, D_MODEL))
        out_shape = [jax.ShapeDtypeStruct((ra, D_MODEL), F32), jax.ShapeDtypeStruct((rb, D_MODEL), F32)]
    else:
        out_specs = [tok(D_MODEL)]
        out_shape = [jax.ShapeDtypeStruct((rows, D_MODEL), F32)]
    return pl.pallas_call(
        functools.partial(_post_kernel, first_steps=first_steps),
        grid=(rows // tm,),
        in_specs=[tok(D_MODEL), spec_a, spec_b, _resident(wo.shape), _resident(nf.shape, layer),
                  _resident(w_in.shape, layer), _resident(w_out.shape, layer), _resident(nfin.shape)],
        out_specs=out_specs,
        out_shape=out_shape,
        compiler_params=pltpu.CompilerParams(dimension_semantics=("arbitrary",), vmem_limit_bytes=VMEM_LIMIT_BYTES),
        name=name,
    )(x, oa, ob, wo, nf, w_in, w_out, nfin)


def _gla_heads(q, k, v, log_f, gate, n_heads, dk, dv, causal, sequential, read_state, write_state,
               o_ref, o_col0):
    q_dec, k_inv, k_end, decay, g_last = _decayed_operands(q, k, log_f)
    _gla_core(q_dec, k_inv, k_end, decay, g_last, v, gate, n_heads, dk, dv, causal, sequential, read_state,
              write_state, o_ref, o_col0)


def _gla_core(q_dec, k_inv, k_end, decay, g_last, v, gate, n_heads, dk, dv, causal, sequential, read_state,
              write_state, o_ref, o_col0):
    nb, c, _ = q_dec.shape
    vb = v.astype(BF16)
    heads = range(n_heads)
    ks = [slice(h * dk, (h + 1) * dk) for h in heads]
    vs = [slice(h * dv, (h + 1) * dv) for h in heads]
    scores = [_bdot(q_dec[:, :, ks[h]], k_inv[:, :, ks[h]], 2, 2) for h in heads]
    scores = [jnp.where(causal, sc, 0.0).astype(BF16) for sc in scores]
    o_intra = [_bdot(scores[h], vb[:, :, vs[h]], 2, 1) for h in heads]

    def emit(h, b, o):
        cols = slice(o_col0 + h * dv, o_col0 + (h + 1) * dv)
        o_ref[b, :, cols] = (_rms_core(o) * gate[b, :, vs[h]]).astype(o_ref.dtype)

    if sequential:
        kv_t = [_bdot(vb[:, :, vs[h]], k_end[:, :, ks[h]], 1, 1) for h in heads]
        for b in range(nb):
            for h in heads:
                s_t = read_state(h)
                o_inter = lax.dot_general(q_dec[b, :, ks[h]], s_t.astype(BF16), (((1,), (1,)), ((), ())),
                                          preferred_element_type=F32)
                write_state(h, decay[b, :, ks[h]] * s_t + kv_t[h][b])
                emit(h, b, o_intra[h][b] + o_inter)
    else:
        for h in heads:
            kv = _bdot(k_end[:, :, ks[h]], vb[:, :, vs[h]], 1, 1)
            decay_col = jnp.exp(_col_bcast(g_last[:, :, ks[h]]))
            decay_col = jnp.concatenate([decay_col] * (dv // LANES), axis=-1)
            s0 = read_state(h)
            o_inter = _bdot(q_dec[:, :, ks[h]], s0.astype(BF16), 2, 1)
            write_state(h, decay_col * s0 + kv)
            emit(h, slice(None), o_intra[h] + o_inter)


def _expand_heads(x, expand):
    return _exact_bdot_rhs01(x, expand, 2, 1)


def _ssd_heads(xs, bs, cs, z, dt, a_row, d_row, norm_w, causal, sequential, read_state, write_state,
               o_ref, o_col0):
    nb, c, _ = xs.shape
    hrow = lax.broadcasted_iota(jnp.int32, (SSM_HEADS, SSM_INNER), 0)
    hcol = lax.broadcasted_iota(jnp.int32, (SSM_HEADS, SSM_INNER), 1) // SSM_HEAD_DIM
    expand2d = (hrow == hcol).astype(BF16)
    expand = jnp.broadcast_to(expand2d[None], (nb, SSM_HEADS, SSM_INNER))
    d_x = sum(_dot(p, expand2d) for p in _split3(d_row))
    dta = dt * a_row
    cum = _chunk_cumsum(dta, causal)
    cum_t = _chunk_cumsum_t(dta)
    dt_x = _expand_heads(dt, expand)
    cum_x = _expand_heads(cum, expand)
    xdt = xs * dt_x
    cum_last = cum_x[:, c - 1:c, :]
    x_end = (xdt * jnp.exp(cum_last - cum_x)).astype(BF16)
    chunk_dec = jnp.exp(cum_x)
    bsb = bs.astype(BF16)
    csb = cs.astype(BF16)
    lane_head = lax.broadcasted_iota(jnp.int32, (nb, c, SSM_GROUP_WIDTH), 2) // SSM_HEAD_DIM
    groups = range(SSM_GROUPS)
    gl = [slice(g * SSM_STATE, (g + 1) * SSM_STATE) for g in groups]
    hl = [slice(g * SSM_GROUP_WIDTH, (g + 1) * SSM_GROUP_WIDTH) for g in groups]
    cb = [_bdot(csb[:, :, gl[g]], bsb[:, :, gl[g]], 2, 2) for g in groups]
    y_intra = []
    for g in groups:
        xdt_g = xdt[:, :, hl[g]]
        y = None
        for r in range(HEADS_PER_GROUP):
            h = g * HEADS_PER_GROUP + r
            col = jnp.broadcast_to(cum[:, :, h:h + 1], (nb, c, c))
            row = cum_t[:, h:h + 1, :]
            dec = jnp.where(causal, jnp.exp(col - row), 0.0)
            lmat = (cb[g] * dec).astype(BF16)
            xm = jnp.where(lane_head == r, xdt_g, 0.0).astype(BF16)
            part = _bdot(lmat, xm, 2, 1)
            y = part if y is None else y + part
        y_intra.append(y)

    def finish(g, y_in, y_inter, b):
        y_all = y_in + y_inter * chunk_dec[b, :, hl[g]] + d_x[:, hl[g]] * xs[b, :, hl[g]]
        y_all = _rms(y_all * z[b, :, hl[g]], norm_w[:, hl[g]])
        cols = slice(o_col0 + g * SSM_GROUP_WIDTH, o_col0 + (g + 1) * SSM_GROUP_WIDTH)
        o_ref[b, :, cols] = y_all.astype(o_ref.dtype)

    if sequential:
        kv_t = [_bdot(bsb[:, :, gl[g]], x_end[:, :, hl[g]], 1, 1) for g in groups]
        for b in range(nb):
            for g in groups:
                s_t = read_state(g)
                y_inter = _dot(csb[b, :, gl[g]], s_t.astype(BF16))
                write_state(g, chunk_dec[b, c - 1:c, hl[g]] * s_t + kv_t[g][b])
                finish(g, y_intra[g][b], y_inter, b)
    else:
        for g in groups:
            kv = _bdot(x_end[:, :, hl[g]], bsb[:, :, gl[g]], 1, 1)
            decay = jnp.exp(_col_bcast(cum_last[:, :, hl[g]], SSM_STATE))
            s0 = read_state(g)
            y_inter = _bdot(csb[:, :, gl[g]], s0.astype(BF16), 2, 2)
            write_state(g, decay * s0 + kv)
            finish(g, y_intra[g], y_inter, slice(None))


def _upper(nb, c):
    r = lax.broadcasted_iota(jnp.int32, (nb, c, c), 1)
    col = lax.broadcasted_iota(jnp.int32, (nb, c, c), 2)
    return (r <= col).astype(BF16)


def _lower_bound(lb_logits):
    m = jnp.max(lb_logits, axis=0, keepdims=True)
    e = jnp.exp(lb_logits - m)
    return e[0:1, :] / jnp.sum(e, axis=0, keepdims=True)


def _ab_math(feat, conv, alog_ref, d_ref, sn_ref, sequential, read_h, write_h, read_s, write_s, o_ref):
    nb, c, _ = o_ref.shape
    causal = _causal(nb, c)
    w = HGRN_WIDTH
    _gla_heads(feat[:, :, _Q0:_Q0 + w], feat[:, :, _K0:_K0 + w], feat[:, :, _V0:_V0 + w], feat[:, :, _LF0:_LF0 + w],
               feat[:, :, _G0:_G0 + w], HGRN_HEADS, HGRN_HEAD_DIM, HGRN_HEAD_DIM, causal, sequential,
               read_h, write_h, o_ref, 0)
    act = _silu(conv)
    _ssd_heads(act[:, :, :SSM_INNER], act[:, :, SSM_INNER:SSM_INNER + SSM_GROUPS * SSM_STATE],
               act[:, :, SSM_INNER + SSM_GROUPS * SSM_STATE:], feat[:, :, _Z0:_Z0 + SSM_INNER],
               feat[:, :, _DT0:_DT0 + SSM_HEADS], -jnp.exp(alog_ref[...]), d_ref[...], sn_ref[...], causal,
               sequential, read_s, write_s, o_ref, HGRN_WIDTH)


def _ab_seq_kernel(proj_ref, cw_ref, cbias_ref, alog_ref, d_ref, sn_ref,
                   o_ref, sh_out, ss_out, sc_out, sh, ss, xpad):
    t = pl.program_id(1)
    nb, c, _ = o_ref.shape
    rows = nb * c

    @pl.when(t == 0)
    def _():
        sh[...] = jnp.zeros_like(sh)
        ss[...] = jnp.zeros_like(ss)
        xpad[0:SUBLANES, :] = jnp.zeros((SUBLANES, CONV_DIM), F32)

    xpad[SUBLANES:SUBLANES + rows, :] = proj_ref[:, :, _XBC0:_XBC0 + CONV_DIM].reshape(rows, CONV_DIM)
    padded = xpad[...]
    conv = cbias_ref[...] + padded[SUBLANES:] * cw_ref[CONV_W - 1:CONV_W, :]
    for d in range(1, CONV_W):
        conv = conv + pltpu.roll(padded, d, 0)[SUBLANES:] * cw_ref[CONV_W - 1 - d:CONV_W - d, :]
    xpad[0:SUBLANES, :] = padded[rows:rows + SUBLANES]
    conv = conv.reshape(nb, c, CONV_DIM)

    def read_h(h):
        return sh[h]

    def write_h(h, s):
        sh[h] = s

    def read_s(g):
        return ss[g]

    def write_s(g, s):
        ss[g] = s

    _ab_math(proj_ref, conv, alog_ref, d_ref, sn_ref, True, read_h, write_h, read_s, write_s, o_ref)

    @pl.when(t == pl.num_programs(1) - 1)
    def _():
        for h in range(HGRN_HEADS):
            sh_out[0, h] = sh[h].T
        for g in range(SSM_GROUPS):
            ss_out[0, g] = ss[g].T
        sc_out[0] = xpad[SUBLANES - (CONV_W - 1):SUBLANES, :]


def _ab_par_kernel(proj_ref, sh_in, ss_in, sc_in, cw_ref, cbias_ref, alog_ref, d_ref, sn_ref,
                   o_ref, sh_out, ss_out, sc_out, xpad, o3):
    nb, c, _ = o3.shape
    nbuf = CONV_W - 1
    proj = proj_ref[...].reshape(nb, c, proj_ref.shape[-1])
    xpad[:, 0:nbuf, :] = sc_in[...]
    xpad[:, nbuf:nbuf + c, :] = proj[:, :, _XBC0:_XBC0 + CONV_DIM]
    conv = cbias_ref[...]
    for k in range(CONV_W):
        conv = conv + xpad[:, k:k + c, :] * cw_ref[k:k + 1, :]
    sc_out[...] = xpad[:, c:c + nbuf, :]

    def read_h(h):
        return sh_in[:, h]

    def write_h(h, s):
        sh_out[:, h] = s

    def read_s(g):
        return ss_in[:, g]

    def write_s(g, s):
        ss_out[:, g] = s

    _ab_math(proj, conv, alog_ref, d_ref, sn_ref, False, read_h, write_h, read_s, write_s, o3)
    o_ref[...] = o3[...].reshape(o_ref.shape)


def _gla_seq_kernel(ready_ref, decay_ref, gate_ref, v_ref, o_ref, sg_out, sg):
    t = pl.program_id(1)

    @pl.when(t == 0)
    def _():
        sg[...] = jnp.zeros_like(sg)

    def read_g(h):
        return sg[h]

    def write_g(h, s):
        sg[h] = s

    nb, c, _ = o_ref.shape
    _gla_core(ready_ref[:, :, _GQD0:_GQD0 + GLA_KEY], ready_ref[:, :, _GKI0:_GKI0 + GLA_KEY],
              ready_ref[:, :, _GKE0:_GKE0 + GLA_KEY], decay_ref[...].reshape(nb, 1, GLA_KEY), None, v_ref[...],
              gate_ref, GLA_HEADS, GLA_HEAD_K, GLA_HEAD_V, _causal(nb, c), True, read_g, write_g, o_ref, 0)

    @pl.when(t == pl.num_programs(1) - 1)
    def _():
        for h in range(GLA_HEADS):
            sg_out[0, h] = sg[h].T


def _gla_par_kernel(proj_ref, gate_ref, v_ref, sg_in, o_ref, sg_out, o3):
    def read_g(h):
        return sg_in[:, h]

    def write_g(h, s):
        sg_out[:, h] = s

    nb, c, _ = o3.shape
    feat = proj_ref[...].reshape(nb, c, proj_ref.shape[-1])
    gate = gate_ref[...].astype(F32).reshape(nb, c, gate_ref.shape[-1])
    v = v_ref[...].astype(F32).reshape(nb, c, v_ref.shape[-1])
    _gla_heads(feat[:, :, _GQ0:_GQ0 + GLA_KEY], feat[:, :, _GK0:_GK0 + GLA_KEY], v,
               feat[:, :, _GLF0:_GLF0 + GLA_KEY], gate, GLA_HEADS, GLA_HEAD_K, GLA_HEAD_V, _causal(nb, c), False,
               read_g, write_g, o3, 0)
    o_ref[...] = o3[...].reshape(o_ref.shape)


SEQ_TILE_CHUNKS = 8
GLA_SEQ_TILE_CHUNKS = 16
PAR_TILE_SEQS = 16


def _full(shape):
    return pl.BlockSpec(shape, lambda *_: (0,) * len(shape))


def _mixer_o_dtype(c):
    return BF16 if c % (2 * SUBLANES) == 0 else F32


def _ab_seq_call(proj, batch, length, params, *, name):
    c = math.gcd(length, CHUNK)
    nb = SEQ_TILE_CHUNKS
    tiles = length // (c * nb)
    assert tiles * c * nb == length
    width = HEADS_PER_GROUP * SSM_HEAD_DIM
    blk = lambda n: pl.BlockSpec((nb, c, n), lambda b, t: (b * tiles + t, 0, 0))
    out_shapes = [
        jax.ShapeDtypeStruct((batch * length // c, c, AB_WIDTH), _mixer_o_dtype(c)),
        jax.ShapeDtypeStruct((batch, HGRN_HEADS, HGRN_HEAD_DIM, HGRN_HEAD_DIM), F32),
        jax.ShapeDtypeStruct((batch, SSM_GROUPS, width, SSM_STATE), F32),
        jax.ShapeDtypeStruct((batch, CONV_W - 1, CONV_DIM), F32),
    ]
    out_specs = [
        blk(AB_WIDTH),
        pl.BlockSpec((1, HGRN_HEADS, HGRN_HEAD_DIM, HGRN_HEAD_DIM), lambda b, t: (b, 0, 0, 0)),
        pl.BlockSpec((1, SSM_GROUPS, width, SSM_STATE), lambda b, t: (b, 0, 0, 0)),
        pl.BlockSpec((1, CONV_W - 1, CONV_DIM), lambda b, t: (b, 0, 0)),
    ]
    return pl.pallas_call(
        _ab_seq_kernel,
        grid=(batch, tiles),
        in_specs=[blk(AB_FEAT)] + [_full(p.shape) for p in params],
        out_specs=out_specs,
        out_shape=out_shapes,
        scratch_shapes=[
            pltpu.VMEM((HGRN_HEADS, HGRN_HEAD_DIM, HGRN_HEAD_DIM), F32),
            pltpu.VMEM((SSM_GROUPS, SSM_STATE, width), F32),
            pltpu.VMEM((nb * c + SUBLANES, CONV_DIM), F32),
        ],
        compiler_params=pltpu.CompilerParams(dimension_semantics=("arbitrary", "arbitrary"),
                                             vmem_limit_bytes=VMEM_LIMIT_BYTES),
        name=name,
    )(proj, *params)


def _ab_par_call(proj, row0, c, s_hgrn, s_ssm, s_conv, params, *, name):
    batch = s_hgrn.shape[0]
    nb = _row_tile(batch, PAR_TILE_SEQS)
    rows = nb * c
    assert row0 % rows == 0
    width = HEADS_PER_GROUP * SSM_HEAD_DIM
    s_ssm = s_ssm.reshape(batch, SSM_GROUPS, width, SSM_STATE)
    blk3 = lambda a, n: pl.BlockSpec((nb, a, n), lambda b: (b, 0, 0))
    blk4 = lambda a, r, n: pl.BlockSpec((nb, a, r, n), lambda b: (b, 0, 0, 0))
    state_specs = [blk4(HGRN_HEADS, HGRN_HEAD_DIM, HGRN_HEAD_DIM), blk4(SSM_GROUPS, width, SSM_STATE),
                   blk3(CONV_W - 1, CONV_DIM)]
    out_shapes = [
        jax.ShapeDtypeStruct((batch * c, AB_WIDTH), F32),
        jax.ShapeDtypeStruct(s_hgrn.shape, F32),
        jax.ShapeDtypeStruct(s_ssm.shape, F32),
        jax.ShapeDtypeStruct(s_conv.shape, F32),
    ]
    return pl.pallas_call(
        _ab_par_kernel,
        grid=(batch // nb,),
        in_specs=[pl.BlockSpec((rows, proj.shape[1]), lambda b: (row0 // rows + b, 0))] + state_specs
                 + [_full(p.shape) for p in params],
        out_specs=[pl.BlockSpec((rows, AB_WIDTH), lambda b: (b, 0))] + state_specs,
        out_shape=out_shapes,
        scratch_shapes=[pltpu.VMEM((nb, c + CONV_W - 1, CONV_DIM), F32), pltpu.VMEM((nb, c, AB_WIDTH), F32)],
        compiler_params=pltpu.CompilerParams(dimension_semantics=("arbitrary",),
                                             vmem_limit_bytes=VMEM_LIMIT_BYTES),
        name=name,
    )(proj, s_hgrn, s_ssm, s_conv, *params)


def _gla_seq_call(ready, decay, gate, v, batch, length, *, name):
    c = math.gcd(length, CHUNK)
    assert c == CHUNK
    nb = GLA_SEQ_TILE_CHUNKS
    tiles = length // (c * nb)
    assert tiles * c * nb == length
    blk = lambda n: pl.BlockSpec((nb, c, n), lambda b, t: (b * tiles + t, 0, 0))
    return pl.pallas_call(
        _gla_seq_kernel,
        grid=(batch, tiles),
        in_specs=[blk(GLA_READY), pl.BlockSpec((nb, GLA_KEY), lambda b, t: (b * tiles + t, 0)), blk(GLA_VAL),
                  blk(GLA_VAL)],
        out_specs=[blk(GLA_VAL),
                   pl.BlockSpec((1, GLA_HEADS, GLA_HEAD_K, GLA_HEAD_V), lambda b, t: (b, 0, 0, 0))],
        out_shape=[jax.ShapeDtypeStruct((batch * length // c, c, GLA_VAL), _mixer_o_dtype(c)),
                   jax.ShapeDtypeStruct((batch, GLA_HEADS, GLA_HEAD_K, GLA_HEAD_V), F32)],
        scratch_shapes=[pltpu.VMEM((GLA_HEADS, GLA_HEAD_V, GLA_HEAD_K), F32)],
        compiler_params=pltpu.CompilerParams(dimension_semantics=("arbitrary", "arbitrary"),
                                             vmem_limit_bytes=VMEM_LIMIT_BYTES),
        name=name,
    )(ready, decay, gate, v)


def _gla_par_call(proj, gate, v, row0, c, s_gla, *, name):
    batch = s_gla.shape[0]
    nb = _row_tile(batch, PAR_TILE_SEQS)
    rows = nb * c
    assert row0 % rows == 0
    sspec = pl.BlockSpec((nb, GLA_HEADS, GLA_HEAD_K, GLA_HEAD_V), lambda b: (b, 0, 0, 0))
    return pl.pallas_call(
        _gla_par_kernel,
        grid=(batch // nb,),
        in_specs=[pl.BlockSpec((rows, a.shape[1]), lambda b: (row0 // rows + b, 0)) for a in (proj, gate, v)] + [sspec],
        out_specs=[pl.BlockSpec((rows, GLA_VAL), lambda b: (b, 0)), sspec],
        out_shape=[jax.ShapeDtypeStruct((batch * c, GLA_VAL), F32), jax.ShapeDtypeStruct(s_gla.shape, F32)],
        scratch_shapes=[pltpu.VMEM((nb, c, GLA_VAL), F32)],
        compiler_params=pltpu.CompilerParams(dimension_semantics=("arbitrary",),
                                             vmem_limit_bytes=VMEM_LIMIT_BYTES),
        name=name,
    )(proj, gate, v, s_gla)


def _split_cols(w):
    main = w.shape[1] // LANES * LANES
    tail = jnp.pad(w[:, main:].astype(BF16), ((0, 0), (0, LANES - (w.shape[1] - main))))
    return w[:, :main].astype(BF16), tail


def kernel(x_prompt, x_sample, state_hgrn, state_ssm, state_conv, state_gla, norm_ffn1, norm_mix, norm_ffn2, norm_final, ffn1_w_in, ffn1_w_out, ffn2_w_in, ffn2_w_out, ab_w_in, ab_w_out, hgrn_lb_logits, hgrn_norm, ssm_conv_w, ssm_conv_b, ssm_dt_bias, ssm_a_log, ssm_d, ssm_norm, gla_w_in, gla_w_gk, gla_b_gk, gla_norm, gla_w_out):
    assert ab_w_in.shape[0] == 1 and gla_w_in.shape[0] == 1, "one HGRN2/SSD layer and one GLA layer"
    bp, lp, _ = x_prompt.shape
    bs, ls, _ = x_sample.shape
    rp, rs = bp * lp, bs * ls
    cp, cs = math.gcd(lp, CHUNK), math.gcd(ls, CHUNK)
    assert rs % cp == 0
    row = lambda v: v.reshape(1, -1)
    stack_row = lambda v: v.reshape(v.shape[0], 1, v.shape[1])
    nf1, nm, nf2 = stack_row(norm_ffn1), stack_row(norm_mix), stack_row(norm_ffn2)
    in_rows, out_rows = D_MODEL // 64, D_FF // 16
    xp = x_prompt.reshape(rp, D_MODEL)
    xs = x_sample.reshape(rs, D_MODEL)

    wp, wt = _split_cols(ab_w_in[0])
    ab_feat_params = (hgrn_lb_logits, jnp.tile(hgrn_norm[0], HGRN_HEADS).reshape(1, -1),
                      jnp.pad(ssm_dt_bias[0], (0, LANES - SSM_HEADS)).reshape(1, -1))
    casts = (_Cast(ffn2_w_in, in_rows), _Cast(ffn2_w_out, out_rows), _Cast(ffn1_w_in, in_rows, layer=1),
             _Cast(ffn1_w_out, out_rows, layer=1), _Cast(ab_w_out, in_rows, layer=0),
             _Cast(gla_w_in[0], in_rows, split=True), _Cast(gla_w_out, in_rows, layer=0))
    (x1, feat, w2_in, w2_out, w1_in, w1_out, ab_wo, gla_wp, gla_wt, gla_wo) = _pre_call(
        xp, xs, 0, nf1, ffn1_w_in[0].astype(BF16), ffn1_w_out[0].astype(BF16), nm, wp, wt, ab_feat_params, casts,
        mode="ab", row_tile=PRE0_ROWS, name="pre0")
    ab_params = (ssm_conv_w[0], row(ssm_conv_b[0]), row(ssm_a_log[0]), row(ssm_d[0]), row(ssm_norm[0]))
    o_p, hgrn_p, ssm_p, conv_p = _ab_seq_call(feat.reshape((rp + rs) // cp, cp, AB_FEAT), bp, lp, ab_params,
                                              name="mix0_prompt")
    o_s, hgrn_s, ssm_s, conv_s = _ab_par_call(feat, rp, cs, state_hgrn[0], state_ssm[0], state_conv[0], ab_params,
                                              name="mix0_sample")
    (x3,) = _post_call(x1, o_p.reshape(rp, AB_WIDTH), o_s, 0, ab_wo, nf2,
                       w2_in, w2_out, row(norm_final), split_output=False, name="post0")

    gla_feat_params = (gla_w_gk[0].astype(BF16), row(gla_b_gk[0]), jnp.tile(gla_norm[0], GLA_HEADS).reshape(1, -1))
    x4, feat, gate, v16, ready, decay = _pre_call(x3, None, 1, nf1, w1_in, w1_out, nm, gla_wp, gla_wt, gla_feat_params,
                                                  mode="gla", row_tile=PRE1_ROWS, name="pre1")
    chunked = lambda a: a.reshape((rp + rs) // cp, cp, a.shape[1])
    o_p, gla_p = _gla_seq_call(chunked(ready), decay, chunked(gate), chunked(v16), bp, lp, name="mix1_prompt")
    o_s, gla_s = _gla_par_call(feat, gate, v16, rp, cs, state_gla[0], name="mix1_sample")
    y_p, y_s = _post_call(x4, o_p.reshape(rp, GLA_VAL), o_s, 1, gla_wo, nf2,
                          w2_in, w2_out, row(norm_final), split_output=True, name="post1")

    ssm_shape = (1, -1, SSM_HEADS, SSM_HEAD_DIM, SSM_STATE)
    return (y_p.reshape(bp, lp, D_MODEL), y_s.reshape(bs, ls, D_MODEL), hgrn_p[None], hgrn_s[None],
            ssm_p.reshape(ssm_shape), ssm_s.reshape(ssm_shape), conv_p[None], conv_s[None], gla_p[None], gla_s[None])
```

```python
import functools
import math

import jax
import jax.numpy as jnp
from jax import lax
from jax.experimental import pallas as pl
from jax.experimental.pallas import tpu as pltpu

F32 = jnp.float32
BF16 = jnp.bfloat16

D_MODEL = 1024
D_FF = 2816
EPS = 1e-6
CHUNK = 64

HGRN_HEADS = 4
HGRN_HEAD_DIM = 128
HGRN_WIDTH = HGRN_HEADS * HGRN_HEAD_DIM

SSM_HEADS = 8
SSM_HEAD_DIM = 64
SSM_INNER = SSM_HEADS * SSM_HEAD_DIM
SSM_GROUPS = 2
SSM_STATE = 128
SSM_GROUP_WIDTH = SSM_INNER // SSM_GROUPS
HEADS_PER_GROUP = SSM_HEADS // SSM_GROUPS
CONV_W = 4
CONV_DIM = SSM_INNER + 2 * SSM_GROUPS * SSM_STATE
AB_WIDTH = HGRN_WIDTH + SSM_INNER

GLA_HEADS = 4
GLA_HEAD_K = 128
GLA_HEAD_V = 256
GLA_KEY = GLA_HEADS * GLA_HEAD_K
GLA_VAL = GLA_HEADS * GLA_HEAD_V
GK_RANK = 16
GK_NORMALIZER = 16.0

LANES = 128
SUBLANES = 8
VMEM_LIMIT_BYTES = 56 * 1024 * 1024

_Q0, _K0, _LF0, _V0, _G0 = (i * HGRN_WIDTH for i in range(5))
_Z0 = 5 * HGRN_WIDTH
_XBC0 = _Z0 + SSM_INNER
_DT0 = _XBC0 + CONV_DIM
AB_FEAT = _DT0 + LANES
_WQ, _WF, _WI, _WG, _WZ = (i * HGRN_WIDTH for i in range(5))
_WXBC = 4 * HGRN_WIDTH + SSM_INNER

_GQ0, _GK0, _GLF0 = 0, GLA_KEY, 2 * GLA_KEY
GLA_FEAT = 3 * GLA_KEY
_GQD0, _GKI0, _GKE0 = 0, GLA_KEY, 2 * GLA_KEY
GLA_READY = 3 * GLA_KEY
_GWQ, _GWK = 0, GLA_KEY
_GWV = 2 * GLA_KEY
_GWG = _GWV + GLA_VAL


def _rms(x, w):
    return x * lax.rsqrt(jnp.mean(x * x, axis=-1, keepdims=True) + EPS) * w


def _rms_core(x):
    return x * lax.rsqrt(jnp.mean(x * x, axis=-1, keepdims=True) + EPS)


def _silu(x):
    return x * jax.nn.sigmoid(x)


def _softplus(x):
    return jnp.maximum(x, 0.0) + jnp.log1p(jnp.exp(-jnp.abs(x)))


def _dot(a, b):
    return jnp.dot(a, b, preferred_element_type=F32)


def _bdot(a, b, ca, cb):
    return lax.dot_general(a, b, (((ca,), (cb,)), ((0,), (0,))), preferred_element_type=F32)


def _split3(x):
    hi = x.astype(BF16)
    r1 = x - hi.astype(F32)
    mid = r1.astype(BF16)
    lo = (r1 - mid.astype(F32)).astype(BF16)
    return hi, mid, lo


def _exact_bdot_lhs01(m01, x, ca, cb):
    return sum(_bdot(m01, p, ca, cb) for p in _split3(x))


def _exact_bdot_rhs01(x, m01, ca, cb):
    return sum(_bdot(p, m01, ca, cb) for p in _split3(x))


def _causal(nb, c):
    r = lax.broadcasted_iota(jnp.int32, (nb, c, c), 1)
    col = lax.broadcasted_iota(jnp.int32, (nb, c, c), 2)
    return r >= col


BF16_ROWS = 2 * SUBLANES


def _triangle3(nb, c, wide_axis):
    shape = (1, c, 3 * c) if wide_axis == 2 else (1, 3 * c, c)
    wide = lax.broadcasted_iota(jnp.int32, shape, wide_axis)
    narrow = lax.broadcasted_iota(jnp.int32, shape, 3 - wide_axis)
    hit = None
    for k in range(3):
        wk = wide - k * c
        term = (wk >= 0) & (wk < c) & (wk <= narrow)
        hit = term if hit is None else hit | term
    return jnp.broadcast_to(hit.astype(BF16), (nb,) + shape[1:])


def _chunk_cumsum(x, causal):
    nb, c, _ = x.shape
    if c % BF16_ROWS:
        return _exact_bdot_lhs01(causal.astype(BF16), x, 2, 1)
    return _bdot(_triangle3(nb, c, 2), jnp.concatenate(_split3(x), axis=1), 2, 1)


def _chunk_cumsum_t(x):
    nb, c, _ = x.shape
    if c % BF16_ROWS:
        return _exact_bdot_rhs01(x, _upper(nb, c), 1, 1)
    return _bdot(jnp.concatenate(_split3(x), axis=1), _triangle3(nb, c, 1), 1, 1)


def _col_bcast(row, lanes=LANES):
    nb, _, k = row.shape
    hi, mid, lo = (p.astype(F32) for p in _split3(row))
    r = lax.broadcasted_iota(jnp.int32, (1, BF16_ROWS, k), 1)
    stacked = jnp.where(r == 0, hi, jnp.where(r == 1, mid, jnp.where(r == 2, lo, 0.0))).astype(BF16)
    return _bdot(stacked, jnp.ones((nb, BF16_ROWS, lanes), BF16), 1, 1)


MXU_DIM = 256
FF_TILES = ((0, 6 * MXU_DIM), (6 * MXU_DIM, D_FF))
assert all(lo % MXU_DIM == 0 and hi % MXU_DIM == 0 for lo, hi in FF_TILES)


def _ffn_half(x, norm_w, w_in_ref, w_out_ref):
    hb = _rms(x, norm_w).astype(BF16)
    acc = None
    for lo, hi in FF_TILES:
        gate = _dot(hb, w_in_ref[:, lo:hi])
        up = _dot(hb, w_in_ref[:, D_FF + lo:D_FF + hi])
        act = (_silu(gate) * up).astype(BF16)
        part = _dot(act, w_out_ref[lo:hi, :])
        acc = part if acc is None else acc + part
    return 0.5 * acc


def _pick_group(first_steps, a, b):
    return jnp.where(pl.program_id(0) < first_steps, a, b)


def _cast_rows(src_ref, *dst_refs):
    if len(dst_refs) == 1:
        dst_refs[0][...] = src_ref[...].astype(BF16)
        return
    main_ref, tail_ref = dst_refs
    main = main_ref.shape[-1]
    rest = src_ref.shape[-1] - main
    main_ref[...] = src_ref[:, :main].astype(BF16)
    tail_ref[...] = jnp.zeros(tail_ref.shape, BF16)
    tail_ref[:, :rest] = src_ref[:, main:].astype(BF16)


def _ab_features(hb, wp_ref, wt_ref, lb_ref, hn_ref, dtb_ref, feat_ref):
    def group(w0):
        return _dot(hb, wp_ref[:, w0:w0 + HGRN_WIDTH])

    def put(c0, val):
        feat_ref[:, c0:c0 + val.shape[-1]] = val

    lb = _lower_bound(lb_ref[...])
    f = lb + (1.0 - lb) * jax.nn.sigmoid(group(_WF))
    put(_K0, 1.0 - f)
    put(_LF0, jnp.log(f))
    put(_Q0, _silu(group(_WQ)))
    put(_G0, _silu(group(_WG)) * hn_ref[...])
    put(_Z0, _silu(group(_WZ)))
    put(_DT0, _softplus(_dot(hb, wt_ref[...]) + dtb_ref[...]))
    put(_XBC0, _dot(hb, wp_ref[:, _WXBC:_WXBC + CONV_DIM]))
    put(_V0, group(_WI))


def _decayed_operands(q, k, log_f):
    nb, c, _ = q.shape
    g = _chunk_cumsum(log_f, _causal(nb, c))
    g_last = g[:, c - 1:c, :]
    q_dec = (q * jnp.exp(g)).astype(BF16)
    k_inv_f = k * jnp.exp(-g)
    decay = jnp.exp(g_last)
    k_end = (k_inv_f * decay).astype(BF16)
    return q_dec, k_inv_f.astype(BF16), k_end, decay, g_last


def _gla_features(hb, wp_ref, wt_ref, wgk_ref, bgk_ref, gn_ref, feat_ref, gate_ref, v_ref, ready_ref, decay_ref):
    def put(c0, val):
        feat_ref[:, c0:c0 + val.shape[-1]] = val

    rows = hb.shape[0]
    nb = rows // CHUNK
    gk_low = _dot(hb, wt_ref[...])[:, :GK_RANK].astype(BF16)
    gk = _dot(gk_low, wgk_ref[...]) + bgk_ref[...]
    log_f = -_softplus(-gk) / GK_NORMALIZER
    q = _dot(hb, wp_ref[:, _GWQ:_GWQ + GLA_KEY]) * (GLA_HEAD_K ** -0.5)
    k = _dot(hb, wp_ref[:, _GWK:_GWK + GLA_KEY])
    put(_GLF0, log_f)
    put(_GQ0, q)
    put(_GK0, k)
    chunked = lambda t: t.reshape(nb, CHUNK, GLA_KEY)
    q_dec, k_inv, k_end, decay, _ = _decayed_operands(chunked(q), chunked(k), chunked(log_f))
    for c0, val in ((_GQD0, q_dec), (_GKI0, k_inv), (_GKE0, k_end)):
        ready_ref[:, c0:c0 + GLA_KEY] = val.reshape(rows, GLA_KEY)
    decay_ref[...] = decay.reshape(nb, GLA_KEY)
    gate_ref[...] = (_silu(_dot(hb, wp_ref[:, _GWG:_GWG + GLA_VAL])) * gn_ref[...]).astype(BF16)
    v_ref[...] = _dot(hb, wp_ref[:, _GWV:_GWV + GLA_VAL]).astype(BF16)


_FEATURES = {
    "ab": (_ab_features, ((AB_FEAT, F32, 1),)),
    "gla": (_gla_features, ((GLA_FEAT, F32, 1), (GLA_VAL, BF16, 1), (GLA_VAL, BF16, 1), (GLA_READY, BF16, 1),
                            (GLA_KEY, F32, CHUNK))),
}
N_FEATURE_PARAMS = 3


def _pre_kernel(*refs, first_steps, cast_arity, mode):
    n_x = 1 if first_steps is None else 2
    x_refs, refs = refs[:n_x], refs[n_x:]
    nf_ref, w_in_ref, w_out_ref, nm_ref, wp_ref, wt_ref = refs[:6]
    feat_params, refs = refs[6:6 + N_FEATURE_PARAMS], refs[6 + N_FEATURE_PARAMS:]
    feature_fn, feature_outs = _FEATURES[mode]
    cast_src, refs = refs[:len(cast_arity)], refs[len(cast_arity):]
    x1_ref, feat_refs, cast_dst = refs[0], refs[1:1 + len(feature_outs)], list(refs[1 + len(feature_outs):])
    if first_steps is None:
        x = x_refs[0][...]
    else:
        x = _pick_group(first_steps, x_refs[0][...], x_refs[1][...])
    x1 = x + _ffn_half(x, nf_ref[...], w_in_ref, w_out_ref)
    x1_ref[...] = x1
    hb = _rms(x1, nm_ref[...]).astype(BF16)
    feature_fn(hb, wp_ref, wt_ref, *feat_params, *feat_refs)
    for src_ref, arity in zip(cast_src, cast_arity):
        _cast_rows(src_ref, *cast_dst[:arity])
        cast_dst = cast_dst[arity:]


def _post_kernel(x_ref, oa_ref, ob_ref, wo_ref, nf_ref, w_in_ref, w_out_ref, nfin_ref, *y_refs, first_steps):
    o = _pick_group(first_steps, oa_ref[...].astype(BF16), ob_ref[...].astype(BF16))
    x2 = x_ref[...] + _dot(o, wo_ref[...])
    y = x2 + _ffn_half(x2, nf_ref[...], w_in_ref, w_out_ref)
    if len(y_refs) == 1:
        y_refs[0][...] = y
    else:
        y = _rms(y, nfin_ref[...])
        ya_ref, yb_ref = y_refs

        @pl.when(pl.program_id(0) < first_steps)
        def _():
            ya_ref[...] = y

        @pl.when(pl.program_id(0) >= first_steps)
        def _():
            yb_ref[...] = y


def _resident(shape, layer=None):
    if layer is None:
        return pl.BlockSpec(shape, lambda *_: (0,) * len(shape), pipeline_mode=pl.Buffered(1))
    return pl.BlockSpec((None,) + tuple(shape[1:]), lambda *_: (layer,) + (0,) * (len(shape) - 1),
                        pipeline_mode=pl.Buffered(1))


def _row_tile(rows, want):
    t = min(rows, want)
    assert rows % t == 0
    return t


PRE0_ROWS = 256
PRE1_ROWS = 512
POST_ROWS = 512


def _group_specs(tm, first_steps, width):
    first = pl.BlockSpec((tm, width), lambda i: (jnp.minimum(i, first_steps - 1), 0))
    second = pl.BlockSpec((tm, width), lambda i: (jnp.maximum(i - first_steps, 0), 0))
    return first, second


def _weight_spec(w, layer):
    return _resident(w.shape, layer if w.ndim == 3 else None)


class _Cast:
    def __init__(self, src, rows_per_step, layer=None, split=False):
        self.src, self.rps, self.layer, self.split = src, rows_per_step, layer, split
        rows = src.shape[-2]
        assert rows % rows_per_step == 0 and rows_per_step % (2 * SUBLANES) == 0
        self.steps = rows // rows_per_step
        assert not split or layer is not None or src.ndim == 2

    def _index(self, lead):
        last = self.steps - 1
        return lambda i: lead + (jnp.minimum(i, last), 0)

    def in_spec(self):
        cols = self.src.shape[-1]
        if self.src.ndim == 2:
            return pl.BlockSpec((self.rps, cols), self._index(()))
        if self.layer is None:
            return pl.BlockSpec((self.src.shape[0], self.rps, cols), self._index((0,)))
        return pl.BlockSpec((None, self.rps, cols), self._index((self.layer,)))

    def outs(self):
        rows, cols = self.src.shape[-2:]
        if self.src.ndim == 3 and self.layer is None:
            n = self.src.shape[0]
            return [(jax.ShapeDtypeStruct((n, rows, cols), BF16), pl.BlockSpec((n, self.rps, cols), self._index((0,))))]
        widths = [cols // LANES * LANES, LANES] if self.split else [cols]
        return [(jax.ShapeDtypeStruct((rows, w), BF16), pl.BlockSpec((self.rps, w), self._index(()))) for w in widths]


def _pre_call(xa, xb, layer, nf, w_in, w_out, nm, wp, wt, feat_params, casts=(), *, mode, row_tile, name):
    ra = xa.shape[0]
    rb = 0 if xb is None else xb.shape[0]
    rows = ra + rb
    tm = _row_tile(rb if rb else ra, row_tile)
    assert ra % tm == 0
    assert all(cast.steps <= rows // tm for cast in casts)
    assert len(feat_params) == N_FEATURE_PARAMS
    feature_outs = _FEATURES[mode][1]
    tok = lambda n: pl.BlockSpec((tm, n), lambda i: (i, 0))
    if xb is None:
        first_steps, x_specs, xs = None, [tok(D_MODEL)], (xa,)
    else:
        first_steps = ra // tm
        x_specs, xs = list(_group_specs(tm, first_steps, D_MODEL)), (xa, xb)
    cast_outs = [cast.outs() for cast in casts]
    flat_outs = [o for outs in cast_outs for o in outs]
    return pl.pallas_call(
        functools.partial(_pre_kernel, first_steps=first_steps, cast_arity=tuple(len(o) for o in cast_outs),
                          mode=mode),
        grid=(rows // tm,),
        in_specs=x_specs + [_resident(nf.shape, layer), _weight_spec(w_in, layer), _weight_spec(w_out, layer),
                            _resident(nm.shape, layer), _resident(wp.shape), _resident(wt.shape)]
                         + [_resident(p.shape) for p in feat_params] + [cast.in_spec() for cast in casts],
        out_specs=[tok(D_MODEL)] + [pl.BlockSpec((tm // div, w), lambda i: (i, 0)) for w, _, div in feature_outs]
                  + [spec for _, spec in flat_outs],
        out_shape=[jax.ShapeDtypeStruct((rows, D_MODEL), F32)]
                  + [jax.ShapeDtypeStruct((rows // div, w), dt) for w, dt, div in feature_outs]
                  + [shape for shape, _ in flat_outs],
        compiler_params=pltpu.CompilerParams(dimension_semantics=("arbitrary",), vmem_limit_bytes=VMEM_LIMIT_BYTES),
        name=name,
    )(*xs, nf, w_in, w_out, nm, wp, wt, *feat_params, *[cast.src for cast in casts])


def _post_call(x, oa, ob, layer, wo, nf, w_in, w_out, nfin, *, split_output, name):
    rows = x.shape[0]
    ra, rb = oa.shape[0], ob.shape[0]
    assert ra + rb == rows
    tm = _row_tile(rb, POST_ROWS)
    assert ra % tm == 0
    first_steps = ra // tm
    tok = lambda n: pl.BlockSpec((tm, n), lambda i: (i, 0))
    spec_a, spec_b = _group_specs(tm, first_steps, oa.shape[1])
    if split_output:
        out_specs = list(_group_specs(tm, first_steps, D_MODEL))
        out_shape = [jax.ShapeDtypeStruct((ra, D_MODEL), F32), jax.ShapeDtypeStruct((rb, D_MODEL), F32)]
    else:
        out_specs = [tok(D_MODEL)]
        out_shape = [jax.ShapeDtypeStruct((rows, D_MODEL), F32)]
    return pl.pallas_call(
        functools.partial(_post_kernel, first_steps=first_steps),
        grid=(rows // tm,),
        in_specs=[tok(D_MODEL), spec_a, spec_b, _resident(wo.shape), _resident(nf.shape, layer),
                  _resident(w_in.shape, layer), _resident(w_out.shape, layer), _resident(nfin.shape)],
        out_specs=out_specs,
        out_shape=out_shape,
        compiler_params=pltpu.CompilerParams(dimension_semantics=("arbitrary",), vmem_limit_bytes=VMEM_LIMIT_BYTES),
        name=name,
    )(x, oa, ob, wo, nf, w_in, w_out, nfin)


def _gla_heads(q, k, v, log_f, gate, n_heads, dk, dv, causal, sequential, read_state, write_state,
               o_ref, o_col0):
    q_dec, k_inv, k_end, decay, g_last = _decayed_operands(q, k, log_f)
    _gla_core(q_dec, k_inv, k_end, decay, g_last, v, gate, n_heads, dk, dv, causal, sequential, read_state,
              write_state, o_ref, o_col0)


def _gla_core(q_dec, k_inv, k_end, decay, g_last, v, gate, n_heads, dk, dv, causal, sequential, read_state,
              write_state, o_ref, o_col0):
    nb, c, _ = q_dec.shape
    vb = v.astype(BF16)
    heads = range(n_heads)
    ks = [slice(h * dk, (h + 1) * dk) for h in heads]
    vs = [slice(h * dv, (h + 1) * dv) for h in heads]
    scores = [_bdot(q_dec[:, :, ks[h]], k_inv[:, :, ks[h]], 2, 2) for h in heads]
    scores = [jnp.where(causal, sc, 0.0).astype(BF16) for sc in scores]
    o_intra = [_bdot(scores[h], vb[:, :, vs[h]], 2, 1) for h in heads]

    def emit(h, b, o):
        cols = slice(o_col0 + h * dv, o_col0 + (h + 1) * dv)
        o_ref[b, :, cols] = (_rms_core(o) * gate[b, :, vs[h]]).astype(o_ref.dtype)

    if sequential:
        kv_t = [_bdot(vb[:, :, vs[h]], k_end[:, :, ks[h]], 1, 1) for h in heads]
        for b in range(nb):
            for h in heads:
                s_t = read_state(h)
                o_inter = lax.dot_general(q_dec[b, :, ks[h]], s_t.astype(BF16), (((1,), (1,)), ((), ())),
                                          preferred_element_type=F32)
                write_state(h, decay[b, :, ks[h]] * s_t + kv_t[h][b])
                emit(h, b, o_intra[h][b] + o_inter)
    else:
        for h in heads:
            kv = _bdot(k_end[:, :, ks[h]], vb[:, :, vs[h]], 1, 1)
            decay_col = jnp.exp(_col_bcast(g_last[:, :, ks[h]]))
            decay_col = jnp.concatenate([decay_col] * (dv // LANES), axis=-1)
            s0 = read_state(h)
            o_inter = _bdot(q_dec[:, :, ks[h]], s0.astype(BF16), 2, 1)
            write_state(h, decay_col * s0 + kv)
            emit(h, slice(None), o_intra[h] + o_inter)


def _expand_heads(x, expand):
    return _exact_bdot_rhs01(x, expand, 2, 1)


def _ssd_heads(xs, bs, cs, z, dt, a_row, d_row, norm_w, causal, sequential, read_state, write_state,
               o_ref, o_col0):
    nb, c, _ = xs.shape
    hrow = lax.broadcasted_iota(jnp.int32, (SSM_HEADS, SSM_INNER), 0)
    hcol = lax.broadcasted_iota(jnp.int32, (SSM_HEADS, SSM_INNER), 1) // SSM_HEAD_DIM
    expand2d = (hrow == hcol).astype(BF16)
    expand = jnp.broadcast_to(expand2d[None], (nb, SSM_HEADS, SSM_INNER))
    d_x = sum(_dot(p, expand2d) for p in _split3(d_row))
    dta = dt * a_row
    cum = _chunk_cumsum(dta, causal)
    cum_t = _chunk_cumsum_t(dta)
    dt_x = _expand_heads(dt, expand)
    cum_x = _expand_heads(cum, expand)
    xdt = xs * dt_x
    cum_last = cum_x[:, c - 1:c, :]
    x_end = (xdt * jnp.exp(cum_last - cum_x)).astype(BF16)
    chunk_dec = jnp.exp(cum_x)
    bsb = bs.astype(BF16)
    csb = cs.astype(BF16)
    lane_head = lax.broadcasted_iota(jnp.int32, (nb, c, SSM_GROUP_WIDTH), 2) // SSM_HEAD_DIM
    groups = range(SSM_GROUPS)
    gl = [slice(g * SSM_STATE, (g + 1) * SSM_STATE) for g in groups]
    hl = [slice(g * SSM_GROUP_WIDTH, (g + 1) * SSM_GROUP_WIDTH) for g in groups]
    cb = [_bdot(csb[:, :, gl[g]], bsb[:, :, gl[g]], 2, 2) for g in groups]
    y_intra = []
    for g in groups:
        xdt_g = xdt[:, :, hl[g]]
        y = None
        for r in range(HEADS_PER_GROUP):
            h = g * HEADS_PER_GROUP + r
            col = jnp.broadcast_to(cum[:, :, h:h + 1], (nb, c, c))
            row = cum_t[:, h:h + 1, :]
            dec = jnp.where(causal, jnp.exp(col - row), 0.0)
            lmat = (cb[g] * dec).astype(BF16)
            xm = jnp.where(lane_head == r, xdt_g, 0.0).astype(BF16)
            part = _bdot(lmat, xm, 2, 1)
            y = part if y is None else y + part
        y_intra.append(y)

    def finish(g, y_in, y_inter, b):
        y_all = y_in + y_inter * chunk_dec[b, :, hl[g]] + d_x[:, hl[g]] * xs[b, :, hl[g]]
        y_all = _rms(y_all * z[b, :, hl[g]], norm_w[:, hl[g]])
        cols = slice(o_col0 + g * SSM_GROUP_WIDTH, o_col0 + (g + 1) * SSM_GROUP_WIDTH)
        o_ref[b, :, cols] = y_all.astype(o_ref.dtype)

    if sequential:
        kv_t = [_bdot(bsb[:, :, gl[g]], x_end[:, :, hl[g]], 1, 1) for g in groups]
        for b in range(nb):
            for g in groups:
                s_t = read_state(g)
                y_inter = _dot(csb[b, :, gl[g]], s_t.astype(BF16))
                write_state(g, chunk_dec[b, c - 1:c, hl[g]] * s_t + kv_t[g][b])
                finish(g, y_intra[g][b], y_inter, b)
    else:
        for g in groups:
            kv = _bdot(x_end[:, :, hl[g]], bsb[:, :, gl[g]], 1, 1)
            decay = jnp.exp(_col_bcast(cum_last[:, :, hl[g]], SSM_STATE))
            s0 = read_state(g)
            y_inter = _bdot(csb[:, :, gl[g]], s0.astype(BF16), 2, 2)
            write_state(g, decay * s0 + kv)
            finish(g, y_intra[g], y_inter, slice(None))


def _upper(nb, c):
    r = lax.broadcasted_iota(jnp.int32, (nb, c, c), 1)
    col = lax.broadcasted_iota(jnp.int32, (nb, c, c), 2)
    return (r <= col).astype(BF16)


def _lower_bound(lb_logits):
    m = jnp.max(lb_logits, axis=0, keepdims=True)
    e = jnp.exp(lb_logits - m)
    return e[0:1, :] / jnp.sum(e, axis=0, keepdims=True)


def _ab_math(feat, conv, alog_ref, d_ref, sn_ref, sequential, read_h, write_h, read_s, write_s, o_ref):
    nb, c, _ = o_ref.shape
    causal = _causal(nb, c)
    w = HGRN_WIDTH
    _gla_heads(feat[:, :, _Q0:_Q0 + w], feat[:, :, _K0:_K0 + w], feat[:, :, _V0:_V0 + w], feat[:, :, _LF0:_LF0 + w],
               feat[:, :, _G0:_G0 + w], HGRN_HEADS, HGRN_HEAD_DIM, HGRN_HEAD_DIM, causal, sequential,
               read_h, write_h, o_ref, 0)
    act = _silu(conv)
    _ssd_heads(act[:, :, :SSM_INNER], act[:, :, SSM_INNER:SSM_INNER + SSM_GROUPS * SSM_STATE],
               act[:, :, SSM_INNER + SSM_GROUPS * SSM_STATE:], feat[:, :, _Z0:_Z0 + SSM_INNER],
               feat[:, :, _DT0:_DT0 + SSM_HEADS], -jnp.exp(alog_ref[...]), d_ref[...], sn_ref[...], causal,
               sequential, read_s, write_s, o_ref, HGRN_WIDTH)


def _ab_seq_kernel(proj_ref, cw_ref, cbias_ref, alog_ref, d_ref, sn_ref,
                   o_ref, sh_out, ss_out, sc_out, sh, ss, xpad):
    t = pl.program_id(1)
    nb, c, _ = o_ref.shape
    rows = nb * c

    @pl.when(t == 0)
    def _():
        sh[...] = jnp.zeros_like(sh)
        ss[...] = jnp.zeros_like(ss)
        xpad[0:SUBLANES, :] = jnp.zeros((SUBLANES, CONV_DIM), F32)

    xpad[SUBLANES:SUBLANES + rows, :] = proj_ref[:, :, _XBC0:_XBC0 + CONV_DIM].reshape(rows, CONV_DIM)
    padded = xpad[...]
    conv = cbias_ref[...] + padded[SUBLANES:] * cw_ref[CONV_W - 1:CONV_W, :]
    for d in range(1, CONV_W):
        conv = conv + pltpu.roll(padded, d, 0)[SUBLANES:] * cw_ref[CONV_W - 1 - d:CONV_W - d, :]
    xpad[0:SUBLANES, :] = padded[rows:rows + SUBLANES]
    conv = conv.reshape(nb, c, CONV_DIM)

    def read_h(h):
        return sh[h]

    def write_h(h, s):
        sh[h] = s

    def read_s(g):
        return ss[g]

    def write_s(g, s):
        ss[g] = s

    _ab_math(proj_ref, conv, alog_ref, d_ref, sn_ref, True, read_h, write_h, read_s, write_s, o_ref)

    @pl.when(t == pl.num_programs(1) - 1)
    def _():
        for h in range(HGRN_HEADS):
            sh_out[0, h] = sh[h].T
        for g in range(SSM_GROUPS):
            ss_out[0, g] = ss[g].T
        sc_out[0] = xpad[SUBLANES - (CONV_W - 1):SUBLANES, :]


def _ab_par_kernel(proj_ref, sh_in, ss_in, sc_in, cw_ref, cbias_ref, alog_ref, d_ref, sn_ref,
                   o_ref, sh_out, ss_out, sc_out, xpad, o3):
    nb, c, _ = o3.shape
    nbuf = CONV_W - 1
    proj = proj_ref[...].reshape(nb, c, proj_ref.shape[-1])
    xpad[:, 0:nbuf, :] = sc_in[...]
    xpad[:, nbuf:nbuf + c, :] = proj[:, :, _XBC0:_XBC0 + CONV_DIM]
    conv = cbias_ref[...]
    for k in range(CONV_W):
        conv = conv + xpad[:, k:k + c, :] * cw_ref[k:k + 1, :]
    sc_out[...] = xpad[:, c:c + nbuf, :]

    def read_h(h):
        return sh_in[:, h]

    def write_h(h, s):
        sh_out[:, h] = s

    def read_s(g):
        return ss_in[:, g]

    def write_s(g, s):
        ss_out[:, g] = s

    _ab_math(proj, conv, alog_ref, d_ref, sn_ref, False, read_h, write_h, read_s, write_s, o3)
    o_ref[...] = o3[...].reshape(o_ref.shape)


def _gla_seq_kernel(ready_ref, decay_ref, gate_ref, v_ref, o_ref, sg_out, sg):
    t = pl.program_id(1)

    @pl.when(t == 0)
    def _():
        sg[...] = jnp.zeros_like(sg)

    def read_g(h):
        return sg[h]

    def write_g(h, s):
        sg[h] = s

    nb, c, _ = o_ref.shape
    _gla_core(ready_ref[:, :, _GQD0:_GQD0 + GLA_KEY], ready_ref[:, :, _GKI0:_GKI0 + GLA_KEY],
              ready_ref[:, :, _GKE0:_GKE0 + GLA_KEY], decay_ref[...].reshape(nb, 1, GLA_KEY), None, v_ref[...],
              gate_ref, GLA_HEADS, GLA_HEAD_K, GLA_HEAD_V, _causal(nb, c), True, read_g, write_g, o_ref, 0)

    @pl.when(t == pl.num_programs(1) - 1)
    def _():
        for h in range(GLA_HEADS):
            sg_out[0, h] = sg[h].T


def _gla_par_kernel(proj_ref, gate_ref, v_ref, sg_in, o_ref, sg_out, o3):
    def read_g(h):
        return sg_in[:, h]

    def write_g(h, s):
        sg_out[:, h] = s

    nb, c, _ = o3.shape
    feat = proj_ref[...].reshape(nb, c, proj_ref.shape[-1])
    gate = gate_ref[...].astype(F32).reshape(nb, c, gate_ref.shape[-1])
    v = v_ref[...].astype(F32).reshape(nb, c, v_ref.shape[-1])
    _gla_heads(feat[:, :, _GQ0:_GQ0 + GLA_KEY], feat[:, :, _GK0:_GK0 + GLA_KEY], v,
               feat[:, :, _GLF0:_GLF0 + GLA_KEY], gate, GLA_HEADS, GLA_HEAD_K, GLA_HEAD_V, _causal(nb, c), False,
               read_g, write_g, o3, 0)
    o_ref[...] = o3[...].reshape(o_ref.shape)


SEQ_TILE_CHUNKS = 16
GLA_SEQ_TILE_CHUNKS = 16
PAR_TILE_SEQS = 16


def _full(shape):
    return pl.BlockSpec(shape, lambda *_: (0,) * len(shape))


def _mixer_o_dtype(c):
    return BF16 if c % (2 * SUBLANES) == 0 else F32


def _ab_seq_call(proj, batch, length, params, *, name):
    c = math.gcd(length, CHUNK)
    nb = SEQ_TILE_CHUNKS
    tiles = length // (c * nb)
    assert tiles * c * nb == length
    width = HEADS_PER_GROUP * SSM_HEAD_DIM
    blk = lambda n: pl.BlockSpec((nb, c, n), lambda b, t: (b * tiles + t, 0, 0))
    out_shapes = [
        jax.ShapeDtypeStruct((batch * length // c, c, AB_WIDTH), _mixer_o_dtype(c)),
        jax.ShapeDtypeStruct((batch, HGRN_HEADS, HGRN_HEAD_DIM, HGRN_HEAD_DIM), F32),
        jax.ShapeDtypeStruct((batch, SSM_GROUPS, width, SSM_STATE), F32),
        jax.ShapeDtypeStruct((batch, CONV_W - 1, CONV_DIM), F32),
    ]
    out_specs = [
        blk(AB_WIDTH),
        pl.BlockSpec((1, HGRN_HEADS, HGRN_HEAD_DIM, HGRN_HEAD_DIM), lambda b, t: (b, 0, 0, 0)),
        pl.BlockSpec((1, SSM_GROUPS, width, SSM_STATE), lambda b, t: (b, 0, 0, 0)),
        pl.BlockSpec((1, CONV_W - 1, CONV_DIM), lambda b, t: (b, 0, 0)),
    ]
    return pl.pallas_call(
        _ab_seq_kernel,
        grid=(batch, tiles),
        in_specs=[blk(AB_FEAT)] + [_full(p.shape) for p in params],
        out_specs=out_specs,
        out_shape=out_shapes,
        scratch_shapes=[
            pltpu.VMEM((HGRN_HEADS, HGRN_HEAD_DIM, HGRN_HEAD_DIM), F32),
            pltpu.VMEM((SSM_GROUPS, SSM_STATE, width), F32),
            pltpu.VMEM((nb * c + SUBLANES, CONV_DIM), F32),
        ],
        compiler_params=pltpu.CompilerParams(dimension_semantics=("arbitrary", "arbitrary"),
                                             vmem_limit_bytes=VMEM_LIMIT_BYTES),
        name=name,
    )(proj, *params)


def _ab_par_call(proj, row0, c, s_hgrn, s_ssm, s_conv, params, *, name):
    batch = s_hgrn.shape[0]
    nb = _row_tile(batch, PAR_TILE_SEQS)
    rows = nb * c
    assert row0 % rows == 0
    width = HEADS_PER_GROUP * SSM_HEAD_DIM
    s_ssm = s_ssm.reshape(batch, SSM_GROUPS, width, SSM_STATE)
    blk3 = lambda a, n: pl.BlockSpec((nb, a, n), lambda b: (b, 0, 0))
    blk4 = lambda a, r, n: pl.BlockSpec((nb, a, r, n), lambda b: (b, 0, 0, 0))
    state_specs = [blk4(HGRN_HEADS, HGRN_HEAD_DIM, HGRN_HEAD_DIM), blk4(SSM_GROUPS, width, SSM_STATE),
                   blk3(CONV_W - 1, CONV_DIM)]
    out_shapes = [
        jax.ShapeDtypeStruct((batch * c, AB_WIDTH), F32),
        jax.ShapeDtypeStruct(s_hgrn.shape, F32),
        jax.ShapeDtypeStruct(s_ssm.shape, F32),
        jax.ShapeDtypeStruct(s_conv.shape, F32),
    ]
    return pl.pallas_call(
        _ab_par_kernel,
        grid=(batch // nb,),
        in_specs=[pl.BlockSpec((rows, proj.shape[1]), lambda b: (row0 // rows + b, 0))] + state_specs
                 + [_full(p.shape) for p in params],
        out_specs=[pl.BlockSpec((rows, AB_WIDTH), lambda b: (b, 0))] + state_specs,
        out_shape=out_shapes,
        scratch_shapes=[pltpu.VMEM((nb, c + CONV_W - 1, CONV_DIM), F32), pltpu.VMEM((nb, c, AB_WIDTH), F32)],
        compiler_params=pltpu.CompilerParams(dimension_semantics=("arbitrary",),
                                             vmem_limit_bytes=VMEM_LIMIT_BYTES),
        name=name,
    )(proj, s_hgrn, s_ssm, s_conv, *params)


def _gla_seq_call(ready, decay, gate, v, batch, length, *, name):
    c = math.gcd(length, CHUNK)
    assert c == CHUNK
    nb = GLA_SEQ_TILE_CHUNKS
    tiles = length // (c * nb)
    assert tiles * c * nb == length
    blk = lambda n: pl.BlockSpec((nb, c, n), lambda b, t: (b * tiles + t, 0, 0))
    return pl.pallas_call(
        _gla_seq_kernel,
        grid=(batch, tiles),
        in_specs=[blk(GLA_READY), pl.BlockSpec((nb, GLA_KEY), lambda b, t: (b * tiles + t, 0)), blk(GLA_VAL),
                  blk(GLA_VAL)],
        out_specs=[blk(GLA_VAL),
                   pl.BlockSpec((1, GLA_HEADS, GLA_HEAD_K, GLA_HEAD_V), lambda b, t: (b, 0, 0, 0))],
        out_shape=[jax.ShapeDtypeStruct((batch * length // c, c, GLA_VAL), _mixer_o_dtype(c)),
                   jax.ShapeDtypeStruct((batch, GLA_HEADS, GLA_HEAD_K, GLA_HEAD_V), F32)],
        scratch_shapes=[pltpu.VMEM((GLA_HEADS, GLA_HEAD_V, GLA_HEAD_K), F32)],
        compiler_params=pltpu.CompilerParams(dimension_semantics=("arbitrary", "arbitrary"),
                                             vmem_limit_bytes=VMEM_LIMIT_BYTES),
        name=name,
    )(ready, decay, gate, v)


def _gla_par_call(proj, gate, v, row0, c, s_gla, *, name):
    batch = s_gla.shape[0]
    nb = _row_tile(batch, PAR_TILE_SEQS)
    rows = nb * c
    assert row0 % rows == 0
    sspec = pl.BlockSpec((nb, GLA_HEADS, GLA_HEAD_K, GLA_HEAD_V), lambda b: (b, 0, 0, 0))
    return pl.pallas_call(
        _gla_par_kernel,
        grid=(batch // nb,),
        in_specs=[pl.BlockSpec((rows, a.shape[1]), lambda b: (row0 // rows + b, 0)) for a in (proj, gate, v)] + [sspec],
        out_specs=[pl.BlockSpec((rows, GLA_VAL), lambda b: (b, 0)), sspec],
        out_shape=[jax.ShapeDtypeStruct((batch * c, GLA_VAL), F32), jax.ShapeDtypeStruct(s_gla.shape, F32)],
        scratch_shapes=[pltpu.VMEM((nb, c, GLA_VAL), F32)],
        compiler_params=pltpu.CompilerParams(dimension_semantics=("arbitrary",),
                                             vmem_limit_bytes=VMEM_LIMIT_BYTES),
        name=name,
    )(proj, gate, v, s_gla)


def _split_cols(w):
    main = w.shape[1] // LANES * LANES
    tail = jnp.pad(w[:, main:].astype(BF16), ((0, 0), (0, LANES - (w.shape[1] - main))))
    return w[:, :main].astype(BF16), tail


def kernel(x_prompt, x_sample, state_hgrn, state_ssm, state_conv, state_gla, norm_ffn1, norm_mix, norm_ffn2, norm_final, ffn1_w_in, ffn1_w_out, ffn2_w_in, ffn2_w_out, ab_w_in, ab_w_out, hgrn_lb_logits, hgrn_norm, ssm_conv_w, ssm_conv_b, ssm_dt_bias, ssm_a_log, ssm_d, ssm_norm, gla_w_in, gla_w_gk, gla_b_gk, gla_norm, gla_w_out):
    assert ab_w_in.shape[0] == 1 and gla_w_in.shape[0] == 1, "one HGRN2/SSD layer and one GLA layer"
    bp, lp, _ = x_prompt.shape
    bs, ls, _ = x_sample.shape
    rp, rs = bp * lp, bs * ls
    cp, cs = math.gcd(lp, CHUNK), math.gcd(ls, CHUNK)
    assert rs % cp == 0
    row = lambda v: v.reshape(1, -1)
    stack_row = lambda v: v.reshape(v.shape[0], 1, v.shape[1])
    nf1, nm, nf2 = stack_row(norm_ffn1), stack_row(norm_mix), stack_row(norm_ffn2)
    in_rows, out_rows = D_MODEL // 64, D_FF // 16
    xp = x_prompt.reshape(rp, D_MODEL)
    xs = x_sample.reshape(rs, D_MODEL)

    wp, wt = _split_cols(ab_w_in[0])
    ab_feat_params = (hgrn_lb_logits, jnp.tile(hgrn_norm[0], HGRN_HEADS).reshape(1, -1),
                      jnp.pad(ssm_dt_bias[0], (0, LANES - SSM_HEADS)).reshape(1, -1))
    casts = (_Cast(ffn2_w_in, in_rows), _Cast(ffn2_w_out, out_rows), _Cast(ffn1_w_in, in_rows, layer=1),
             _Cast(ffn1_w_out, out_rows, layer=1), _Cast(ab_w_out, in_rows, layer=0),
             _Cast(gla_w_in[0], in_rows, split=True), _Cast(gla_w_out, in_rows, layer=0))
    (x1, feat, w2_in, w2_out, w1_in, w1_out, ab_wo, gla_wp, gla_wt, gla_wo) = _pre_call(
        xp, xs, 0, nf1, ffn1_w_in[0].astype(BF16), ffn1_w_out[0].astype(BF16), nm, wp, wt, ab_feat_params, casts,
        mode="ab", row_tile=PRE0_ROWS, name="pre0")
    ab_params = (ssm_conv_w[0], row(ssm_conv_b[0]), row(ssm_a_log[0]), row(ssm_d[0]), row(ssm_norm[0]))
    o_p, hgrn_p, ssm_p, conv_p = _ab_seq_call(feat.reshape((rp + rs) // cp, cp, AB_FEAT), bp, lp, ab_params,
                                              name="mix0_prompt")
    o_s, hgrn_s, ssm_s, conv_s = _ab_par_call(feat, rp, cs, state_hgrn[0], state_ssm[0], state_conv[0], ab_params,
                                              name="mix0_sample")
    (x3,) = _post_call(x1, o_p.reshape(rp, AB_WIDTH), o_s, 0, ab_wo, nf2,
                       w2_in, w2_out, row(norm_final), split_output=False, name="post0")

    gla_feat_params = (gla_w_gk[0].astype(BF16), row(gla_b_gk[0]), jnp.tile(gla_norm[0], GLA_HEADS).reshape(1, -1))
    x4, feat, gate, v16, ready, decay = _pre_call(x3, None, 1, nf1, w1_in, w1_out, nm, gla_wp, gla_wt, gla_feat_params,
                                                  mode="gla", row_tile=PRE1_ROWS, name="pre1")
    chunked = lambda a: a.reshape((rp + rs) // cp, cp, a.shape[1])
    o_p, gla_p = _gla_seq_call(chunked(ready), decay, chunked(gate), chunked(v16), bp, lp, name="mix1_prompt")
    o_s, gla_s = _gla_par_call(feat, gate, v16, rp, cs, state_gla[0], name="mix1_sample")
    y_p, y_s = _post_call(x4, o_p.reshape(rp, GLA_VAL), o_s, 1, gla_wo, nf2,
                          w2_in, w2_out, row(norm_final), split_output=True, name="post1")

    ssm_shape = (1, -1, SSM_HEADS, SSM_HEAD_DIM, SSM_STATE)
    return (y_p.reshape(bp, lp, D_MODEL), y_s.reshape(bs, ls, D_MODEL), hgrn_p[None], hgrn_s[None],
            ssm_p.reshape(ssm_shape), ssm_s.reshape(ssm_shape), conv_p[None], conv_s[None], gla_p[None], gla_s[None])
```

```python
import functools
import math

import jax
import jax.numpy as jnp
from jax import lax
from jax.experimental import pallas as pl
from jax.experimental.pallas import tpu as pltpu

F32 = jnp.float32
BF16 = jnp.bfloat16

D_MODEL = 1024
D_FF = 2816
EPS = 1e-6
CHUNK = 64

HGRN_HEADS = 4
HGRN_HEAD_DIM = 128
HGRN_WIDTH = HGRN_HEADS * HGRN_HEAD_DIM

SSM_HEADS = 8
SSM_HEAD_DIM = 64
SSM_INNER = SSM_HEADS * SSM_HEAD_DIM
SSM_GROUPS = 2
SSM_STATE = 128
SSM_GROUP_WIDTH = SSM_INNER // SSM_GROUPS
HEADS_PER_GROUP = SSM_HEADS // SSM_GROUPS
CONV_W = 4
CONV_DIM = SSM_INNER + 2 * SSM_GROUPS * SSM_STATE
AB_WIDTH = HGRN_WIDTH + SSM_INNER

GLA_HEADS = 4
GLA_HEAD_K = 128
GLA_HEAD_V = 256
GLA_KEY = GLA_HEADS * GLA_HEAD_K
GLA_VAL = GLA_HEADS * GLA_HEAD_V
GK_RANK = 16
GK_NORMALIZER = 16.0

LANES = 128
SUBLANES = 8
VMEM_LIMIT_BYTES = 56 * 1024 * 1024

_Q0, _K0, _LF0, _V0, _G0 = (i * HGRN_WIDTH for i in range(5))
_Z0 = 5 * HGRN_WIDTH
_XBC0 = _Z0 + SSM_INNER
_DT0 = _XBC0 + CONV_DIM
AB_FEAT = _DT0 + LANES
_WQ, _WF, _WI, _WG, _WZ = (i * HGRN_WIDTH for i in range(5))
_WXBC = 4 * HGRN_WIDTH + SSM_INNER

_GQ0, _GK0, _GLF0 = 0, GLA_KEY, 2 * GLA_KEY
GLA_FEAT = 3 * GLA_KEY
_GQD0, _GKI0, _GKE0 = 0, GLA_KEY, 2 * GLA_KEY
GLA_READY = 3 * GLA_KEY
_GWQ, _GWK = 0, GLA_KEY
_GWV = 2 * GLA_KEY
_GWG = _GWV + GLA_VAL


def _rms(x, w):
    return x * lax.rsqrt(jnp.mean(x * x, axis=-1, keepdims=True) + EPS) * w


def _rms_core(x):
    return x * lax.rsqrt(jnp.mean(x * x, axis=-1, keepdims=True) + EPS)


def _silu(x):
    return x * jax.nn.sigmoid(x)


def _softplus(x):
    return jnp.maximum(x, 0.0) + jnp.log1p(jnp.exp(-jnp.abs(x)))


def _dot(a, b):
    return jnp.dot(a, b, preferred_element_type=F32)


def _bdot(a, b, ca, cb):
    return lax.dot_general(a, b, (((ca,), (cb,)), ((0,), (0,))), preferred_element_type=F32)


def _split3(x):
    hi = x.astype(BF16)
    r1 = x - hi.astype(F32)
    mid = r1.astype(BF16)
    lo = (r1 - mid.astype(F32)).astype(BF16)
    return hi, mid, lo


def _exact_bdot_lhs01(m01, x, ca, cb):
    return sum(_bdot(m01, p, ca, cb) for p in _split3(x))


def _exact_bdot_rhs01(x, m01, ca, cb):
    return sum(_bdot(p, m01, ca, cb) for p in _split3(x))


def _causal(nb, c):
    r = lax.broadcasted_iota(jnp.int32, (nb, c, c), 1)
    col = lax.broadcasted_iota(jnp.int32, (nb, c, c), 2)
    return r >= col


BF16_ROWS = 2 * SUBLANES


def _triangle3(nb, c, wide_axis):
    shape = (1, c, 3 * c) if wide_axis == 2 else (1, 3 * c, c)
    wide = lax.broadcasted_iota(jnp.int32, shape, wide_axis)
    narrow = lax.broadcasted_iota(jnp.int32, shape, 3 - wide_axis)
    hit = None
    for k in range(3):
        wk = wide - k * c
        term = (wk >= 0) & (wk < c) & (wk <= narrow)
        hit = term if hit is None else hit | term
    return jnp.broadcast_to(hit.astype(BF16), (nb,) + shape[1:])


def _chunk_cumsum(x, causal):
    nb, c, _ = x.shape
    if c % BF16_ROWS:
        return _exact_bdot_lhs01(causal.astype(BF16), x, 2, 1)
    return _bdot(_triangle3(nb, c, 2), jnp.concatenate(_split3(x), axis=1), 2, 1)


def _chunk_cumsum_t(x):
    nb, c, _ = x.shape
    if c % BF16_ROWS:
        return _exact_bdot_rhs01(x, _upper(nb, c), 1, 1)
    return _bdot(jnp.concatenate(_split3(x), axis=1), _triangle3(nb, c, 1), 1, 1)


def _col_bcast(row, lanes=LANES):
    nb, _, k = row.shape
    hi, mid, lo = (p.astype(F32) for p in _split3(row))
    r = lax.broadcasted_iota(jnp.int32, (1, BF16_ROWS, k), 1)
    stacked = jnp.where(r == 0, hi, jnp.where(r == 1, mid, jnp.where(r == 2, lo, 0.0))).astype(BF16)
    return _bdot(stacked, jnp.ones((nb, BF16_ROWS, lanes), BF16), 1, 1)


MXU_DIM = 256
FF_TILES = ((0, 6 * MXU_DIM), (6 * MXU_DIM, D_FF))
assert all(lo % MXU_DIM == 0 and hi % MXU_DIM == 0 for lo, hi in FF_TILES)


def _ffn_half(x, norm_w, w_in_ref, w_out_ref):
    hb = _rms(x, norm_w).astype(BF16)
    acc = None
    for lo, hi in FF_TILES:
        gate = _dot(hb, w_in_ref[:, lo:hi])
        up = _dot(hb, w_in_ref[:, D_FF + lo:D_FF + hi])
        act = (_silu(gate) * up).astype(BF16)
        part = _dot(act, w_out_ref[lo:hi, :])
        acc = part if acc is None else acc + part
    return 0.5 * acc


def _pick_group(first_steps, a, b):
    return jnp.where(pl.program_id(0) < first_steps, a, b)


def _cast_rows(src_ref, *dst_refs):
    if len(dst_refs) == 1:
        dst_refs[0][...] = src_ref[...].astype(BF16)
        return
    main_ref, tail_ref = dst_refs
    main = main_ref.shape[-1]
    rest = src_ref.shape[-1] - main
    main_ref[...] = src_ref[:, :main].astype(BF16)
    tail_ref[...] = jnp.zeros(tail_ref.shape, BF16)
    tail_ref[:, :rest] = src_ref[:, main:].astype(BF16)


def _ab_features(hb, wp_ref, wt_ref, lb_ref, hn_ref, dtb_ref, feat_ref):
    def group(w0):
        return _dot(hb, wp_ref[:, w0:w0 + HGRN_WIDTH])

    def put(c0, val):
        feat_ref[:, c0:c0 + val.shape[-1]] = val

    lb = _lower_bound(lb_ref[...])
    f = lb + (1.0 - lb) * jax.nn.sigmoid(group(_WF))
    put(_K0, 1.0 - f)
    put(_LF0, jnp.log(f))
    put(_Q0, _silu(group(_WQ)))
    put(_G0, _silu(group(_WG)) * hn_ref[...])
    put(_Z0, _silu(group(_WZ)))
    put(_DT0, _softplus(_dot(hb, wt_ref[...]) + dtb_ref[...]))
    put(_XBC0, _dot(hb, wp_ref[:, _WXBC:_WXBC + CONV_DIM]))
    put(_V0, group(_WI))


def _decayed_operands(q, k, log_f):
    nb, c, _ = q.shape
    g = _chunk_cumsum(log_f, _causal(nb, c))
    g_last = g[:, c - 1:c, :]
    q_dec = (q * jnp.exp(g)).astype(BF16)
    k_inv_f = k * jnp.exp(-g)
    decay = jnp.exp(g_last)
    k_end = (k_inv_f * decay).astype(BF16)
    return q_dec, k_inv_f.astype(BF16), k_end, decay, g_last


def _gla_features(hb, wp_ref, wt_ref, wgk_ref, bgk_ref, gn_ref, feat_ref, gate_ref, v_ref, ready_ref, decay_ref):
    def put(c0, val):
        feat_ref[:, c0:c0 + val.shape[-1]] = val

    rows = hb.shape[0]
    nb = rows // CHUNK
    gk_low = _dot(hb, wt_ref[...])[:, :GK_RANK].astype(BF16)
    gk = _dot(gk_low, wgk_ref[...]) + bgk_ref[...]
    log_f = -_softplus(-gk) / GK_NORMALIZER
    q = _dot(hb, wp_ref[:, _GWQ:_GWQ + GLA_KEY]) * (GLA_HEAD_K ** -0.5)
    k = _dot(hb, wp_ref[:, _GWK:_GWK + GLA_KEY])
    put(_GLF0, log_f)
    put(_GQ0, q)
    put(_GK0, k)
    chunked = lambda t: t.reshape(nb, CHUNK, GLA_KEY)
    q_dec, k_inv, k_end, decay, _ = _decayed_operands(chunked(q), chunked(k), chunked(log_f))
    for c0, val in ((_GQD0, q_dec), (_GKI0, k_inv), (_GKE0, k_end)):
        ready_ref[:, c0:c0 + GLA_KEY] = val.reshape(rows, GLA_KEY)
    decay_ref[...] = decay.reshape(nb, GLA_KEY)
    gate_ref[...] = (_silu(_dot(hb, wp_ref[:, _GWG:_GWG + GLA_VAL])) * gn_ref[...]).astype(BF16)
    v_ref[...] = _dot(hb, wp_ref[:, _GWV:_GWV + GLA_VAL]).astype(BF16)


_FEATURES = {
    "ab": (_ab_features, ((AB_FEAT, F32, 1),)),
    "gla": (_gla_features, ((GLA_FEAT, F32, 1), (GLA_VAL, BF16, 1), (GLA_VAL, BF16, 1), (GLA_READY, BF16, 1),
                            (GLA_KEY, F32, CHUNK))),
}
N_FEATURE_PARAMS = 3


def _pre_kernel(*refs, first_steps, cast_arity, mode):
    n_x = 1 if first_steps is None else 2
    x_refs, refs = refs[:n_x], refs[n_x:]
    nf_ref, w_in_ref, w_out_ref, nm_ref, wp_ref, wt_ref = refs[:6]
    feat_params, refs = refs[6:6 + N_FEATURE_PARAMS], refs[6 + N_FEATURE_PARAMS:]
    feature_fn, feature_outs = _FEATURES[mode]
    cast_src, refs = refs[:len(cast_arity)], refs[len(cast_arity):]
    x1_ref, feat_refs, cast_dst = refs[0], refs[1:1 + len(feature_outs)], list(refs[1 + len(feature_outs):])
    if first_steps is None:
        x = x_refs[0][...]
    else:
        x = _pick_group(first_steps, x_refs[0][...], x_refs[1][...])
    x1 = x + _ffn_half(x, nf_ref[...], w_in_ref, w_out_ref)
    x1_ref[...] = x1
    hb = _rms(x1, nm_ref[...]).astype(BF16)
    feature_fn(hb, wp_ref, wt_ref, *feat_params, *feat_refs)
    for src_ref, arity in zip(cast_src, cast_arity):
        _cast_rows(src_ref, *cast_dst[:arity])
        cast_dst = cast_dst[arity:]


def _post_kernel(x_hbm, oa_hbm, ob_hbm, wo_hbm, nf_ref, w_in_hbm, w_out_hbm, nfin_ref, *rest, layer, tm, final_norm):
    y_hbms, (wo_v, w_in_v, w_out_v) = rest[:-3], rest[-3:]
    pltpu.sync_copy(wo_hbm, wo_v)
    pltpu.sync_copy(w_in_hbm.at[layer], w_in_v)
    pltpu.sync_copy(w_out_hbm.at[layer], w_out_v)
    norm_w = nf_ref[layer]

    def tile(x_ref, o_ref, y_ref):
        x2 = x_ref[...] + _dot(o_ref[...].astype(BF16), wo_v[...])
        y = x2 + _ffn_half(x2, norm_w, w_in_v, w_out_v)
        if final_norm:
            y = _rms(y, nfin_ref[...])
        y_ref[...] = y

    ra, rb = oa_hbm.shape[0], ob_hbm.shape[0]
    first_steps = ra // tm

    def stream(o_hbm, y_hbm, steps, x_off, y_off):
        rows = lambda off, width: pl.BlockSpec((tm, width), lambda i: (i + off, 0))
        pltpu.emit_pipeline(tile, grid=(steps,), in_specs=[rows(x_off, D_MODEL), rows(0, o_hbm.shape[1])],
                            out_specs=[rows(y_off, D_MODEL)])(x_hbm, o_hbm, y_hbm)

    split = len(y_hbms) == 2
    stream(oa_hbm, y_hbms[0], first_steps, 0, 0)
    stream(ob_hbm, y_hbms[-1], rb // tm, first_steps, 0 if split else first_steps)


def _resident(shape, layer=None):
    if layer is None:
        return pl.BlockSpec(shape, lambda *_: (0,) * len(shape), pipeline_mode=pl.Buffered(1))
    return pl.BlockSpec((None,) + tuple(shape[1:]), lambda *_: (layer,) + (0,) * (len(shape) - 1),
                        pipeline_mode=pl.Buffered(1))


def _row_tile(rows, want):
    t = min(rows, want)
    assert rows % t == 0
    return t


PRE0_ROWS = 256
PRE1_ROWS = 512
POST_ROWS = 512


def _group_specs(tm, first_steps, width):
    first = pl.BlockSpec((tm, width), lambda i: (jnp.minimum(i, first_steps - 1), 0))
    second = pl.BlockSpec((tm, width), lambda i: (jnp.maximum(i - first_steps, 0), 0))
    return first, second


def _weight_spec(w, layer):
    return _resident(w.shape, layer if w.ndim == 3 else None)


class _Cast:
    def __init__(self, src, rows_per_step, layer=None, split=False):
        self.src, self.rps, self.layer, self.split = src, rows_per_step, layer, split
        rows = src.shape[-2]
        assert rows % rows_per_step == 0 and rows_per_step % (2 * SUBLANES) == 0
        self.steps = rows // rows_per_step
        assert not split or layer is not None or src.ndim == 2

    def _index(self, lead):
        last = self.steps - 1
        return lambda i: lead + (jnp.minimum(i, last), 0)

    def in_spec(self):
        cols = self.src.shape[-1]
        if self.src.ndim == 2:
            return pl.BlockSpec((self.rps, cols), self._index(()))
        if self.layer is None:
            return pl.BlockSpec((self.src.shape[0], self.rps, cols), self._index((0,)))
        return pl.BlockSpec((None, self.rps, cols), self._index((self.layer,)))

    def outs(self):
        rows, cols = self.src.shape[-2:]
        if self.src.ndim == 3 and self.layer is None:
            n = self.src.shape[0]
            return [(jax.ShapeDtypeStruct((n, rows, cols), BF16), pl.BlockSpec((n, self.rps, cols), self._index((0,))))]
        widths = [cols // LANES * LANES, LANES] if self.split else [cols]
        return [(jax.ShapeDtypeStruct((rows, w), BF16), pl.BlockSpec((self.rps, w), self._index(()))) for w in widths]


def _pre_call(xa, xb, layer, nf, w_in, w_out, nm, wp, wt, feat_params, casts=(), *, mode, row_tile, name):
    ra = xa.shape[0]
    rb = 0 if xb is None else xb.shape[0]
    rows = ra + rb
    tm = _row_tile(rb if rb else ra, row_tile)
    assert ra % tm == 0
    assert all(cast.steps <= rows // tm for cast in casts)
    assert len(feat_params) == N_FEATURE_PARAMS
    feature_outs = _FEATURES[mode][1]
    tok = lambda n: pl.BlockSpec((tm, n), lambda i: (i, 0))
    if xb is None:
        first_steps, x_specs, xs = None, [tok(D_MODEL)], (xa,)
    else:
        first_steps = ra // tm
        x_specs, xs = list(_group_specs(tm, first_steps, D_MODEL)), (xa, xb)
    cast_outs = [cast.outs() for cast in casts]
    flat_outs = [o for outs in cast_outs for o in outs]
    return pl.pallas_call(
        functools.partial(_pre_kernel, first_steps=first_steps, cast_arity=tuple(len(o) for o in cast_outs),
                          mode=mode),
        grid=(rows // tm,),
        in_specs=x_specs + [_resident(nf.shape, layer), _weight_spec(w_in, layer), _weight_spec(w_out, layer),
                            _resident(nm.shape, layer), _resident(wp.shape), _resident(wt.shape)]
                         + [_resident(p.shape) for p in feat_params] + [cast.in_spec() for cast in casts],
        out_specs=[tok(D_MODEL)] + [pl.BlockSpec((tm // div, w), lambda i: (i, 0)) for w, _, div in feature_outs]
                  + [spec for _, spec in flat_outs],
        out_shape=[jax.ShapeDtypeStruct((rows, D_MODEL), F32)]
                  + [jax.ShapeDtypeStruct((rows // div, w), dt) for w, dt, div in feature_outs]
                  + [shape for shape, _ in flat_outs],
        compiler_params=pltpu.CompilerParams(dimension_semantics=("arbitrary",), vmem_limit_bytes=VMEM_LIMIT_BYTES),
        name=name,
    )(*xs, nf, w_in, w_out, nm, wp, wt, *feat_params, *[cast.src for cast in casts])


def _post_call(x, oa, ob, layer, wo, nf, w_in, w_out, nfin, *, split_output, name):
    rows = x.shape[0]
    ra, rb = oa.shape[0], ob.shape[0]
    assert ra + rb == rows and w_in.ndim == 3 and w_out.ndim == 3
    tm = _row_tile(rb, POST_ROWS)
    assert ra % tm == 0
    if split_output:
        out_shape = [jax.ShapeDtypeStruct((ra, D_MODEL), F32), jax.ShapeDtypeStruct((rb, D_MODEL), F32)]
    else:
        out_shape = [jax.ShapeDtypeStruct((rows, D_MODEL), F32)]
    hbm = pl.BlockSpec(memory_space=pl.ANY)
    vmem = pl.BlockSpec(memory_space=pltpu.VMEM)
    return pl.pallas_call(
        functools.partial(_post_kernel, layer=layer, tm=tm, final_norm=split_output),
        in_specs=[hbm, hbm, hbm, hbm, vmem, hbm, hbm, vmem],
        out_specs=[hbm] * len(out_shape),
        out_shape=out_shape,
        scratch_shapes=[pltpu.VMEM(wo.shape, BF16), pltpu.VMEM(w_in.shape[1:], BF16), pltpu.VMEM(w_out.shape[1:], BF16)],
        compiler_params=pltpu.CompilerParams(vmem_limit_bytes=VMEM_LIMIT_BYTES),
        name=name,
    )(x, oa, ob, wo, nf, w_in, w_out, nfin)


def _gla_heads(q, k, v, log_f, gate, n_heads, dk, dv, causal, sequential, read_state, write_state,
               o_ref, o_col0):
    q_dec, k_inv, k_end, decay, g_last = _decayed_operands(q, k, log_f)
    _gla_core(q_dec, k_inv, k_end, decay, g_last, v, gate, n_heads, dk, dv, causal, sequential, read_state,
              write_state, o_ref, o_col0)


def _gla_core(q_dec, k_inv, k_end, decay, g_last, v, gate, n_heads, dk, dv, causal, sequential, read_state,
              write_state, o_ref, o_col0):
    nb, c, _ = q_dec.shape
    vb = v.astype(BF16)
    heads = range(n_heads)
    ks = [slice(h * dk, (h + 1) * dk) for h in heads]
    vs = [slice(h * dv, (h + 1) * dv) for h in heads]
    scores = [_bdot(q_dec[:, :, ks[h]], k_inv[:, :, ks[h]], 2, 2) for h in heads]
    scores = [jnp.where(causal, sc, 0.0).astype(BF16) for sc in scores]
    o_intra = [_bdot(scores[h], vb[:, :, vs[h]], 2, 1) for h in heads]

    def emit(h, b, o):
        cols = slice(o_col0 + h * dv, o_col0 + (h + 1) * dv)
        o_ref[b, :, cols] = (_rms_core(o) * gate[b, :, vs[h]]).astype(o_ref.dtype)

    if sequential:
        kv_t = [_bdot(vb[:, :, vs[h]], k_end[:, :, ks[h]], 1, 1) for h in heads]
        for b in range(nb):
            for h in heads:
                s_t = read_state(h)
                o_inter = lax.dot_general(q_dec[b, :, ks[h]], s_t.astype(BF16), (((1,), (1,)), ((), ())),
                                          preferred_element_type=F32)
                write_state(h, decay[b, :, ks[h]] * s_t + kv_t[h][b])
                emit(h, b, o_intra[h][b] + o_inter)
    else:
        for h in heads:
            kv = _bdot(k_end[:, :, ks[h]], vb[:, :, vs[h]], 1, 1)
            decay_col = jnp.exp(_col_bcast(g_last[:, :, ks[h]]))
            decay_col = jnp.concatenate([decay_col] * (dv // LANES), axis=-1)
            s0 = read_state(h)
            o_inter = _bdot(q_dec[:, :, ks[h]], s0.astype(BF16), 2, 1)
            write_state(h, decay_col * s0 + kv)
            emit(h, slice(None), o_intra[h] + o_inter)


def _expand_heads(x, expand):
    return _exact_bdot_rhs01(x, expand, 2, 1)


def _ssd_heads(xs, bs, cs, z, dt, a_row, d_row, norm_w, causal, sequential, read_state, write_state,
               o_ref, o_col0):
    nb, c, _ = xs.shape
    hrow = lax.broadcasted_iota(jnp.int32, (SSM_HEADS, SSM_INNER), 0)
    hcol = lax.broadcasted_iota(jnp.int32, (SSM_HEADS, SSM_INNER), 1) // SSM_HEAD_DIM
    expand2d = (hrow == hcol).astype(BF16)
    expand = jnp.broadcast_to(expand2d[None], (nb, SSM_HEADS, SSM_INNER))
    d_x = sum(_dot(p, expand2d) for p in _split3(d_row))
    dta = dt * a_row
    cum = _chunk_cumsum(dta, causal)
    cum_t = _chunk_cumsum_t(dta)
    dt_x = _expand_heads(dt, expand)
    cum_x = _expand_heads(cum, expand)
    xdt = xs * dt_x
    cum_last = cum_x[:, c - 1:c, :]
    x_end = (xdt * jnp.exp(cum_last - cum_x)).astype(BF16)
    chunk_dec = jnp.exp(cum_x)
    bsb = bs.astype(BF16)
    csb = cs.astype(BF16)
    lane_head = lax.broadcasted_iota(jnp.int32, (nb, c, SSM_GROUP_WIDTH), 2) // SSM_HEAD_DIM
    groups = range(SSM_GROUPS)
    gl = [slice(g * SSM_STATE, (g + 1) * SSM_STATE) for g in groups]
    hl = [slice(g * SSM_GROUP_WIDTH, (g + 1) * SSM_GROUP_WIDTH) for g in groups]
    cb = [_bdot(csb[:, :, gl[g]], bsb[:, :, gl[g]], 2, 2) for g in groups]
    y_intra = []
    for g in groups:
        xdt_g = xdt[:, :, hl[g]]
        y = None
        for r in range(HEADS_PER_GROUP):
            h = g * HEADS_PER_GROUP + r
            col = jnp.broadcast_to(cum[:, :, h:h + 1], (nb, c, c))
            row = cum_t[:, h:h + 1, :]
            dec = jnp.where(causal, jnp.exp(col - row), 0.0)
            lmat = (cb[g] * dec).astype(BF16)
            xm = jnp.where(lane_head == r, xdt_g, 0.0).astype(BF16)
            part = _bdot(lmat, xm, 2, 1)
            y = part if y is None else y + part
        y_intra.append(y)

    def finish(g, y_in, y_inter, b):
        y_all = y_in + y_inter * chunk_dec[b, :, hl[g]] + d_x[:, hl[g]] * xs[b, :, hl[g]]
        y_all = _rms(y_all * z[b, :, hl[g]], norm_w[:, hl[g]])
        cols = slice(o_col0 + g * SSM_GROUP_WIDTH, o_col0 + (g + 1) * SSM_GROUP_WIDTH)
        o_ref[b, :, cols] = y_all.astype(o_ref.dtype)

    if sequential:
        kv_t = [_bdot(bsb[:, :, gl[g]], x_end[:, :, hl[g]], 1, 1) for g in groups]
        for b in range(nb):
            for g in groups:
                s_t = read_state(g)
                y_inter = _dot(csb[b, :, gl[g]], s_t.astype(BF16))
                write_state(g, chunk_dec[b, c - 1:c, hl[g]] * s_t + kv_t[g][b])
                finish(g, y_intra[g][b], y_inter, b)
    else:
        for g in groups:
            kv = _bdot(x_end[:, :, hl[g]], bsb[:, :, gl[g]], 1, 1)
            decay = jnp.exp(_col_bcast(cum_last[:, :, hl[g]], SSM_STATE))
            s0 = read_state(g)
            y_inter = _bdot(csb[:, :, gl[g]], s0.astype(BF16), 2, 2)
            write_state(g, decay * s0 + kv)
            finish(g, y_intra[g], y_inter, slice(None))


def _upper(nb, c):
    r = lax.broadcasted_iota(jnp.int32, (nb, c, c), 1)
    col = lax.broadcasted_iota(jnp.int32, (nb, c, c), 2)
    return (r <= col).astype(BF16)


def _lower_bound(lb_logits):
    m = jnp.max(lb_logits, axis=0, keepdims=True)
    e = jnp.exp(lb_logits - m)
    return e[0:1, :] / jnp.sum(e, axis=0, keepdims=True)


def _ab_math(feat, conv, alog_ref, d_ref, sn_ref, sequential, read_h, write_h, read_s, write_s, o_ref):
    nb, c, _ = o_ref.shape
    causal = _causal(nb, c)
    w = HGRN_WIDTH
    _gla_heads(feat[:, :, _Q0:_Q0 + w], feat[:, :, _K0:_K0 + w], feat[:, :, _V0:_V0 + w], feat[:, :, _LF0:_LF0 + w],
               feat[:, :, _G0:_G0 + w], HGRN_HEADS, HGRN_HEAD_DIM, HGRN_HEAD_DIM, causal, sequential,
               read_h, write_h, o_ref, 0)
    act = _silu(conv)
    _ssd_heads(act[:, :, :SSM_INNER], act[:, :, SSM_INNER:SSM_INNER + SSM_GROUPS * SSM_STATE],
               act[:, :, SSM_INNER + SSM_GROUPS * SSM_STATE:], feat[:, :, _Z0:_Z0 + SSM_INNER],
               feat[:, :, _DT0:_DT0 + SSM_HEADS], -jnp.exp(alog_ref[...]), d_ref[...], sn_ref[...], causal,
               sequential, read_s, write_s, o_ref, HGRN_WIDTH)


def _ab_seq_kernel(proj_ref, cw_ref, cbias_ref, alog_ref, d_ref, sn_ref,
                   o_ref, sh_out, ss_out, sc_out, sh, ss, xpad):
    t = pl.program_id(1)
    nb, c, _ = o_ref.shape
    rows = nb * c

    @pl.when(t == 0)
    def _():
        sh[...] = jnp.zeros_like(sh)
        ss[...] = jnp.zeros_like(ss)
        xpad[0:SUBLANES, :] = jnp.zeros((SUBLANES, CONV_DIM), F32)

    xpad[SUBLANES:SUBLANES + rows, :] = proj_ref[:, :, _XBC0:_XBC0 + CONV_DIM].reshape(rows, CONV_DIM)
    padded = xpad[...]
    conv = cbias_ref[...] + padded[SUBLANES:] * cw_ref[CONV_W - 1:CONV_W, :]
    for d in range(1, CONV_W):
        conv = conv + pltpu.roll(padded, d, 0)[SUBLANES:] * cw_ref[CONV_W - 1 - d:CONV_W - d, :]
    xpad[0:SUBLANES, :] = padded[rows:rows + SUBLANES]
    conv = conv.reshape(nb, c, CONV_DIM)

    def read_h(h):
        return sh[h]

    def write_h(h, s):
        sh[h] = s

    def read_s(g):
        return ss[g]

    def write_s(g, s):
        ss[g] = s

    _ab_math(proj_ref, conv, alog_ref, d_ref, sn_ref, True, read_h, write_h, read_s, write_s, o_ref)

    @pl.when(t == pl.num_programs(1) - 1)
    def _():
        for h in range(HGRN_HEADS):
            sh_out[0, h] = sh[h].T
        for g in range(SSM_GROUPS):
            ss_out[0, g] = ss[g].T
        sc_out[0] = xpad[SUBLANES - (CONV_W - 1):SUBLANES, :]


def _ab_par_kernel(proj_ref, sh_in, ss_in, sc_in, cw_ref, cbias_ref, alog_ref, d_ref, sn_ref,
                   o_ref, sh_out, ss_out, sc_out, xpad, o3):
    nb, c, _ = o3.shape
    nbuf = CONV_W - 1
    proj = proj_ref[...].reshape(nb, c, proj_ref.shape[-1])
    xpad[:, 0:nbuf, :] = sc_in[...]
    xpad[:, nbuf:nbuf + c, :] = proj[:, :, _XBC0:_XBC0 + CONV_DIM]
    conv = cbias_ref[...]
    for k in range(CONV_W):
        conv = conv + xpad[:, k:k + c, :] * cw_ref[k:k + 1, :]
    sc_out[...] = xpad[:, c:c + nbuf, :]

    def read_h(h):
        return sh_in[:, h]

    def write_h(h, s):
        sh_out[:, h] = s

    def read_s(g):
        return ss_in[:, g]

    def write_s(g, s):
        ss_out[:, g] = s

    _ab_math(proj, conv, alog_ref, d_ref, sn_ref, False, read_h, write_h, read_s, write_s, o3)
    o_ref[...] = o3[...].reshape(o_ref.shape)


def _gla_seq_kernel(ready_ref, decay_ref, gate_ref, v_ref, o_ref, sg_out, sg):
    t = pl.program_id(1)

    @pl.when(t == 0)
    def _():
        sg[...] = jnp.zeros_like(sg)

    def read_g(h):
        return sg[h]

    def write_g(h, s):
        sg[h] = s

    nb, c, _ = o_ref.shape
    _gla_core(ready_ref[:, :, _GQD0:_GQD0 + GLA_KEY], ready_ref[:, :, _GKI0:_GKI0 + GLA_KEY],
              ready_ref[:, :, _GKE0:_GKE0 + GLA_KEY], decay_ref[...].reshape(nb, 1, GLA_KEY), None, v_ref[...],
              gate_ref, GLA_HEADS, GLA_HEAD_K, GLA_HEAD_V, _causal(nb, c), True, read_g, write_g, o_ref, 0)

    @pl.when(t == pl.num_programs(1) - 1)
    def _():
        for h in range(GLA_HEADS):
            sg_out[0, h] = sg[h].T


def _gla_par_kernel(proj_ref, gate_ref, v_ref, sg_in, o_ref, sg_out, o3):
    def read_g(h):
        return sg_in[:, h]

    def write_g(h, s):
        sg_out[:, h] = s

    nb, c, _ = o3.shape
    feat = proj_ref[...].reshape(nb, c, proj_ref.shape[-1])
    gate = gate_ref[...].astype(F32).reshape(nb, c, gate_ref.shape[-1])
    v = v_ref[...].astype(F32).reshape(nb, c, v_ref.shape[-1])
    _gla_heads(feat[:, :, _GQ0:_GQ0 + GLA_KEY], feat[:, :, _GK0:_GK0 + GLA_KEY], v,
               feat[:, :, _GLF0:_GLF0 + GLA_KEY], gate, GLA_HEADS, GLA_HEAD_K, GLA_HEAD_V, _causal(nb, c), False,
               read_g, write_g, o3, 0)
    o_ref[...] = o3[...].reshape(o_ref.shape)


SEQ_TILE_CHUNKS = 8
GLA_SEQ_TILE_CHUNKS = 16
PAR_TILE_SEQS = 16


def _full(shape):
    return pl.BlockSpec(shape, lambda *_: (0,) * len(shape))


def _mixer_o_dtype(c):
    return BF16 if c % (2 * SUBLANES) == 0 else F32


def _ab_seq_call(proj, batch, length, params, *, name):
    c = math.gcd(length, CHUNK)
    nb = SEQ_TILE_CHUNKS
    tiles = length // (c * nb)
    assert tiles * c * nb == length
    width = HEADS_PER_GROUP * SSM_HEAD_DIM
    blk = lambda n: pl.BlockSpec((nb, c, n), lambda b, t: (b * tiles + t, 0, 0))
    out_shapes = [
        jax.ShapeDtypeStruct((batch * length // c, c, AB_WIDTH), _mixer_o_dtype(c)),
        jax.ShapeDtypeStruct((batch, HGRN_HEADS, HGRN_HEAD_DIM, HGRN_HEAD_DIM), F32),
        jax.ShapeDtypeStruct((batch, SSM_GROUPS, width, SSM_STATE), F32),
        jax.ShapeDtypeStruct((batch, CONV_W - 1, CONV_DIM), F32),
    ]
    out_specs = [
        blk(AB_WIDTH),
        pl.BlockSpec((1, HGRN_HEADS, HGRN_HEAD_DIM, HGRN_HEAD_DIM), lambda b, t: (b, 0, 0, 0)),
        pl.BlockSpec((1, SSM_GROUPS, width, SSM_STATE), lambda b, t: (b, 0, 0, 0)),
        pl.BlockSpec((1, CONV_W - 1, CONV_DIM), lambda b, t: (b, 0, 0)),
    ]
    return pl.pallas_call(
        _ab_seq_kernel,
        grid=(batch, tiles),
        in_specs=[blk(AB_FEAT)] + [_full(p.shape) for p in params],
        out_specs=out_specs,
        out_shape=out_shapes,
        scratch_shapes=[
            pltpu.VMEM((HGRN_HEADS, HGRN_HEAD_DIM, HGRN_HEAD_DIM), F32),
            pltpu.VMEM((SSM_GROUPS, SSM_STATE, width), F32),
            pltpu.VMEM((nb * c + SUBLANES, CONV_DIM), F32),
        ],
        compiler_params=pltpu.CompilerParams(dimension_semantics=("arbitrary", "arbitrary"),
                                             vmem_limit_bytes=VMEM_LIMIT_BYTES),
        name=name,
    )(proj, *params)


def _ab_par_call(proj, row0, c, s_hgrn, s_ssm, s_conv, params, *, name):
    batch = s_hgrn.shape[0]
    nb = _row_tile(batch, PAR_TILE_SEQS)
    rows = nb * c
    assert row0 % rows == 0
    width = HEADS_PER_GROUP * SSM_HEAD_DIM
    s_ssm = s_ssm.reshape(batch, SSM_GROUPS, width, SSM_STATE)
    blk3 = lambda a, n: pl.BlockSpec((nb, a, n), lambda b: (b, 0, 0))
    blk4 = lambda a, r, n: pl.BlockSpec((nb, a, r, n), lambda b: (b, 0, 0, 0))
    state_specs = [blk4(HGRN_HEADS, HGRN_HEAD_DIM, HGRN_HEAD_DIM), blk4(SSM_GROUPS, width, SSM_STATE),
                   blk3(CONV_W - 1, CONV_DIM)]
    out_shapes = [
        jax.ShapeDtypeStruct((batch * c, AB_WIDTH), F32),
        jax.ShapeDtypeStruct(s_hgrn.shape, F32),
        jax.ShapeDtypeStruct(s_ssm.shape, F32),
        jax.ShapeDtypeStruct(s_conv.shape, F32),
    ]
    return pl.pallas_call(
        _ab_par_kernel,
        grid=(batch // nb,),
        in_specs=[pl.BlockSpec((rows, proj.shape[1]), lambda b: (row0 // rows + b, 0))] + state_specs
                 + [_full(p.shape) for p in params],
        out_specs=[pl.BlockSpec((rows, AB_WIDTH), lambda b: (b, 0))] + state_specs,
        out_shape=out_shapes,
        scratch_shapes=[pltpu.VMEM((nb, c + CONV_W - 1, CONV_DIM), F32), pltpu.VMEM((nb, c, AB_WIDTH), F32)],
        compiler_params=pltpu.CompilerParams(dimension_semantics=("arbitrary",),
                                             vmem_limit_bytes=VMEM_LIMIT_BYTES),
        name=name,
    )(proj, s_hgrn, s_ssm, s_conv, *params)


def _gla_seq_call(ready, decay, gate, v, batch, length, *, name):
    c = math.gcd(length, CHUNK)
    assert c == CHUNK
    nb = GLA_SEQ_TILE_CHUNKS
    tiles = length // (c * nb)
    assert tiles * c * nb == length
    blk = lambda n: pl.BlockSpec((nb, c, n), lambda b, t: (b * tiles + t, 0, 0))
    return pl.pallas_call(
        _gla_seq_kernel,
        grid=(batch, tiles),
        in_specs=[blk(GLA_READY), pl.BlockSpec((nb, GLA_KEY), lambda b, t: (b * tiles + t, 0)), blk(GLA_VAL),
                  blk(GLA_VAL)],
        out_specs=[blk(GLA_VAL),
                   pl.BlockSpec((1, GLA_HEADS, GLA_HEAD_K, GLA_HEAD_V), lambda b, t: (b, 0, 0, 0))],
        out_shape=[jax.ShapeDtypeStruct((batch * length // c, c, GLA_VAL), _mixer_o_dtype(c)),
                   jax.ShapeDtypeStruct((batch, GLA_HEADS, GLA_HEAD_K, GLA_HEAD_V), F32)],
        scratch_shapes=[pltpu.VMEM((GLA_HEADS, GLA_HEAD_V, GLA_HEAD_K), F32)],
        compiler_params=pltpu.CompilerParams(dimension_semantics=("arbitrary", "arbitrary"),
                                             vmem_limit_bytes=VMEM_LIMIT_BYTES),
        name=name,
    )(ready, decay, gate, v)


def _gla_par_call(proj, gate, v, row0, c, s_gla, *, name):
    batch = s_gla.shape[0]
    nb = _row_tile(batch, PAR_TILE_SEQS)
    rows = nb * c
    assert row0 % rows == 0
    sspec = pl.BlockSpec((nb, GLA_HEADS, GLA_HEAD_K, GLA_HEAD_V), lambda b: (b, 0, 0, 0))
    return pl.pallas_call(
        _gla_par_kernel,
        grid=(batch // nb,),
        in_specs=[pl.BlockSpec((rows, a.shape[1]), lambda b: (row0 // rows + b, 0)) for a in (proj, gate, v)] + [sspec],
        out_specs=[pl.BlockSpec((rows, GLA_VAL), lambda b: (b, 0)), sspec],
        out_shape=[jax.ShapeDtypeStruct((batch * c, GLA_VAL), F32), jax.ShapeDtypeStruct(s_gla.shape, F32)],
        scratch_shapes=[pltpu.VMEM((nb, c, GLA_VAL), F32)],
        compiler_params=pltpu.CompilerParams(dimension_semantics=("arbitrary",),
                                             vmem_limit_bytes=VMEM_LIMIT_BYTES),
        name=name,
    )(proj, gate, v, s_gla)


def _split_cols(w):
    main = w.shape[1] // LANES * LANES
    tail = jnp.pad(w[:, main:].astype(BF16), ((0, 0), (0, LANES - (w.shape[1] - main))))
    return w[:, :main].astype(BF16), tail


def kernel(x_prompt, x_sample, state_hgrn, state_ssm, state_conv, state_gla, norm_ffn1, norm_mix, norm_ffn2, norm_final, ffn1_w_in, ffn1_w_out, ffn2_w_in, ffn2_w_out, ab_w_in, ab_w_out, hgrn_lb_logits, hgrn_norm, ssm_conv_w, ssm_conv_b, ssm_dt_bias, ssm_a_log, ssm_d, ssm_norm, gla_w_in, gla_w_gk, gla_b_gk, gla_norm, gla_w_out):
    assert ab_w_in.shape[0] == 1 and gla_w_in.shape[0] == 1, "one HGRN2/SSD layer and one GLA layer"
    bp, lp, _ = x_prompt.shape
    bs, ls, _ = x_sample.shape
    rp, rs = bp * lp, bs * ls
    cp, cs = math.gcd(lp, CHUNK), math.gcd(ls, CHUNK)
    assert rs % cp == 0
    row = lambda v: v.reshape(1, -1)
    stack_row = lambda v: v.reshape(v.shape[0], 1, v.shape[1])
    nf1, nm, nf2 = stack_row(norm_ffn1), stack_row(norm_mix), stack_row(norm_ffn2)
    in_rows, out_rows = D_MODEL // 64, D_FF // 16
    xp = x_prompt.reshape(rp, D_MODEL)
    xs = x_sample.reshape(rs, D_MODEL)

    wp, wt = _split_cols(ab_w_in[0])
    ab_feat_params = (hgrn_lb_logits, jnp.tile(hgrn_norm[0], HGRN_HEADS).reshape(1, -1),
                      jnp.pad(ssm_dt_bias[0], (0, LANES - SSM_HEADS)).reshape(1, -1))
    casts = (_Cast(ffn2_w_in, in_rows), _Cast(ffn2_w_out, out_rows), _Cast(ffn1_w_in, in_rows, layer=1),
             _Cast(ffn1_w_out, out_rows, layer=1), _Cast(ab_w_out, in_rows, layer=0),
             _Cast(gla_w_in[0], in_rows, split=True), _Cast(gla_w_out, in_rows, layer=0))
    (x1, feat, w2_in, w2_out, w1_in, w1_out, ab_wo, gla_wp, gla_wt, gla_wo) = _pre_call(
        xp, xs, 0, nf1, ffn1_w_in[0].astype(BF16), ffn1_w_out[0].astype(BF16), nm, wp, wt, ab_feat_params, casts,
        mode="ab", row_tile=PRE0_ROWS, name="pre0")
    ab_params = (ssm_conv_w[0], row(ssm_conv_b[0]), row(ssm_a_log[0]), row(ssm_d[0]), row(ssm_norm[0]))
    o_p, hgrn_p, ssm_p, conv_p = _ab_seq_call(feat.reshape((rp + rs) // cp, cp, AB_FEAT), bp, lp, ab_params,
                                              name="mix0_prompt")
    o_s, hgrn_s, ssm_s, conv_s = _ab_par_call(feat, rp, cs, state_hgrn[0], state_ssm[0], state_conv[0], ab_params,
                                              name="mix0_sample")
    (x3,) = _post_call(x1, o_p.reshape(rp, AB_WIDTH), o_s, 0, ab_wo, nf2,
                       w2_in, w2_out, row(norm_final), split_output=False, name="post0")

    gla_feat_params = (gla_w_gk[0].astype(BF16), row(gla_b_gk[0]), jnp.tile(gla_norm[0], GLA_HEADS).reshape(1, -1))
    x4, feat, gate, v16, ready, decay = _pre_call(x3, None, 1, nf1, w1_in, w1_out, nm, gla_wp, gla_wt, gla_feat_params,
                                                  mode="gla", row_tile=PRE1_ROWS, name="pre1")
    chunked = lambda a: a.reshape((rp + rs) // cp, cp, a.shape[1])
    o_p, gla_p = _gla_seq_call(chunked(ready), decay, chunked(gate), chunked(v16), bp, lp, name="mix1_prompt")
    o_s, gla_s = _gla_par_call(feat, gate, v16, rp, cs, state_gla[0], name="mix1_sample")
    y_p, y_s = _post_call(x4, o_p.reshape(rp, GLA_VAL), o_s, 1, gla_wo, nf2,
                          w2_in, w2_out, row(norm_final), split_output=True, name="post1")

    ssm_shape = (1, -1, SSM_HEADS, SSM_HEAD_DIM, SSM_STATE)
    return (y_p.reshape(bp, lp, D_MODEL), y_s.reshape(bs, ls, D_MODEL), hgrn_p[None], hgrn_s[None],
            ssm_p.reshape(ssm_shape), ssm_s.reshape(ssm_shape), conv_p[None], conv_s[None], gla_p[None], gla_s[None])
```

```python
import functools
import math

import jax
import jax.numpy as jnp
from jax import lax
from jax.experimental import pallas as pl
from jax.experimental.pallas import tpu as pltpu

F32 = jnp.float32
BF16 = jnp.bfloat16

D_MODEL = 1024
D_FF = 2816
EPS = 1e-6
CHUNK = 64

HGRN_HEADS = 4
HGRN_HEAD_DIM = 128
HGRN_WIDTH = HGRN_HEADS * HGRN_HEAD_DIM

SSM_HEADS = 8
SSM_HEAD_DIM = 64
SSM_INNER = SSM_HEADS * SSM_HEAD_DIM
SSM_GROUPS = 2
SSM_STATE = 128
SSM_GROUP_WIDTH = SSM_INNER // SSM_GROUPS
HEADS_PER_GROUP = SSM_HEADS // SSM_GROUPS
CONV_W = 4
CONV_DIM = SSM_INNER + 2 * SSM_GROUPS * SSM_STATE
AB_WIDTH = HGRN_WIDTH + SSM_INNER

GLA_HEADS = 4
GLA_HEAD_K = 128
GLA_HEAD_V = 256
GLA_KEY = GLA_HEADS * GLA_HEAD_K
GLA_VAL = GLA_HEADS * GLA_HEAD_V
GK_RANK = 16
GK_NORMALIZER = 16.0

LANES = 128
SUBLANES = 8
VMEM_LIMIT_BYTES = 56 * 1024 * 1024

_Q0, _K0, _LF0, _V0, _G0 = (i * HGRN_WIDTH for i in range(5))
_Z0 = 5 * HGRN_WIDTH
_XBC0 = _Z0 + SSM_INNER
_DT0 = _XBC0 + CONV_DIM
AB_FEAT = _DT0 + LANES
_WQ, _WF, _WI, _WG, _WZ = (i * HGRN_WIDTH for i in range(5))
_WXBC = 4 * HGRN_WIDTH + SSM_INNER

_GQ0, _GK0, _GLF0 = 0, GLA_KEY, 2 * GLA_KEY
GLA_FEAT = 3 * GLA_KEY
_GQD0, _GKI0, _GKE0 = 0, GLA_KEY, 2 * GLA_KEY
GLA_READY = 3 * GLA_KEY
_GWQ, _GWK = 0, GLA_KEY
_GWV = 2 * GLA_KEY
_GWG = _GWV + GLA_VAL


def _rms(x, w):
    return x * lax.rsqrt(jnp.mean(x * x, axis=-1, keepdims=True) + EPS) * w


def _rms_core(x):
    return x * lax.rsqrt(jnp.mean(x * x, axis=-1, keepdims=True) + EPS)


def _silu(x):
    return x * jax.nn.sigmoid(x)


def _softplus(x):
    return jnp.maximum(x, 0.0) + jnp.log1p(jnp.exp(-jnp.abs(x)))


def _dot(a, b):
    return jnp.dot(a, b, preferred_element_type=F32)


def _bdot(a, b, ca, cb):
    return lax.dot_general(a, b, (((ca,), (cb,)), ((0,), (0,))), preferred_element_type=F32)


def _split3(x):
    hi = x.astype(BF16)
    r1 = x - hi.astype(F32)
    mid = r1.astype(BF16)
    lo = (r1 - mid.astype(F32)).astype(BF16)
    return hi, mid, lo


def _exact_bdot_lhs01(m01, x, ca, cb):
    return sum(_bdot(m01, p, ca, cb) for p in _split3(x))


def _exact_bdot_rhs01(x, m01, ca, cb):
    return sum(_bdot(p, m01, ca, cb) for p in _split3(x))


def _causal(nb, c):
    r = lax.broadcasted_iota(jnp.int32, (nb, c, c), 1)
    col = lax.broadcasted_iota(jnp.int32, (nb, c, c), 2)
    return r >= col


BF16_ROWS = 2 * SUBLANES


def _triangle3(nb, c, wide_axis):
    shape = (1, c, 3 * c) if wide_axis == 2 else (1, 3 * c, c)
    wide = lax.broadcasted_iota(jnp.int32, shape, wide_axis)
    narrow = lax.broadcasted_iota(jnp.int32, shape, 3 - wide_axis)
    hit = None
    for k in range(3):
        wk = wide - k * c
        term = (wk >= 0) & (wk < c) & (wk <= narrow)
        hit = term if hit is None else hit | term
    return jnp.broadcast_to(hit.astype(BF16), (nb,) + shape[1:])


def _chunk_cumsum(x, causal):
    nb, c, _ = x.shape
    if c % BF16_ROWS:
        return _exact_bdot_lhs01(causal.astype(BF16), x, 2, 1)
    return _bdot(_triangle3(nb, c, 2), jnp.concatenate(_split3(x), axis=1), 2, 1)


def _chunk_cumsum_t(x):
    nb, c, _ = x.shape
    if c % BF16_ROWS:
        return _exact_bdot_rhs01(x, _upper(nb, c), 1, 1)
    return _bdot(jnp.concatenate(_split3(x), axis=1), _triangle3(nb, c, 1), 1, 1)


def _col_bcast(row, lanes=LANES):
    nb, _, k = row.shape
    hi, mid, lo = (p.astype(F32) for p in _split3(row))
    r = lax.broadcasted_iota(jnp.int32, (1, BF16_ROWS, k), 1)
    stacked = jnp.where(r == 0, hi, jnp.where(r == 1, mid, jnp.where(r == 2, lo, 0.0))).astype(BF16)
    return _bdot(stacked, jnp.ones((nb, BF16_ROWS, lanes), BF16), 1, 1)


MXU_DIM = 256
FF_TILES = ((0, 6 * MXU_DIM), (6 * MXU_DIM, D_FF))
assert all(lo % MXU_DIM == 0 and hi % MXU_DIM == 0 for lo, hi in FF_TILES)


def _ffn_half(x, norm_w, w_in_ref, w_out_ref):
    hb = _rms(x, norm_w).astype(BF16)
    acc = None
    for lo, hi in FF_TILES:
        gate = _dot(hb, w_in_ref[:, lo:hi])
        up = _dot(hb, w_in_ref[:, D_FF + lo:D_FF + hi])
        act = (_silu(gate) * up).astype(BF16)
        part = _dot(act, w_out_ref[lo:hi, :])
        acc = part if acc is None else acc + part
    return 0.5 * acc


def _pick_group(first_steps, a, b):
    return jnp.where(pl.program_id(0) < first_steps, a, b)


def _cast_rows(src_ref, *dst_refs):
    if len(dst_refs) == 1:
        dst_refs[0][...] = src_ref[...].astype(BF16)
        return
    main_ref, tail_ref = dst_refs
    main = main_ref.shape[-1]
    rest = src_ref.shape[-1] - main
    main_ref[...] = src_ref[:, :main].astype(BF16)
    tail_ref[...] = jnp.zeros(tail_ref.shape, BF16)
    tail_ref[:, :rest] = src_ref[:, main:].astype(BF16)


def _ab_features(hb, wp_ref, wt_ref, lb_ref, hn_ref, dtb_ref, feat_ref):
    def group(w0):
        return _dot(hb, wp_ref[:, w0:w0 + HGRN_WIDTH])

    def put(c0, val):
        feat_ref[:, c0:c0 + val.shape[-1]] = val

    lb = _lower_bound(lb_ref[...])
    f = lb + (1.0 - lb) * jax.nn.sigmoid(group(_WF))
    put(_K0, 1.0 - f)
    put(_LF0, jnp.log(f))
    put(_Q0, _silu(group(_WQ)))
    put(_G0, _silu(group(_WG)) * hn_ref[...])
    put(_Z0, _silu(group(_WZ)))
    put(_DT0, _softplus(_dot(hb, wt_ref[...]) + dtb_ref[...]))
    put(_XBC0, _dot(hb, wp_ref[:, _WXBC:_WXBC + CONV_DIM]))
    put(_V0, group(_WI))


def _decayed_operands(q, k, log_f):
    nb, c, _ = q.shape
    g = _chunk_cumsum(log_f, _causal(nb, c))
    g_last = g[:, c - 1:c, :]
    q_dec = (q * jnp.exp(g)).astype(BF16)
    k_inv_f = k * jnp.exp(-g)
    decay = jnp.exp(g_last)
    k_end = (k_inv_f * decay).astype(BF16)
    return q_dec, k_inv_f.astype(BF16), k_end, decay, g_last


def _gla_features(hb, wp_ref, wt_ref, wgk_ref, bgk_ref, gn_ref, feat_ref, gate_ref, v_ref, ready_ref, decay_ref):
    def put(c0, val):
        feat_ref[:, c0:c0 + val.shape[-1]] = val

    rows = hb.shape[0]
    nb = rows // CHUNK
    gk_low = _dot(hb, wt_ref[...])[:, :GK_RANK].astype(BF16)
    gk = _dot(gk_low, wgk_ref[...]) + bgk_ref[...]
    log_f = -_softplus(-gk) / GK_NORMALIZER
    q = _dot(hb, wp_ref[:, _GWQ:_GWQ + GLA_KEY]) * (GLA_HEAD_K ** -0.5)
    k = _dot(hb, wp_ref[:, _GWK:_GWK + GLA_KEY])
    put(_GLF0, log_f)
    put(_GQ0, q)
    put(_GK0, k)
    chunked = lambda t: t.reshape(nb, CHUNK, GLA_KEY)
    q_dec, k_inv, k_end, decay, _ = _decayed_operands(chunked(q), chunked(k), chunked(log_f))
    for c0, val in ((_GQD0, q_dec), (_GKI0, k_inv), (_GKE0, k_end)):
        ready_ref[:, c0:c0 + GLA_KEY] = val.reshape(rows, GLA_KEY)
    decay_ref[...] = decay.reshape(nb, GLA_KEY)
    gate_ref[...] = (_silu(_dot(hb, wp_ref[:, _GWG:_GWG + GLA_VAL])) * gn_ref[...]).astype(BF16)
    v_ref[...] = _dot(hb, wp_ref[:, _GWV:_GWV + GLA_VAL]).astype(BF16)


_FEATURES = {
    "ab": (_ab_features, ((AB_FEAT, F32, 1),)),
    "gla": (_gla_features, ((GLA_FEAT, F32, 1), (GLA_VAL, BF16, 1), (GLA_VAL, BF16, 1), (GLA_READY, BF16, 1),
                            (GLA_KEY, F32, CHUNK))),
}
N_FEATURE_PARAMS = 3


def _pre_kernel(*refs, first_steps, cast_arity, mode):
    n_x = 1 if first_steps is None else 2
    x_refs, refs = refs[:n_x], refs[n_x:]
    nf_ref, w_in_ref, w_out_ref, nm_ref, wp_ref, wt_ref = refs[:6]
    feat_params, refs = refs[6:6 + N_FEATURE_PARAMS], refs[6 + N_FEATURE_PARAMS:]
    feature_fn, feature_outs = _FEATURES[mode]
    cast_src, refs = refs[:len(cast_arity)], refs[len(cast_arity):]
    x1_ref, feat_refs, cast_dst = refs[0], refs[1:1 + len(feature_outs)], list(refs[1 + len(feature_outs):])
    if first_steps is None:
        x = x_refs[0][...]
    else:
        x = _pick_group(first_steps, x_refs[0][...], x_refs[1][...].reshape(x_refs[0].shape))
    x1 = x + _ffn_half(x, nf_ref[...], w_in_ref, w_out_ref)
    x1_ref[...] = x1
    hb = _rms(x1, nm_ref[...]).astype(BF16)
    feature_fn(hb, wp_ref, wt_ref, *feat_params, *feat_refs)
    for src_ref, arity in zip(cast_src, cast_arity):
        _cast_rows(src_ref, *cast_dst[:arity])
        cast_dst = cast_dst[arity:]


def _post_kernel(x_ref, oa_ref, ob_ref, wo_ref, nf_ref, w_in_ref, w_out_ref, nfin_ref, *y_refs, first_steps):
    o = _pick_group(first_steps, oa_ref[...].astype(BF16), ob_ref[...].astype(BF16))
    x2 = x_ref[...] + _dot(o, wo_ref[...])
    y = x2 + _ffn_half(x2, nf_ref[...], w_in_ref, w_out_ref)
    if len(y_refs) == 1:
        y_refs[0][...] = y
    else:
        y = _rms(y, nfin_ref[...])
        ya_ref, yb_ref = y_refs

        @pl.when(pl.program_id(0) < first_steps)
        def _():
            ya_ref[...] = y

        @pl.when(pl.program_id(0) >= first_steps)
        def _():
            yb_ref[...] = y.reshape(yb_ref.shape)


def _resident(shape, layer=None):
    if layer is None:
        return pl.BlockSpec(shape, lambda *_: (0,) * len(shape), pipeline_mode=pl.Buffered(1))
    return pl.BlockSpec((None,) + tuple(shape[1:]), lambda *_: (layer,) + (0,) * (len(shape) - 1),
                        pipeline_mode=pl.Buffered(1))


def _row_tile(rows, want):
    t = min(rows, want)
    assert rows % t == 0
    return t


PRE0_ROWS = 256
PRE1_ROWS = 512
POST_ROWS = 512


def _group_specs(tm, first_steps, width, second_len=None):
    first = pl.BlockSpec((tm, width), lambda i: (jnp.minimum(i, first_steps - 1), 0))
    if second_len is None:
        second = pl.BlockSpec((tm, width), lambda i: (jnp.maximum(i - first_steps, 0), 0))
    else:
        second = pl.BlockSpec((tm // second_len, second_len, width), lambda i: (jnp.maximum(i - first_steps, 0), 0, 0))
    return first, second


def _weight_spec(w, layer):
    return _resident(w.shape, layer if w.ndim == 3 else None)


class _Cast:
    def __init__(self, src, rows_per_step, layer=None, split=False):
        self.src, self.rps, self.layer, self.split = src, rows_per_step, layer, split
        rows = src.shape[-2]
        assert rows % rows_per_step == 0 and rows_per_step % (2 * SUBLANES) == 0
        self.steps = rows // rows_per_step
        assert not split or layer is not None or src.ndim == 2

    def _index(self, lead):
        last = self.steps - 1
        return lambda i: lead + (jnp.minimum(i, last), 0)

    def in_spec(self):
        cols = self.src.shape[-1]
        if self.src.ndim == 2:
            return pl.BlockSpec((self.rps, cols), self._index(()))
        if self.layer is None:
            return pl.BlockSpec((self.src.shape[0], self.rps, cols), self._index((0,)))
        return pl.BlockSpec((None, self.rps, cols), self._index((self.layer,)))

    def outs(self):
        rows, cols = self.src.shape[-2:]
        if self.src.ndim == 3 and self.layer is None:
            n = self.src.shape[0]
            return [(jax.ShapeDtypeStruct((n, rows, cols), BF16), pl.BlockSpec((n, self.rps, cols), self._index((0,))))]
        widths = [cols // LANES * LANES, LANES] if self.split else [cols]
        return [(jax.ShapeDtypeStruct((rows, w), BF16), pl.BlockSpec((self.rps, w), self._index(()))) for w in widths]


def _pre_call(xa, xb, layer, nf, w_in, w_out, nm, wp, wt, feat_params, casts=(), *, mode, row_tile, name):
    ra = xa.shape[0]
    rb = 0 if xb is None else math.prod(xb.shape[:-1])
    rows = ra + rb
    tm = _row_tile(rb if rb else ra, row_tile)
    assert ra % tm == 0
    assert all(cast.steps <= rows // tm for cast in casts)
    assert len(feat_params) == N_FEATURE_PARAMS
    feature_outs = _FEATURES[mode][1]
    tok = lambda n: pl.BlockSpec((tm, n), lambda i: (i, 0))
    if xb is None:
        first_steps, x_specs, xs = None, [tok(D_MODEL)], (xa,)
    else:
        first_steps = ra // tm
        x_specs, xs = list(_group_specs(tm, first_steps, D_MODEL, xb.shape[1] if xb.ndim == 3 else None)), (xa, xb)
    cast_outs = [cast.outs() for cast in casts]
    flat_outs = [o for outs in cast_outs for o in outs]
    return pl.pallas_call(
        functools.partial(_pre_kernel, first_steps=first_steps, cast_arity=tuple(len(o) for o in cast_outs),
                          mode=mode),
        grid=(rows // tm,),
        in_specs=x_specs + [_resident(nf.shape, layer), _weight_spec(w_in, layer), _weight_spec(w_out, layer),
                            _resident(nm.shape, layer), _resident(wp.shape), _resident(wt.shape)]
                         + [_resident(p.shape) for p in feat_params] + [cast.in_spec() for cast in casts],
        out_specs=[tok(D_MODEL)] + [pl.BlockSpec((tm // div, w), lambda i: (i, 0)) for w, _, div in feature_outs]
                  + [spec for _, spec in flat_outs],
        out_shape=[jax.ShapeDtypeStruct((rows, D_MODEL), F32)]
                  + [jax.ShapeDtypeStruct((rows // div, w), dt) for w, dt, div in feature_outs]
                  + [shape for shape, _ in flat_outs],
        compiler_params=pltpu.CompilerParams(dimension_semantics=("arbitrary",), vmem_limit_bytes=VMEM_LIMIT_BYTES),
        name=name,
    )(*xs, nf, w_in, w_out, nm, wp, wt, *feat_params, *[cast.src for cast in casts])


def _post_call(x, oa, ob, layer, wo, nf, w_in, w_out, nfin, *, split_output, name, second_len=None):
    rows = x.shape[0]
    ra, rb = oa.shape[0], ob.shape[0]
    assert ra + rb == rows
    tm = _row_tile(rb, POST_ROWS)
    assert ra % tm == 0
    first_steps = ra // tm
    tok = lambda n: pl.BlockSpec((tm, n), lambda i: (i, 0))
    spec_a, spec_b = _group_specs(tm, first_steps, oa.shape[1])
    if split_output:
        out_specs = list(_group_specs(tm, first_steps, D_MODEL, second_len))
        second_shape = (rb, D_MODEL) if second_len is None else (rb // second_len, second_len, D_MODEL)
        out_shape = [jax.ShapeDtypeStruct((ra, D_MODEL), F32), jax.ShapeDtypeStruct(second_shape, F32)]
    else:
        out_specs = [tok(D_MODEL)]
        out_shape = [jax.ShapeDtypeStruct((rows, D_MODEL), F32)]
    return pl.pallas_call(
        functools.partial(_post_kernel, first_steps=first_steps),
        grid=(rows // tm,),
        in_specs=[tok(D_MODEL), spec_a, spec_b, _resident(wo.shape), _resident(nf.shape, layer),
                  _resident(w_in.shape, layer), _resident(w_out.shape, layer), _resident(nfin.shape)],
        out_specs=out_specs,
        out_shape=out_shape,
        compiler_params=pltpu.CompilerParams(dimension_semantics=("arbitrary",), vmem_limit_bytes=VMEM_LIMIT_BYTES),
        name=name,
    )(x, oa, ob, wo, nf, w_in, w_out, nfin)


def _gla_heads(q, k, v, log_f, gate, n_heads, dk, dv, causal, sequential, read_state, write_state,
               o_ref, o_col0):
    q_dec, k_inv, k_end, decay, g_last = _decayed_operands(q, k, log_f)
    _gla_core(q_dec, k_inv, k_end, decay, g_last, v, gate, n_heads, dk, dv, causal, sequential, read_state,
              write_state, o_ref, o_col0)


def _gla_core(q_dec, k_inv, k_end, decay, g_last, v, gate, n_heads, dk, dv, causal, sequential, read_state,
              write_state, o_ref, o_col0):
    nb, c, _ = q_dec.shape
    vb = v.astype(BF16)
    heads = range(n_heads)
    ks = [slice(h * dk, (h + 1) * dk) for h in heads]
    vs = [slice(h * dv, (h + 1) * dv) for h in heads]
    scores = [_bdot(q_dec[:, :, ks[h]], k_inv[:, :, ks[h]], 2, 2) for h in heads]
    scores = [jnp.where(causal, sc, 0.0).astype(BF16) for sc in scores]
    o_intra = [_bdot(scores[h], vb[:, :, vs[h]], 2, 1) for h in heads]

    def emit(h, b, o):
        cols = slice(o_col0 + h * dv, o_col0 + (h + 1) * dv)
        o_ref[b, :, cols] = (_rms_core(o) * gate[b, :, vs[h]]).astype(o_ref.dtype)

    if sequential:
        kv_t = [_bdot(vb[:, :, vs[h]], k_end[:, :, ks[h]], 1, 1) for h in heads]
        for b in range(nb):
            for h in heads:
                s_t = read_state(h)
                o_inter = lax.dot_general(q_dec[b, :, ks[h]], s_t.astype(BF16), (((1,), (1,)), ((), ())),
                                          preferred_element_type=F32)
                write_state(h, decay[b, :, ks[h]] * s_t + kv_t[h][b])
                emit(h, b, o_intra[h][b] + o_inter)
    else:
        for h in heads:
            kv = _bdot(k_end[:, :, ks[h]], vb[:, :, vs[h]], 1, 1)
            decay_col = jnp.exp(_col_bcast(g_last[:, :, ks[h]]))
            decay_col = jnp.concatenate([decay_col] * (dv // LANES), axis=-1)
            s0 = read_state(h)
            o_inter = _bdot(q_dec[:, :, ks[h]], s0.astype(BF16), 2, 1)
            write_state(h, decay_col * s0 + kv)
            emit(h, slice(None), o_intra[h] + o_inter)


def _expand_heads(x, expand):
    return _exact_bdot_rhs01(x, expand, 2, 1)


def _ssd_heads(xs, bs, cs, z, dt, a_row, d_row, norm_w, causal, sequential, read_state, write_state,
               o_ref, o_col0):
    nb, c, _ = xs.shape
    hrow = lax.broadcasted_iota(jnp.int32, (SSM_HEADS, SSM_INNER), 0)
    hcol = lax.broadcasted_iota(jnp.int32, (SSM_HEADS, SSM_INNER), 1) // SSM_HEAD_DIM
    expand2d = (hrow == hcol).astype(BF16)
    expand = jnp.broadcast_to(expand2d[None], (nb, SSM_HEADS, SSM_INNER))
    d_x = sum(_dot(p, expand2d) for p in _split3(d_row))
    dta = dt * a_row
    cum = _chunk_cumsum(dta, causal)
    cum_t = _chunk_cumsum_t(dta)
    dt_x = _expand_heads(dt, expand)
    cum_x = _expand_heads(cum, expand)
    xdt = xs * dt_x
    cum_last = cum_x[:, c - 1:c, :]
    x_end = (xdt * jnp.exp(cum_last - cum_x)).astype(BF16)
    chunk_dec = jnp.exp(cum_x)
    bsb = bs.astype(BF16)
    csb = cs.astype(BF16)
    lane_head = lax.broadcasted_iota(jnp.int32, (nb, c, SSM_GROUP_WIDTH), 2) // SSM_HEAD_DIM
    groups = range(SSM_GROUPS)
    gl = [slice(g * SSM_STATE, (g + 1) * SSM_STATE) for g in groups]
    hl = [slice(g * SSM_GROUP_WIDTH, (g + 1) * SSM_GROUP_WIDTH) for g in groups]
    cb = [_bdot(csb[:, :, gl[g]], bsb[:, :, gl[g]], 2, 2) for g in groups]
    y_intra = []
    for g in groups:
        xdt_g = xdt[:, :, hl[g]]
        y = None
        for r in range(HEADS_PER_GROUP):
            h = g * HEADS_PER_GROUP + r
            col = jnp.broadcast_to(cum[:, :, h:h + 1], (nb, c, c))
            row = cum_t[:, h:h + 1, :]
            dec = jnp.where(causal, jnp.exp(col - row), 0.0)
            lmat = (cb[g] * dec).astype(BF16)
            xm = jnp.where(lane_head == r, xdt_g, 0.0).astype(BF16)
            part = _bdot(lmat, xm, 2, 1)
            y = part if y is None else y + part
        y_intra.append(y)

    def finish(g, y_in, y_inter, b):
        y_all = y_in + y_inter * chunk_dec[b, :, hl[g]] + d_x[:, hl[g]] * xs[b, :, hl[g]]
        y_all = _rms(y_all * z[b, :, hl[g]], norm_w[:, hl[g]])
        cols = slice(o_col0 + g * SSM_GROUP_WIDTH, o_col0 + (g + 1) * SSM_GROUP_WIDTH)
        o_ref[b, :, cols] = y_all.astype(o_ref.dtype)

    if sequential:
        kv_t = [_bdot(bsb[:, :, gl[g]], x_end[:, :, hl[g]], 1, 1) for g in groups]
        for b in range(nb):
            for g in groups:
                s_t = read_state(g)
                y_inter = _dot(csb[b, :, gl[g]], s_t.astype(BF16))
                write_state(g, chunk_dec[b, c - 1:c, hl[g]] * s_t + kv_t[g][b])
                finish(g, y_intra[g][b], y_inter, b)
    else:
        for g in groups:
            kv = _bdot(x_end[:, :, hl[g]], bsb[:, :, gl[g]], 1, 1)
            decay = jnp.exp(_col_bcast(cum_last[:, :, hl[g]], SSM_STATE))
            s0 = read_state(g)
            y_inter = _bdot(csb[:, :, gl[g]], s0.astype(BF16), 2, 2)
            write_state(g, decay * s0 + kv)
            finish(g, y_intra[g], y_inter, slice(None))


def _upper(nb, c):
    r = lax.broadcasted_iota(jnp.int32, (nb, c, c), 1)
    col = lax.broadcasted_iota(jnp.int32, (nb, c, c), 2)
    return (r <= col).astype(BF16)


def _lower_bound(lb_logits):
    m = jnp.max(lb_logits, axis=0, keepdims=True)
    e = jnp.exp(lb_logits - m)
    return e[0:1, :] / jnp.sum(e, axis=0, keepdims=True)


def _ab_math(feat, conv, alog_ref, d_ref, sn_ref, sequential, read_h, write_h, read_s, write_s, o_ref):
    nb, c, _ = o_ref.shape
    causal = _causal(nb, c)
    w = HGRN_WIDTH
    _gla_heads(feat[:, :, _Q0:_Q0 + w], feat[:, :, _K0:_K0 + w], feat[:, :, _V0:_V0 + w], feat[:, :, _LF0:_LF0 + w],
               feat[:, :, _G0:_G0 + w], HGRN_HEADS, HGRN_HEAD_DIM, HGRN_HEAD_DIM, causal, sequential,
               read_h, write_h, o_ref, 0)
    act = _silu(conv)
    _ssd_heads(act[:, :, :SSM_INNER], act[:, :, SSM_INNER:SSM_INNER + SSM_GROUPS * SSM_STATE],
               act[:, :, SSM_INNER + SSM_GROUPS * SSM_STATE:], feat[:, :, _Z0:_Z0 + SSM_INNER],
               feat[:, :, _DT0:_DT0 + SSM_HEADS], -jnp.exp(alog_ref[...]), d_ref[...], sn_ref[...], causal,
               sequential, read_s, write_s, o_ref, HGRN_WIDTH)


def _ab_seq_kernel(proj_ref, cw_ref, cbias_ref, alog_ref, d_ref, sn_ref,
                   o_ref, sh_out, ss_out, sc_out, sh, ss, xpad):
    t = pl.program_id(1)
    nb, c, _ = o_ref.shape
    rows = nb * c

    @pl.when(t == 0)
    def _():
        sh[...] = jnp.zeros_like(sh)
        ss[...] = jnp.zeros_like(ss)
        xpad[0:SUBLANES, :] = jnp.zeros((SUBLANES, CONV_DIM), F32)

    xpad[SUBLANES:SUBLANES + rows, :] = proj_ref[:, :, _XBC0:_XBC0 + CONV_DIM].reshape(rows, CONV_DIM)
    padded = xpad[...]
    conv = cbias_ref[...] + padded[SUBLANES:] * cw_ref[CONV_W - 1:CONV_W, :]
    for d in range(1, CONV_W):
        conv = conv + pltpu.roll(padded, d, 0)[SUBLANES:] * cw_ref[CONV_W - 1 - d:CONV_W - d, :]
    xpad[0:SUBLANES, :] = padded[rows:rows + SUBLANES]
    conv = conv.reshape(nb, c, CONV_DIM)

    def read_h(h):
        return sh[h]

    def write_h(h, s):
        sh[h] = s

    def read_s(g):
        return ss[g]

    def write_s(g, s):
        ss[g] = s

    _ab_math(proj_ref, conv, alog_ref, d_ref, sn_ref, True, read_h, write_h, read_s, write_s, o_ref)

    @pl.when(t == pl.num_programs(1) - 1)
    def _():
        for h in range(HGRN_HEADS):
            sh_out[0, h] = sh[h].T
        for g in range(SSM_GROUPS):
            ss_out[0, g] = ss[g].T
        sc_out[0] = xpad[SUBLANES - (CONV_W - 1):SUBLANES, :]


def _ab_par_kernel(proj_ref, sh_in, ss_in, sc_in, cw_ref, cbias_ref, alog_ref, d_ref, sn_ref,
                   o_ref, sh_out, ss_out, sc_out, xpad, o3):
    nb, c, _ = o3.shape
    nbuf = CONV_W - 1
    proj = proj_ref[...].reshape(nb, c, proj_ref.shape[-1])
    xpad[:, 0:nbuf, :] = sc_in[...]
    xpad[:, nbuf:nbuf + c, :] = proj[:, :, _XBC0:_XBC0 + CONV_DIM]
    conv = cbias_ref[...]
    for k in range(CONV_W):
        conv = conv + xpad[:, k:k + c, :] * cw_ref[k:k + 1, :]
    sc_out[...] = xpad[:, c:c + nbuf, :]

    def read_h(h):
        return sh_in[:, h]

    def write_h(h, s):
        sh_out[:, h] = s

    def read_s(g):
        return ss_in[:, g]

    def write_s(g, s):
        ss_out[:, g] = s

    _ab_math(proj, conv, alog_ref, d_ref, sn_ref, False, read_h, write_h, read_s, write_s, o3)
    o_ref[...] = o3[...].reshape(o_ref.shape)


def _gla_seq_kernel(ready_ref, decay_ref, gate_ref, v_ref, o_ref, sg_out, sg):
    t = pl.program_id(1)

    @pl.when(t == 0)
    def _():
        sg[...] = jnp.zeros_like(sg)

    def read_g(h):
        return sg[h]

    def write_g(h, s):
        sg[h] = s

    nb, c, _ = o_ref.shape
    _gla_core(ready_ref[:, :, _GQD0:_GQD0 + GLA_KEY], ready_ref[:, :, _GKI0:_GKI0 + GLA_KEY],
              ready_ref[:, :, _GKE0:_GKE0 + GLA_KEY], decay_ref[...].reshape(nb, 1, GLA_KEY), None, v_ref[...],
              gate_ref, GLA_HEADS, GLA_HEAD_K, GLA_HEAD_V, _causal(nb, c), True, read_g, write_g, o_ref, 0)

    @pl.when(t == pl.num_programs(1) - 1)
    def _():
        for h in range(GLA_HEADS):
            sg_out[0, h] = sg[h].T


def _gla_par_kernel(proj_ref, gate_ref, v_ref, sg_in, o_ref, sg_out, o3):
    def read_g(h):
        return sg_in[:, h]

    def write_g(h, s):
        sg_out[:, h] = s

    nb, c, _ = o3.shape
    feat = proj_ref[...].reshape(nb, c, proj_ref.shape[-1])
    gate = gate_ref[...].astype(F32).reshape(nb, c, gate_ref.shape[-1])
    v = v_ref[...].astype(F32).reshape(nb, c, v_ref.shape[-1])
    _gla_heads(feat[:, :, _GQ0:_GQ0 + GLA_KEY], feat[:, :, _GK0:_GK0 + GLA_KEY], v,
               feat[:, :, _GLF0:_GLF0 + GLA_KEY], gate, GLA_HEADS, GLA_HEAD_K, GLA_HEAD_V, _causal(nb, c), False,
               read_g, write_g, o3, 0)
    o_ref[...] = o3[...].reshape(o_ref.shape)


SEQ_TILE_CHUNKS = 8
GLA_SEQ_TILE_CHUNKS = 16
PAR_TILE_SEQS = 16


def _full(shape):
    return pl.BlockSpec(shape, lambda *_: (0,) * len(shape))


def _mixer_o_dtype(c):
    return BF16 if c % (2 * SUBLANES) == 0 else F32


def _ab_seq_call(proj, batch, length, params, *, name):
    c = math.gcd(length, CHUNK)
    nb = SEQ_TILE_CHUNKS
    tiles = length // (c * nb)
    assert tiles * c * nb == length
    width = HEADS_PER_GROUP * SSM_HEAD_DIM
    blk = lambda n: pl.BlockSpec((nb, c, n), lambda b, t: (b * tiles + t, 0, 0))
    out_shapes = [
        jax.ShapeDtypeStruct((batch * length // c, c, AB_WIDTH), _mixer_o_dtype(c)),
        jax.ShapeDtypeStruct((batch, HGRN_HEADS, HGRN_HEAD_DIM, HGRN_HEAD_DIM), F32),
        jax.ShapeDtypeStruct((batch, SSM_GROUPS, width, SSM_STATE), F32),
        jax.ShapeDtypeStruct((batch, CONV_W - 1, CONV_DIM), F32),
    ]
    out_specs = [
        blk(AB_WIDTH),
        pl.BlockSpec((1, HGRN_HEADS, HGRN_HEAD_DIM, HGRN_HEAD_DIM), lambda b, t: (b, 0, 0, 0)),
        pl.BlockSpec((1, SSM_GROUPS, width, SSM_STATE), lambda b, t: (b, 0, 0, 0)),
        pl.BlockSpec((1, CONV_W - 1, CONV_DIM), lambda b, t: (b, 0, 0)),
    ]
    return pl.pallas_call(
        _ab_seq_kernel,
        grid=(batch, tiles),
        in_specs=[blk(AB_FEAT)] + [_full(p.shape) for p in params],
        out_specs=out_specs,
        out_shape=out_shapes,
        scratch_shapes=[
            pltpu.VMEM((HGRN_HEADS, HGRN_HEAD_DIM, HGRN_HEAD_DIM), F32),
            pltpu.VMEM((SSM_GROUPS, SSM_STATE, width), F32),
            pltpu.VMEM((nb * c + SUBLANES, CONV_DIM), F32),
        ],
        compiler_params=pltpu.CompilerParams(dimension_semantics=("arbitrary", "arbitrary"),
                                             vmem_limit_bytes=VMEM_LIMIT_BYTES),
        name=name,
    )(proj, *params)


def _ab_par_call(proj, row0, c, s_hgrn, s_ssm, s_conv, params, *, name):
    batch = s_hgrn.shape[0]
    nb = _row_tile(batch, PAR_TILE_SEQS)
    rows = nb * c
    assert row0 % rows == 0
    width = HEADS_PER_GROUP * SSM_HEAD_DIM
    s_ssm = s_ssm.reshape(batch, SSM_GROUPS, width, SSM_STATE)
    blk3 = lambda a, n: pl.BlockSpec((nb, a, n), lambda b: (b, 0, 0))
    blk4 = lambda a, r, n: pl.BlockSpec((nb, a, r, n), lambda b: (b, 0, 0, 0))
    state_specs = [blk4(HGRN_HEADS, HGRN_HEAD_DIM, HGRN_HEAD_DIM), blk4(SSM_GROUPS, width, SSM_STATE),
                   blk3(CONV_W - 1, CONV_DIM)]
    out_shapes = [
        jax.ShapeDtypeStruct((batch * c, AB_WIDTH), F32),
        jax.ShapeDtypeStruct(s_hgrn.shape, F32),
        jax.ShapeDtypeStruct(s_ssm.shape, F32),
        jax.ShapeDtypeStruct(s_conv.shape, F32),
    ]
    return pl.pallas_call(
        _ab_par_kernel,
        grid=(batch // nb,),
        in_specs=[pl.BlockSpec((rows, proj.shape[1]), lambda b: (row0 // rows + b, 0))] + state_specs
                 + [_full(p.shape) for p in params],
        out_specs=[pl.BlockSpec((rows, AB_WIDTH), lambda b: (b, 0))] + state_specs,
        out_shape=out_shapes,
        scratch_shapes=[pltpu.VMEM((nb, c + CONV_W - 1, CONV_DIM), F32), pltpu.VMEM((nb, c, AB_WIDTH), F32)],
        compiler_params=pltpu.CompilerParams(dimension_semantics=("arbitrary",),
                                             vmem_limit_bytes=VMEM_LIMIT_BYTES),
        name=name,
    )(proj, s_hgrn, s_ssm, s_conv, *params)


def _gla_seq_call(ready, decay, gate, v, batch, length, *, name):
    c = math.gcd(length, CHUNK)
    assert c == CHUNK
    nb = GLA_SEQ_TILE_CHUNKS
    tiles = length // (c * nb)
    assert tiles * c * nb == length
    blk = lambda n: pl.BlockSpec((nb, c, n), lambda b, t: (b * tiles + t, 0, 0))
    return pl.pallas_call(
        _gla_seq_kernel,
        grid=(batch, tiles),
        in_specs=[blk(GLA_READY), pl.BlockSpec((nb, GLA_KEY), lambda b, t: (b * tiles + t, 0)), blk(GLA_VAL),
                  blk(GLA_VAL)],
        out_specs=[blk(GLA_VAL),
                   pl.BlockSpec((1, GLA_HEADS, GLA_HEAD_K, GLA_HEAD_V), lambda b, t: (b, 0, 0, 0))],
        out_shape=[jax.ShapeDtypeStruct((batch * length // c, c, GLA_VAL), _mixer_o_dtype(c)),
                   jax.ShapeDtypeStruct((batch, GLA_HEADS, GLA_HEAD_K, GLA_HEAD_V), F32)],
        scratch_shapes=[pltpu.VMEM((GLA_HEADS, GLA_HEAD_V, GLA_HEAD_K), F32)],
        compiler_params=pltpu.CompilerParams(dimension_semantics=("arbitrary", "arbitrary"),
                                             vmem_limit_bytes=VMEM_LIMIT_BYTES),
        name=name,
    )(ready, decay, gate, v)


def _gla_par_call(proj, gate, v, row0, c, s_gla, *, name):
    batch = s_gla.shape[0]
    nb = _row_tile(batch, PAR_TILE_SEQS)
    rows = nb * c
    assert row0 % rows == 0
    sspec = pl.BlockSpec((nb, GLA_HEADS, GLA_HEAD_K, GLA_HEAD_V), lambda b: (b, 0, 0, 0))
    return pl.pallas_call(
        _gla_par_kernel,
        grid=(batch // nb,),
        in_specs=[pl.BlockSpec((rows, a.shape[1]), lambda b: (row0 // rows + b, 0)) for a in (proj, gate, v)] + [sspec],
        out_specs=[pl.BlockSpec((rows, GLA_VAL), lambda b: (b, 0)), sspec],
        out_shape=[jax.ShapeDtypeStruct((batch * c, GLA_VAL), F32), jax.ShapeDtypeStruct(s_gla.shape, F32)],
        scratch_shapes=[pltpu.VMEM((nb, c, GLA_VAL), F32)],
        compiler_params=pltpu.CompilerParams(dimension_semantics=("arbitrary",),
                                             vmem_limit_bytes=VMEM_LIMIT_BYTES),
        name=name,
    )(proj, gate, v, s_gla)


def _split_cols(w):
    main = w.shape[1] // LANES * LANES
    tail = jnp.pad(w[:, main:].astype(BF16), ((0, 0), (0, LANES - (w.shape[1] - main))))
    return w[:, :main].astype(BF16), tail


def kernel(x_prompt, x_sample, state_hgrn, state_ssm, state_conv, state_gla, norm_ffn1, norm_mix, norm_ffn2, norm_final, ffn1_w_in, ffn1_w_out, ffn2_w_in, ffn2_w_out, ab_w_in, ab_w_out, hgrn_lb_logits, hgrn_norm, ssm_conv_w, ssm_conv_b, ssm_dt_bias, ssm_a_log, ssm_d, ssm_norm, gla_w_in, gla_w_gk, gla_b_gk, gla_norm, gla_w_out):
    assert ab_w_in.shape[0] == 1 and gla_w_in.shape[0] == 1, "one HGRN2/SSD layer and one GLA layer"
    bp, lp, _ = x_prompt.shape
    bs, ls, _ = x_sample.shape
    rp, rs = bp * lp, bs * ls
    cp, cs = math.gcd(lp, CHUNK), math.gcd(ls, CHUNK)
    assert rs % cp == 0
    row = lambda v: v.reshape(1, -1)
    stack_row = lambda v: v.reshape(v.shape[0], 1, v.shape[1])
    nf1, nm, nf2 = stack_row(norm_ffn1), stack_row(norm_mix), stack_row(norm_ffn2)
    in_rows, out_rows = D_MODEL // 64, D_FF // 16
    xp = x_prompt.reshape(rp, D_MODEL)
    xs = x_sample

    wp, wt = _split_cols(ab_w_in[0])
    ab_feat_params = (hgrn_lb_logits, jnp.tile(hgrn_norm[0], HGRN_HEADS).reshape(1, -1),
                      jnp.pad(ssm_dt_bias[0], (0, LANES - SSM_HEADS)).reshape(1, -1))
    casts = (_Cast(ffn2_w_in, in_rows), _Cast(ffn2_w_out, out_rows), _Cast(ffn1_w_in, in_rows, layer=1),
             _Cast(ffn1_w_out, out_rows, layer=1), _Cast(ab_w_out, in_rows, layer=0),
             _Cast(gla_w_in[0], in_rows, split=True), _Cast(gla_w_out, in_rows, layer=0))
    (x1, feat, w2_in, w2_out, w1_in, w1_out, ab_wo, gla_wp, gla_wt, gla_wo) = _pre_call(
        xp, xs, 0, nf1, ffn1_w_in[0].astype(BF16), ffn1_w_out[0].astype(BF16), nm, wp, wt, ab_feat_params, casts,
        mode="ab", row_tile=PRE0_ROWS, name="pre0")
    ab_params = (ssm_conv_w[0], row(ssm_conv_b[0]), row(ssm_a_log[0]), row(ssm_d[0]), row(ssm_norm[0]))
    o_p, hgrn_p, ssm_p, conv_p = _ab_seq_call(feat.reshape((rp + rs) // cp, cp, AB_FEAT), bp, lp, ab_params,
                                              name="mix0_prompt")
    o_s, hgrn_s, ssm_s, conv_s = _ab_par_call(feat, rp, cs, state_hgrn[0], state_ssm[0], state_conv[0], ab_params,
                                              name="mix0_sample")
    (x3,) = _post_call(x1, o_p.reshape(rp, AB_WIDTH), o_s, 0, ab_wo, nf2,
                       w2_in, w2_out, row(norm_final), split_output=False, name="post0")

    gla_feat_params = (gla_w_gk[0].astype(BF16), row(gla_b_gk[0]), jnp.tile(gla_norm[0], GLA_HEADS).reshape(1, -1))
    x4, feat, gate, v16, ready, decay = _pre_call(x3, None, 1, nf1, w1_in, w1_out, nm, gla_wp, gla_wt, gla_feat_params,
                                                  mode="gla", row_tile=PRE1_ROWS, name="pre1")
    chunked = lambda a: a.reshape((rp + rs) // cp, cp, a.shape[1])
    o_p, gla_p = _gla_seq_call(chunked(ready), decay, chunked(gate), chunked(v16), bp, lp, name="mix1_prompt")
    o_s, gla_s = _gla_par_call(feat, gate, v16, rp, cs, state_gla[0], name="mix1_sample")
    y_p, y_s = _post_call(x4, o_p.reshape(rp, GLA_VAL), o_s, 1, gla_wo, nf2,
                          w2_in, w2_out, row(norm_final), split_output=True, second_len=ls, name="post1")

    ssm_shape = (1, -1, SSM_HEADS, SSM_HEAD_DIM, SSM_STATE)
    return (y_p.reshape(bp, lp, D_MODEL), y_s, hgrn_p[None], hgrn_s[None],
            ssm_p.reshape(ssm_shape), ssm_s.reshape(ssm_shape), conv_p[None], conv_s[None], gla_p[None], gla_s[None])
```
